```python
import jax, jax.numpy as jnp
from jax import lax
import numpy as np

D_MODEL = 1024
BATCH = 8
SEQ = 2048
DEPTH = 4
DEC_BATCH = 32
DEC_SEQ = 4
PAST_LEN = 8192
PAGE_SIZE = 128

HEAD_DIM = 64
RWKV_HEADS = 4
RWKV_DIM = RWKV_HEADS * HEAD_DIM
RWKV_W_RANK = 64
RWKV_A_RANK = 64
RWKV_G_RANK = 128
RWKV_PROJ = 3 * RWKV_DIM + RWKV_W_RANK + RWKV_A_RANK + RWKV_G_RANK
RWKV_GN_EPS = 64e-5
ATT_HEADS = 6
ATT_DIM = ATT_HEADS * HEAD_DIM
DILATED_BRANCHES = ((128, 1), (512, 4), (2048, 16))
MAX_WINDOW = 2048
SSM_HEADS = 6
SSM_DIM = SSM_HEADS * HEAD_DIM
SSM_GROUPS = 2
SSM_STATE = 128
SSM_CONV = 4
SSM_CHUNK = 128
SSM_CONV_DIM = SSM_DIM + 2 * SSM_GROUPS * SSM_STATE
PROJ_DIM = RWKV_PROJ + 3 * ATT_DIM + SSM_DIM + SSM_CONV_DIM + SSM_HEADS
D_MIX = RWKV_DIM + ATT_DIM + SSM_DIM
FFN_DIM = 2816
NORM_EPS = 1e-6
NEG_INF = -1e30

kernel_name = "hymba_rwkv7_dilated_ssd_decoder_step"


def rms_norm(x, w):
    xf = x.astype(jnp.float32)
    y = xf * lax.rsqrt(jnp.mean(xf * xf, axis=-1, keepdims=True) + NORM_EPS)
    return y.astype(x.dtype) * w


def swiglu(h, wg, wu, wd):
    return (jax.nn.silu(h @ wg) * (h @ wu)) @ wd


def alibi_slopes():
    return 2.0 ** (-8.0 * jnp.arange(1, ATT_HEADS + 1, dtype=jnp.float32) / ATT_HEADS)


def rwkv7_recurrence(r, w, k, v, kk, a, S0):
    def step(S, inp):
        r_t, w_t, k_t, v_t, kk_t, a_t = inp
        sa = jnp.einsum("bhij,bhj->bhi", S, -kk_t)
        S = (S * w_t[:, :, None, :] + sa[..., None] * (kk_t * a_t)[:, :, None, :]
             + v_t[..., None] * k_t[:, :, None, :])
        return S, jnp.einsum("bhij,bhj->bhi", S, r_t)
    xs = tuple(jnp.moveaxis(t, 1, 0) for t in (r, w, k, v, kk, a))
    S_final, ys = lax.scan(step, S0, xs)
    return jnp.moveaxis(ys, 0, 1), S_final


def rwkv7_mix(p, shift_prev, S0, lp):
    B_, T, _ = p.shape
    f32 = jnp.float32
    p_prev = jnp.concatenate([shift_prev[:, None, :].astype(p.dtype), p[:, :-1]], axis=1)
    ps = p + lp["rwkv_mu"] * (p_prev - p)
    o = np.cumsum([RWKV_DIM, RWKV_DIM, RWKV_DIM, RWKV_W_RANK, RWKV_A_RANK]).tolist()
    r, k, v, w_lr, a_lr, g_lr = jnp.split(ps, o, axis=-1)
    w_log = -jax.nn.softplus(-(lp["rwkv_w0"] + jnp.tanh(w_lr) @ lp["rwkv_w2"])) - 0.5
    decay = jnp.exp(-jnp.exp(w_log.astype(f32)))
    a = jax.nn.sigmoid(lp["rwkv_a0"] + a_lr @ lp["rwkv_a2"])
    g = jax.nn.sigmoid(g_lr) @ lp["rwkv_g2"]
    heads = lambda t: t.reshape(B_, T, RWKV_HEADS, HEAD_DIM).astype(f32)
    kk = heads(k * lp["rwkv_k_k"])
    kk = kk / jnp.maximum(jnp.sqrt(jnp.sum(kk * kk, axis=-1, keepdims=True)), 1e-12)
    k = k * (1.0 + (a - 1.0) * lp["rwkv_k_a"])
    rh, kh, vh, ah, wh = heads(r), heads(k), heads(v), heads(a), heads(decay)
    y, S_final = rwkv7_recurrence(rh, wh, kh, vh, kk, ah, S0.astype(f32))
    mu = jnp.mean(y, axis=-1, keepdims=True)
    var = jnp.mean(jnp.square(y - mu), axis=-1, keepdims=True)
    y = ((y - mu) * lax.rsqrt(var + RWKV_GN_EPS)).reshape(B_, T, RWKV_DIM)
    y = y * lp["rwkv_ln_w"] + lp["rwkv_ln_b"]
    bonus = jnp.sum(rh * kh * lp["rwkv_r_k"].astype(f32), axis=-1, keepdims=True) * vh
    out = (y + bonus.reshape(B_, T, RWKV_DIM)).astype(p.dtype) * g
    return out, p[:, -1], S_final.astype(S0.dtype)


def _branch_prompt(q, k, v, slopes, window, dil):
    B_, T, H, Dh = q.shape
    L = window // dil
    span = L * dil
    nb = -(-T // span)
    S_pad = nb * span
    P = S_pad - T

    def blocks(t):
        t = jnp.pad(t, ((0, 0), (P, 0), (0, 0), (0, 0)))
        return t.reshape(B_, nb, L, dil, H, Dh)

    def with_prev(t):
        prev = jnp.pad(t[:, :-1], ((0, 0), (1, 0), (0, 0), (0, 0), (0, 0), (0, 0)))
        return jnp.concatenate([prev, t], axis=2)

    qb = blocks(q)
    kb2, vb2 = with_prev(blocks(k)), with_prev(blocks(v))
    s = jnp.einsum("bnichd,bnjchd->bnchij", qb, kb2)
    qi = jnp.arange(L)[:, None]
    kj = jnp.arange(2 * L)[None, :]
    delta = L + qi - kj
    blk = jnp.arange(nb)[:, None, None, None]
    res = jnp.arange(dil)[None, :, None, None]
    key_pos = ((blk - 1) * L + kj[None, None]) * dil + res
    valid = (delta >= 0) & (delta <= L) & (key_pos >= P)
    bias = -slopes[:, None, None] * (dil * delta).astype(jnp.float32)
    s = jnp.where(valid[None, :, :, None], s + bias, NEG_INF)
    m = jnp.max(s, axis=-1)
    pr = jnp.exp(s - m[..., None])
    l = jnp.sum(pr, axis=-1)
    acc = jnp.einsum("bnchij,bnjchd->bnichd", pr, vb2).reshape(B_, S_pad, H, Dh)[:, P:]
    m = m.transpose(0, 1, 4, 2, 3).reshape(B_, S_pad, H)[:, P:]
    l = l.transpose(0, 1, 4, 2, 3).reshape(B_, S_pad, H)[:, P:]
    return acc, m, l


def _branch_sample(q, kc, vc, slopes, window, dil):
    B_, T, H, Dh = q.shape
    W = kc.shape[1] - T
    n = window // dil
    steps = jnp.arange(n + 1)
    idx = W + jnp.arange(T)[:, None] - dil * steps[None, :]
    valid = idx >= 0
    idx = jnp.maximum(idx, 0)
    kg, vg = kc[:, idx], vc[:, idx]
    s = jnp.einsum("bthd,btkhd->bhtk", q, kg) - slopes[:, None, None] * (dil * steps).astype(jnp.float32)
    s = jnp.where(valid[None, None], s, NEG_INF)
    m = jnp.max(s, axis=-1)
    pr = jnp.exp(s - m[..., None])
    l = jnp.sum(pr, axis=-1)
    acc = jnp.einsum("bhtk,btkhd->bthd", pr, vg)
    return acc, m.transpose(0, 2, 1), l.transpose(0, 2, 1)


def dilated_attention(q, k, v, k_buf, v_buf):
    f32 = jnp.float32
    slopes = alibi_slopes()
    qf = q.astype(f32) * (HEAD_DIM ** -0.5)
    kf, vf = k.astype(f32), v.astype(f32)
    if k_buf is None:
        res = [_branch_prompt(qf, kf, vf, slopes, w, d) for (w, d) in DILATED_BRANCHES]
    else:
        kc = jnp.concatenate([k_buf.astype(f32), kf], axis=1)
        vc = jnp.concatenate([v_buf.astype(f32), vf], axis=1)
        res = [_branch_sample(qf, kc, vc, slopes, w, d) for (w, d) in DILATED_BRANCHES]
    M = res[0][1]
    for br in res[1:]:
        M = jnp.maximum(M, br[1])
    num = jnp.zeros_like(qf)
    den = jnp.zeros_like(M)
    for acc, m, l in res:
        e = jnp.exp(m - M)
        num = num + e[..., None] * acc
        den = den + e * l
    return (num / den[..., None]).astype(q.dtype)


def ssd_chunked(x, dt, A, Bm, Cm, h0):
    B_, T, H, P = x.shape
    L = SSM_CHUNK if T % SSM_CHUNK == 0 else T
    nc = T // L
    rs = lambda t: t.reshape(B_, nc, L, *t.shape[2:])
    xc, dtc, Bc, Cc = rs(x), rs(dt), rs(Bm), rs(Cm)
    acum = jnp.cumsum(dtc * A, axis=2)
    seg = acum[:, :, :, None, :] - acum[:, :, None, :, :]
    causal = jnp.tril(jnp.ones((L, L), dtype=bool))[None, None, :, :, None]
    decay = jnp.exp(jnp.where(causal, seg, -jnp.inf))
    scores = jnp.einsum("bclhn,bcshn->bclsh", Cc, Bc) * decay * dtc[:, :, None, :, :]
    y_intra = jnp.einsum("bclsh,bcshp->bclhp", scores, xc)
    to_end = jnp.exp(acum[:, :, -1:, :] - acum) * dtc
    st = jnp.einsum("bclh,bclhn,bclhp->bchpn", to_end, Bc, xc)
    chunk_decay = jnp.exp(acum[:, :, -1, :])

    def step(h, inp):
        st_c, dec_c = inp
        return h * dec_c[:, :, None, None] + st_c, h

    h_final, h_starts = lax.scan(step, h0, (jnp.moveaxis(st, 1, 0), jnp.moveaxis(chunk_decay, 1, 0)))
    h_starts = jnp.moveaxis(h_starts, 0, 1)
    y_inter = jnp.einsum("bclhn,bchpn->bclhp", Cc, h_starts) * jnp.exp(acum)[..., None]
    return (y_intra + y_inter).reshape(B_, T, H, P), h_final


def mamba2_mix(p_z, p_xbc, p_dt, conv_prev, h0, lp):
    B_, T, _ = p_xbc.shape
    f32 = jnp.float32
    ext = jnp.concatenate([conv_prev.astype(p_xbc.dtype), p_xbc], axis=1)
    conv = lp["ssm_conv_b"]
    for i in range(SSM_CONV):
        conv = conv + ext[:, i:i + T] * lp["ssm_conv_w"][i]
    new_conv = ext[:, T:]
    xbc = jax.nn.silu(conv)
    xs, Bm, Cm = jnp.split(xbc, [SSM_DIM, SSM_DIM + SSM_GROUPS * SSM_STATE], axis=-1)
    x = xs.reshape(B_, T, SSM_HEADS, HEAD_DIM).astype(f32)
    rep = SSM_HEADS // SSM_GROUPS
    Bm = jnp.repeat(Bm.reshape(B_, T, SSM_GROUPS, SSM_STATE), rep, axis=2).astype(f32)
    Cm = jnp.repeat(Cm.reshape(B_, T, SSM_GROUPS, SSM_STATE), rep, axis=2).astype(f32)
    dt = jax.nn.softplus((p_dt + lp["ssm_dt_bias"]).astype(f32))
    A = -jnp.exp(lp["ssm_A_log"].astype(f32))
    y, h_final = ssd_chunked(x, dt, A, Bm, Cm, h0.astype(f32))
    y = y + lp["ssm_D"].astype(f32)[:, None] * x
    y = y.reshape(B_, T, SSM_DIM).astype(p_z.dtype) * jax.nn.silu(p_z)
    return rms_norm(y, lp["ssm_norm_w"]), new_conv, h_final.astype(h0.dtype)


def token_mixer(h, lp, st):
    k_buf, v_buf, shift_prev, conv_prev, S0, h0 = st
    B_, T, _ = h.shape
    p = h @ lp["w_in"]
    o1 = RWKV_PROJ
    o2 = o1 + 3 * ATT_DIM
    o3 = o2 + SSM_DIM
    o4 = o3 + SSM_CONV_DIM
    p_rwkv, p_att, p_z, p_xbc, p_dt = jnp.split(p, [o1, o2, o3, o4], axis=-1)
    y_a, new_shift, S_new = rwkv7_mix(p_rwkv, shift_prev, S0, lp)
    q, k, v = [t.reshape(B_, T, ATT_HEADS, HEAD_DIM) for t in jnp.split(p_att, 3, axis=-1)]
    y_b = dilated_attention(q, k, v, k_buf, v_buf).reshape(B_, T, ATT_DIM)
    y_b = rms_norm(y_b, lp["att_norm_w"])
    y_c, new_conv, h_new = mamba2_mix(p_z, p_xbc, p_dt, conv_prev, h0, lp)
    out = jnp.concatenate([y_a, y_b, y_c], axis=-1) @ lp["w_out"]
    return out, (k, v, new_shift, S_new, new_conv, h_new)


def run_trunk(x, layers, states, final_norm):
    new_states = []
    for l in range(DEPTH):
        lp = layers[l]
        x = x + 0.5 * swiglu(rms_norm(x, lp["ffn1_norm"]), lp["ffn1_w_gate"], lp["ffn1_w_up"], lp["ffn1_w_down"])
        mix, st_new = token_mixer(rms_norm(x, lp["mix_norm"]), lp, states[l])
        x = x + mix
        x = x + 0.5 * swiglu(rms_norm(x, lp["ffn2_norm"]), lp["ffn2_w_gate"], lp["ffn2_w_up"], lp["ffn2_w_down"])
        new_states.append(st_new)
    return rms_norm(x, final_norm), new_states


def setup_inputs(seed: int = 0) -> dict:
    key = jax.random.key(seed)
    it = iter(list(jax.random.split(key, 64)))
    f32 = jnp.float32

    def normal(shape, scale):
        return scale * jax.random.normal(next(it), shape, f32)

    def gain(shape):
        return 1.0 + normal(shape, 0.02)

    att_buf = min(MAX_WINDOW, PAST_LEN)
    dt0 = jnp.exp(jax.random.uniform(next(it), (DEPTH, SSM_HEADS), f32, np.log(1e-3), np.log(1e-1)))
    inp = {}
    inp["x_prompt"] = normal((BATCH, SEQ, D_MODEL), 1.0)
    inp["x_sample"] = normal((DEC_BATCH, DEC_SEQ, D_MODEL), 1.0)
    inp["cache_att_k"] = normal((DEPTH, DEC_BATCH, att_buf, ATT_HEADS, HEAD_DIM), 1.0)
    inp["cache_att_v"] = normal((DEPTH, DEC_BATCH, att_buf, ATT_HEADS, HEAD_DIM), 1.0)
    inp["state_rwkv_shift"] = normal((DEPTH, DEC_BATCH, RWKV_PROJ), 1.0)
    inp["state_rwkv"] = normal((DEPTH, DEC_BATCH, RWKV_HEADS, HEAD_DIM, HEAD_DIM), 0.3)
    inp["state_ssm_conv"] = normal((DEPTH, DEC_BATCH, SSM_CONV - 1, SSM_CONV_DIM), 1.0)
    inp["state_ssm"] = normal((DEPTH, DEC_BATCH, SSM_HEADS, HEAD_DIM, SSM_STATE), 0.3)
    inp["ffn1_norm"] = gain((DEPTH, D_MODEL))
    inp["ffn1_w_gate"] = normal((DEPTH, D_MODEL, FFN_DIM), D_MODEL ** -0.5)
    inp["ffn1_w_up"] = normal((DEPTH, D_MODEL, FFN_DIM), D_MODEL ** -0.5)
    inp["ffn1_w_down"] = normal((DEPTH, FFN_DIM, D_MODEL), FFN_DIM ** -0.5)
    inp["mix_norm"] = gain((DEPTH, D_MODEL))
    inp["w_in"] = normal((DEPTH, D_MODEL, PROJ_DIM), D_MODEL ** -0.5)
    inp["rwkv_mu"] = jax.random.uniform(next(it), (DEPTH, RWKV_PROJ), f32)
    inp["rwkv_w0"] = normal((DEPTH, RWKV_DIM), 0.5)
    inp["rwkv_w2"] = normal((DEPTH, RWKV_W_RANK, RWKV_DIM), RWKV_W_RANK ** -0.5)
    inp["rwkv_a0"] = normal((DEPTH, RWKV_DIM), 0.1)
    inp["rwkv_a2"] = normal((DEPTH, RWKV_A_RANK, RWKV_DIM), 0.5 * RWKV_A_RANK ** -0.5)
    inp["rwkv_g2"] = normal((DEPTH, RWKV_G_RANK, RWKV_DIM), RWKV_G_RANK ** -0.5)
    inp["rwkv_k_k"] = 0.85 + normal((DEPTH, RWKV_DIM), 0.02)
    inp["rwkv_k_a"] = gain((DEPTH, RWKV_DIM))
    inp["rwkv_r_k"] = normal((DEPTH, RWKV_HEADS, HEAD_DIM), 0.1)
    inp["rwkv_ln_w"] = gain((DEPTH, RWKV_DIM))
    inp["rwkv_ln_b"] = normal((DEPTH, RWKV_DIM), 0.01)
    inp["att_norm_w"] = gain((DEPTH, ATT_DIM))
    inp["ssm_conv_w"] = normal((DEPTH, SSM_CONV, SSM_CONV_DIM), 0.5)
    inp["ssm_conv_b"] = normal((DEPTH, SSM_CONV_DIM), 0.01)
    inp["ssm_dt_bias"] = dt0 + jnp.log(-jnp.expm1(-dt0))
    inp["ssm_A_log"] = jnp.log(jax.random.uniform(next(it), (DEPTH, SSM_HEADS), f32, 1.0, 16.0))
    inp["ssm_D"] = 1.0 + normal((DEPTH, SSM_HEADS), 0.1)
    inp["ssm_norm_w"] = gain((DEPTH, SSM_DIM))
    inp["w_out"] = normal((DEPTH, D_MIX, D_MODEL), D_MIX ** -0.5)
    inp["ffn2_norm"] = gain((DEPTH, D_MODEL))
    inp["ffn2_w_gate"] = normal((DEPTH, D_MODEL, FFN_DIM), D_MODEL ** -0.5)
    inp["ffn2_w_up"] = normal((DEPTH, D_MODEL, FFN_DIM), D_MODEL ** -0.5)
    inp["ffn2_w_down"] = normal((DEPTH, FFN_DIM, D_MODEL), FFN_DIM ** -0.5)
    inp["final_norm"] = gain((D_MODEL,))
    return inp


def reference(x_prompt, x_sample, cache_att_k, cache_att_v, state_rwkv_shift, state_rwkv,
              state_ssm_conv, state_ssm, ffn1_norm, ffn1_w_gate, ffn1_w_up, ffn1_w_down,
              mix_norm, w_in, rwkv_mu, rwkv_w0, rwkv_w2, rwkv_a0, rwkv_a2, rwkv_g2, rwkv_k_k,
              rwkv_k_a, rwkv_r_k, rwkv_ln_w, rwkv_ln_b, att_norm_w, ssm_conv_w, ssm_conv_b,
              ssm_dt_bias, ssm_A_log, ssm_D, ssm_norm_w, w_out, ffn2_norm, ffn2_w_gate,
              ffn2_w_up, ffn2_w_down, final_norm):
    layers = [dict(ffn1_norm=ffn1_norm[l], ffn1_w_gate=ffn1_w_gate[l], ffn1_w_up=ffn1_w_up[l],
                   ffn1_w_down=ffn1_w_down[l], mix_norm=mix_norm[l], w_in=w_in[l],
                   rwkv_mu=rwkv_mu[l], rwkv_w0=rwkv_w0[l], rwkv_w2=rwkv_w2[l], rwkv_a0=rwkv_a0[l],
                   rwkv_a2=rwkv_a2[l], rwkv_g2=rwkv_g2[l], rwkv_k_k=rwkv_k_k[l], rwkv_k_a=rwkv_k_a[l],
                   rwkv_r_k=rwkv_r_k[l], rwkv_ln_w=rwkv_ln_w[l], rwkv_ln_b=rwkv_ln_b[l],
                   att_norm_w=att_norm_w[l], ssm_conv_w=ssm_conv_w[l], ssm_conv_b=ssm_conv_b[l],
                   ssm_dt_bias=ssm_dt_bias[l], ssm_A_log=ssm_A_log[l], ssm_D=ssm_D[l],
                   ssm_norm_w=ssm_norm_w[l], w_out=w_out[l], ffn2_norm=ffn2_norm[l],
                   ffn2_w_gate=ffn2_w_gate[l], ffn2_w_up=ffn2_w_up[l], ffn2_w_down=ffn2_w_down[l])
              for l in range(DEPTH)]

    Bp, Tp = x_prompt.shape[0], x_prompt.shape[1]
    dtp = x_prompt.dtype
    zero_state = (None, None,
                  jnp.zeros((Bp, RWKV_PROJ), dtp),
                  jnp.zeros((Bp, SSM_CONV - 1, SSM_CONV_DIM), dtp),
                  jnp.zeros((Bp, RWKV_HEADS, HEAD_DIM, HEAD_DIM), dtp),
                  jnp.zeros((Bp, SSM_HEADS, HEAD_DIM, SSM_STATE), dtp))
    y_prompt, p_new = run_trunk(x_prompt, layers, [zero_state] * DEPTH, final_norm)

    s_states = [(cache_att_k[l], cache_att_v[l], state_rwkv_shift[l], state_ssm_conv[l],
                 state_rwkv[l], state_ssm[l]) for l in range(DEPTH)]
    y_sample, s_new = run_trunk(x_sample, layers, s_states, final_norm)

    keep = min(MAX_WINDOW, Tp)
    p_att_k = jnp.stack([s[0][:, -keep:] for s in p_new])
    p_att_v = jnp.stack([s[1][:, -keep:] for s in p_new])
    p_rwkv_shift = jnp.stack([s[2] for s in p_new])
    p_rwkv = jnp.stack([s[3] for s in p_new])
    p_ssm_conv = jnp.stack([s[4] for s in p_new])
    p_ssm = jnp.stack([s[5] for s in p_new])
    s_att_k = jnp.stack([s[0] for s in s_new])
    s_att_v = jnp.stack([s[1] for s in s_new])
    s_rwkv_shift = jnp.stack([s[2] for s in s_new])
    s_rwkv = jnp.stack([s[3] for s in s_new])
    s_ssm_conv = jnp.stack([s[4] for s in s_new])
    s_ssm = jnp.stack([s[5] for s in s_new])
    return (y_prompt, y_sample, p_att_k, p_att_v, p_rwkv_shift, p_rwkv, p_ssm_conv, p_ssm,
            s_att_k, s_att_v, s_rwkv_shift, s_rwkv, s_ssm_conv, s_ssm)
```

```python
import functools

import jax
import jax.numpy as jnp
from jax import lax
from jax.experimental import pallas as pl
from jax.experimental.pallas import tpu as pltpu

F32 = jnp.float32
BF16 = jnp.bfloat16

D_MODEL = 1024
FFN_DIM = 2816
HEAD_DIM = 64
RWKV_HEADS = 4
RWKV_DIM = RWKV_HEADS * HEAD_DIM
RWKV_PROJ = 1024
RWKV_GN_EPS = 64e-5
ATT_HEADS = 6
ATT_DIM = ATT_HEADS * HEAD_DIM
ATT_PAIRS = ATT_HEADS // 2
DILATIONS = (1, 4, 16)
ATT_STEPS = 128
SSM_HEADS = 6
SSM_DIM = SSM_HEADS * HEAD_DIM
SSM_GROUPS = 2
SSM_STATE = 128
SSM_CONV = 4
SSM_CONV_DIM = SSM_DIM + 2 * SSM_GROUPS * SSM_STATE
PROJ_DIM = RWKV_PROJ + 3 * ATT_DIM + SSM_DIM + SSM_CONV_DIM + SSM_HEADS
PROJ_PAD = 3584
NORM_EPS = 1e-6
NEG_INF = -1e30

LANES = 128
ROW_TILE = 512
FFN_CHUNK = 256
RWKV_CHUNK = 64
SSD_CHUNK = 128
VMEM_LIMIT = 56 * 1024 * 1024

OFF_ATT = RWKV_PROJ
OFF_Z = OFF_ATT + 3 * ATT_DIM
OFF_XBC = OFF_Z + SSM_DIM
OFF_DT = OFF_XBC + SSM_CONV_DIM


def _params(sem):
    return pltpu.CompilerParams(dimension_semantics=sem, vmem_limit_bytes=VMEM_LIMIT)


def _dot(a, b):
    return jnp.dot(a.astype(BF16), b.astype(BF16), preferred_element_type=F32)


def _dot_nt(a, b):
    return lax.dot_general(a.astype(BF16), b.astype(BF16), (((1,), (1,)), ((), ())),
                           preferred_element_type=F32)


def _dot_tn(a, b):
    return lax.dot_general(a.astype(BF16), b.astype(BF16), (((0,), (0,)), ((), ())),
                           preferred_element_type=F32)


def _dot_hi(a, b):
    return jnp.dot(a, b, preferred_element_type=F32, precision=lax.Precision.HIGHEST)


def _rms(x, w):
    return x * lax.rsqrt(jnp.mean(x * x, axis=-1, keepdims=True) + NORM_EPS) * w


def _sigmoid(x):
    return 1.0 / (1.0 + jnp.exp(-x))


def _softplus(x):
    return jnp.maximum(x, 0.0) + jnp.log(1.0 + jnp.exp(-jnp.abs(x)))


def _div(x, n):
    assert n & (n - 1) == 0
    return lax.shift_right_logical(x, n.bit_length() - 1)


def _mod(x, n):
    assert n & (n - 1) == 0
    return x & (n - 1)


def _cumsum_rows(x):
    n = x.shape[0]
    row = lax.broadcasted_iota(jnp.int32, x.shape, 0)
    s = 1
    while s < n:
        x = x + jnp.where(row >= s, pltpu.roll(x, s, axis=0), 0.0)
        s *= 2
    return x


def _ffn_body(has_mix, final, n_prompt_tiles, *refs):
    it = iter(refs)
    x_ref = next(it)
    if has_mix:
        yap, yas, ybp, ybs, ycp, ycs, wo_ref = (next(it) for _ in range(7))
    nw_ref, wg_ref, wu_ref, wd_ref = (next(it) for _ in range(4))
    if final:
        fn_ref = next(it)
        op_ref, os_ref = next(it), next(it)
    else:
        o_ref = next(it)
    h_ref, acc_ref = next(it), next(it)
    i = pl.program_id(0)
    is_sample = i >= n_prompt_tiles

    x = x_ref[...]
    if has_mix:
        ya = jnp.where(is_sample, yas[...], yap[...])
        yb = jnp.where(is_sample, ybs[...], ybp[...])
        yc = jnp.where(is_sample, ycs[...], ycp[...])
        x = (x + _dot(ya, wo_ref[0:RWKV_DIM, :])
             + _dot(yb, wo_ref[RWKV_DIM:RWKV_DIM + ATT_DIM, :])
             + _dot(yc, wo_ref[RWKV_DIM + ATT_DIM:, :]))
    h_ref[...] = _rms(x, nw_ref[...]).astype(BF16)
    acc_ref[...] = x

    def chunk(c, carry):
        off = pl.multiple_of(c * FFN_CHUNK, FFN_CHUNK)
        h = h_ref[...]
        g = jnp.dot(h, wg_ref[:, pl.ds(off, FFN_CHUNK)], preferred_element_type=F32)
        u = jnp.dot(h, wu_ref[:, pl.ds(off, FFN_CHUNK)], preferred_element_type=F32)
        a = (g * _sigmoid(g) * u).astype(BF16)
        acc_ref[...] += 0.5 * jnp.dot(a, wd_ref[pl.ds(off, FFN_CHUNK), :], preferred_element_type=F32)
        return carry

    lax.fori_loop(0, FFN_DIM // FFN_CHUNK, chunk, 0)
    if final:
        y = _rms(acc_ref[...], fn_ref[...])

        @pl.when(jnp.logical_not(is_sample))
        def _():
            op_ref[...] = y

        @pl.when(is_sample)
        def _():
            os_ref[...] = y
    else:
        o_ref[...] = acc_ref[...]


def _ffn_call(x, mix, wo, nw, wg, wu, wd, final_norm, n_prompt_tiles):
    m_pad = x.shape[0]
    tm = ROW_TILE
    has_mix = mix is not None
    final = final_norm is not None
    last_p = n_prompt_tiles - 1
    row = lambda i: (i, 0)
    fixed = lambda i: (0, 0)
    prow = lambda i: (jnp.minimum(i, last_p), 0)
    resident = functools.partial(pl.BlockSpec, index_map=fixed, pipeline_mode=pl.Buffered(1))
    in_specs = [pl.BlockSpec((tm, D_MODEL), row)]
    args = [x]
    if has_mix:
        for yp, ys in mix:
            w = yp.shape[1]
            in_specs += [pl.BlockSpec((tm, w), prow), pl.BlockSpec((tm, w), fixed)]
            args += [yp, ys]
        in_specs.append(resident((D_MODEL, D_MODEL)))
        args.append(wo)
    in_specs += [pl.BlockSpec((1, D_MODEL), fixed), resident((D_MODEL, FFN_DIM)),
                 resident((D_MODEL, FFN_DIM)), resident((FFN_DIM, D_MODEL))]
    args += [nw, wg, wu, wd]
    if final:
        in_specs.append(pl.BlockSpec((1, D_MODEL), fixed))
        args.append(final_norm)
        out_shape = (jax.ShapeDtypeStruct((n_prompt_tiles * tm, D_MODEL), F32),
                     jax.ShapeDtypeStruct((tm, D_MODEL), F32))
        out_specs = (pl.BlockSpec((tm, D_MODEL), prow), pl.BlockSpec((tm, D_MODEL), fixed))
    else:
        out_shape = jax.ShapeDtypeStruct((m_pad, D_MODEL), F32)
        out_specs = pl.BlockSpec((tm, D_MODEL), row)
    return pl.pallas_call(
        functools.partial(_ffn_body, has_mix, final, n_prompt_tiles),
        grid=(m_pad // tm,), in_specs=in_specs, out_specs=out_specs, out_shape=out_shape,
        scratch_shapes=[pltpu.VMEM((tm, D_MODEL), BF16), pltpu.VMEM((tm, D_MODEL), F32)],
        compiler_params=_params(("arbitrary",)), name="ffn")(*args)


def _proj_body(n_prompt_tiles, x_ref, nw_ref, w_ref, pr_ref, q3_ref, k3_ref, v3_ref, kn_ref, vn_ref,
               qs_ref, ks_ref, vs_ref, z_ref, xbc_ref, dt_ref):
    i = pl.program_id(0)
    is_sample = i >= n_prompt_tiles
    h = _rms(x_ref[...], nw_ref[...]).astype(BF16)

    def seg(a, b):
        return jnp.dot(h, w_ref[:, a:b], preferred_element_type=F32)

    pr_ref[...] = seg(0, OFF_ATT)
    q = seg(OFF_ATT, OFF_ATT + ATT_DIM)
    k = seg(OFF_ATT + ATT_DIM, OFF_ATT + 2 * ATT_DIM)
    v = seg(OFF_ATT + 2 * ATT_DIM, OFF_Z)
    z_ref[...] = seg(OFF_Z, OFF_XBC)
    xbc_ref[...] = seg(OFF_XBC, OFF_DT)
    dt_ref[...] = seg(OFF_DT, PROJ_PAD)

    @pl.when(jnp.logical_not(is_sample))
    def _():
        kn_ref[...] = k
        vn_ref[...] = v
        for j in range(ATT_PAIRS):
            q3_ref[j] = q[:, j * LANES:(j + 1) * LANES]
            k3_ref[j] = k[:, j * LANES:(j + 1) * LANES]
            v3_ref[j] = v[:, j * LANES:(j + 1) * LANES]

    @pl.when(is_sample)
    def _():
        qs_ref[...] = q
        ks_ref[...] = k
        vs_ref[...] = v


def _proj_call(x, nw, w, n_prompt_tiles):
    m_pad = x.shape[0]
    tm = ROW_TILE
    mp = n_prompt_tiles * tm
    last_p = n_prompt_tiles - 1
    row = lambda i: (i, 0)
    fixed = lambda i: (0, 0)
    prow = lambda i: (jnp.minimum(i, last_p), 0)
    prow3 = lambda i: (0, jnp.minimum(i, last_p), 0)
    sds = jax.ShapeDtypeStruct
    out_shape = (sds((m_pad, RWKV_PROJ), F32),
                 sds((ATT_PAIRS, mp, LANES), F32), sds((ATT_PAIRS, mp, LANES), F32), sds((ATT_PAIRS, mp, LANES), F32),
                 sds((mp, ATT_DIM), F32), sds((mp, ATT_DIM), F32),
                 sds((tm, ATT_DIM), F32), sds((tm, ATT_DIM), F32), sds((tm, ATT_DIM), F32),
                 sds((m_pad, SSM_DIM), F32), sds((m_pad, SSM_CONV_DIM), F32), sds((m_pad, LANES), F32))
    p3 = pl.BlockSpec((ATT_PAIRS, tm, LANES), prow3)
    out_specs = (pl.BlockSpec((tm, RWKV_PROJ), row), p3, p3, p3,
                 pl.BlockSpec((tm, ATT_DIM), prow), pl.BlockSpec((tm, ATT_DIM), prow),
                 pl.BlockSpec((tm, ATT_DIM), fixed), pl.BlockSpec((tm, ATT_DIM), fixed),
                 pl.BlockSpec((tm, ATT_DIM), fixed),
                 pl.BlockSpec((tm, SSM_DIM), row), pl.BlockSpec((tm, SSM_CONV_DIM), row),
                 pl.BlockSpec((tm, LANES), row))
    return pl.pallas_call(
        functools.partial(_proj_body, n_prompt_tiles),
        grid=(m_pad // tm,),
        in_specs=[pl.BlockSpec((tm, D_MODEL), row), pl.BlockSpec((1, D_MODEL), fixed),
                  pl.BlockSpec((D_MODEL, PROJ_PAD), fixed, pipeline_mode=pl.Buffered(1))],
        out_specs=out_specs, out_shape=out_shape,
        compiler_params=_params(("arbitrary",)), name="proj")(x, nw, w)


def _rwkv_body(chunk, t_valid, t_total, p_ref, sh_ref, s0_ref, mu_ref, w0_ref, w2_ref, a0_ref, a2_ref, g2_ref,
               kk_ref, ka_ref, rk_ref, lnw_ref, lnb_ref, y_ref, so_ref, sb_ref, prev_ref):
    C = chunk
    H = RWKV_HEADS
    N = HEAD_DIM
    R = RWKV_DIM
    c = pl.program_id(1)
    nc = pl.num_programs(1)

    @pl.when(c == 0)
    def _():
        sb_ref[...] = jnp.zeros((R, R), F32)
        for h in range(H):
            sb_ref[h * N:(h + 1) * N, h * N:(h + 1) * N] = s0_ref[0, h]
        prev_ref[...] = jnp.broadcast_to(sh_ref[0], (8, RWKV_PROJ))

    p = p_ref[...]
    row_p = lax.broadcasted_iota(jnp.int32, p.shape, 0)
    p_prev = jnp.where(row_p == 0, prev_ref[0:1, :], pltpu.roll(p, 1, axis=0))
    prev_ref[...] = jnp.broadcast_to(p[C - 1:C, :], (8, RWKV_PROJ))
    ps = p + mu_ref[...] * (p_prev - p)
    r = ps[:, 0:R]
    k = ps[:, R:2 * R]
    v = ps[:, 2 * R:3 * R]
    w_lr = ps[:, 3 * R:3 * R + 64]
    a_lr = ps[:, 3 * R + 64:3 * R + 128]
    g_lr = ps[:, 3 * R + 128:]

    w_log = -_softplus(-(w0_ref[...] + _dot(jnp.tanh(w_lr), w2_ref[...]))) - 0.5
    ld = -jnp.exp(w_log)
    a = _sigmoid(a0_ref[...] + _dot(a_lr, a2_ref[...]))
    g = _dot(_sigmoid(g_lr), g2_ref[...])

    lane_r = lax.broadcasted_iota(jnp.int32, (R, R), 1)
    row_r = lax.broadcasted_iota(jnp.int32, (R, R), 0)
    own_head = _div(lane_r, N) == _div(row_r, N)
    head_sum = jnp.where(own_head, 1.0, 0.0).astype(F32)

    kk = k * kk_ref[...]
    kk = kk / jnp.maximum(jnp.sqrt(_dot_hi(kk * kk, head_sum)), 1e-12)
    kp = k * (1.0 + (a - 1.0) * ka_ref[...])
    if t_valid < t_total:
        live = (c * C + lax.broadcasted_iota(jnp.int32, (C, R), 0)) < t_valid
        ld = jnp.where(live, ld, 0.0)
        kk = jnp.where(live, kk, 0.0)
        kp = jnp.where(live, kp, 0.0)
        v = jnp.where(live, v, 0.0)

    cum = _cumsum_rows(ld)
    w_inc = jnp.exp(cum)
    kk_t = kk * jnp.exp(cum - ld)
    w_inv = jnp.exp(-cum)
    b_t = kk * a * w_inv
    k_t = kp * w_inv
    r_t = r * w_inc
    w_end = w_inc[C - 1:C, :]

    HC = H * C
    bd_row = _div(lax.broadcasted_iota(jnp.int32, (HC, R), 0), C)
    bd_lane = _div(lax.broadcasted_iota(jnp.int32, (HC, R), 1), N)

    def bd_rows(x):
        return jnp.where(bd_row == bd_lane, jnp.concatenate([x] * H, axis=0), 0.0)

    sq_row = _div(lax.broadcasted_iota(jnp.int32, (HC, HC), 0), C)
    sq_col = _div(lax.broadcasted_iota(jnp.int32, (HC, HC), 1), C)

    def bd_square(x):
        return jnp.where(sq_row == sq_col, jnp.concatenate([x] * H, axis=0), 0.0)

    t_idx = lax.broadcasted_iota(jnp.int32, (C, HC), 0)
    s_idx = _mod(lax.broadcasted_iota(jnp.int32, (C, HC), 1), C)
    strict = s_idx < t_idx
    incl = s_idx <= t_idx
    eye = jnp.where(s_idx == t_idx, 1.0, 0.0).astype(F32)
    same16 = _div(s_idx, 16) == _div(t_idx, 16)

    b_bd = bd_rows(b_t)
    k_bd = bd_rows(k_t)
    a_ab = jnp.where(strict, lax.dot_general(kk_t, b_bd, (((1,), (1,)), ((), ())), preferred_element_type=F32,
                                             precision=lax.Precision.HIGHEST), 0.0)
    a_ak = jnp.where(strict, _dot_nt(kk_t, k_bd), 0.0)
    p_rb = jnp.where(incl, _dot_nt(r_t, b_bd), 0.0)
    p_rk = jnp.where(incl, _dot_nt(r_t, k_bd), 0.0)

    def mm(x, y):
        return _dot_hi(x, bd_square(y))

    d = jnp.where(same16, a_ab, 0.0)
    n_low = a_ab - d
    d2 = mm(d, d)
    d4 = mm(d2, d2)
    d8 = mm(d4, d4)
    d_inv = mm(mm(mm(eye - d, eye + d2), eye + d4), eye + d8)
    m1 = mm(d_inv, n_low)
    m2 = mm(m1, m1)
    t_inv = mm(mm(eye - m1, eye + m2), d_inv)

    sb = sb_ref[...]
    v_bd = bd_rows(v)
    rhs = -_dot_nt(kk_t, sb) - _dot(a_ak, v_bd)
    u = _dot_hi(t_inv, bd_rows(rhs))
    y = _dot_nt(r_t, sb) + _dot(p_rb, bd_rows(u)) + _dot(p_rk, v_bd)
    upd = _dot_tn(jnp.concatenate([u, v], axis=0), jnp.concatenate([b_t * w_end, k_t * w_end], axis=0))
    sb_ref[...] = jnp.where(own_head, sb * w_end + upd, 0.0)

    mean = _dot_hi(y, head_sum) * (1.0 / N)
    yc = y - mean
    var = _dot_hi(yc * yc, head_sum) * (1.0 / N)
    yn = yc * lax.rsqrt(var + RWKV_GN_EPS) * lnw_ref[...] + lnb_ref[...]
    bonus = _dot_hi(r * kp * rk_ref[...], head_sum) * v
    y_ref[...] = (yn + bonus) * g

    @pl.when(c == nc - 1)
    def _():
        for h in range(H):
            so_ref[0, h] = sb_ref[h * N:(h + 1) * N, h * N:(h + 1) * N]


def _rwkv_call(p, shift_prev, s0, lp, n_batch, t_total, t_valid):
    C = RWKV_CHUNK
    nc = t_total // C
    vec = lambda n: pl.BlockSpec((1, n), lambda b, c: (0, 0))
    mat = lambda a, n: pl.BlockSpec((a, n), lambda b, c: (0, 0))
    in_specs = [pl.BlockSpec((C, RWKV_PROJ), lambda b, c: (b * nc + c, 0)),
                pl.BlockSpec((1, 1, RWKV_PROJ), lambda b, c: (b, 0, 0)),
                pl.BlockSpec((1, RWKV_HEADS, HEAD_DIM, HEAD_DIM), lambda b, c: (b, 0, 0, 0)),
                vec(RWKV_PROJ), vec(RWKV_DIM), mat(64, RWKV_DIM), vec(RWKV_DIM), mat(64, RWKV_DIM),
                mat(128, RWKV_DIM), vec(RWKV_DIM), vec(RWKV_DIM), vec(RWKV_DIM), vec(RWKV_DIM), vec(RWKV_DIM)]
    out_shape = (jax.ShapeDtypeStruct((n_batch * t_total, RWKV_DIM), F32),
                 jax.ShapeDtypeStruct((n_batch, RWKV_HEADS, HEAD_DIM, HEAD_DIM), F32))
    out_specs = (pl.BlockSpec((C, RWKV_DIM), lambda b, c: (b * nc + c, 0)),
                 pl.BlockSpec((1, RWKV_HEADS, HEAD_DIM, HEAD_DIM), lambda b, c: (b, 0, 0, 0)))
    return pl.pallas_call(
        functools.partial(_rwkv_body, C, t_valid, t_total),
        grid=(n_batch, nc), in_specs=in_specs, out_specs=out_specs, out_shape=out_shape,
        scratch_shapes=[pltpu.VMEM((RWKV_DIM, RWKV_DIM), F32), pltpu.VMEM((8, RWKV_PROJ), F32)],
        compiler_params=_params(("arbitrary", "arbitrary")), name="rwkv")(
            p, shift_prev.reshape(n_batch, 1, RWKV_PROJ), s0,
            lp["rwkv_mu"], lp["rwkv_w0"], lp["rwkv_w2"], lp["rwkv_a0"], lp["rwkv_a2"], lp["rwkv_g2"],
            lp["rwkv_k_k"], lp["rwkv_k_a"], lp["rwkv_r_k"], lp["rwkv_ln_w"], lp["rwkv_ln_b"])


def _alibi_slope(h):
    return 2.0 ** (-8.0 * (h + 1) / ATT_HEADS)


def _att_prompt_body(t_len, q_ref, k_ref, v_ref, nw_ref, o_ref, acc_ref, m_ref, l_ref):
    blk = ATT_STEPS
    for j in range(ATT_PAIRS):
        for d in DILATIONS:
            n = t_len // d
            kw = min(2 * blk, n)
            nb = n // blk
            qi = lax.broadcasted_iota(jnp.int32, (blk, kw), 0)
            kj = lax.broadcasted_iota(jnp.int32, (blk, kw), 1)

            qj, kj_ref, vj = q_ref.at[j], k_ref.at[j], v_ref.at[j]
            accj, mj, lj = acc_ref.at[j], m_ref.at[j], l_ref.at[j]

            def step(it, carry, j=j, d=d, kw=kw, nb=nb, qi=qi, kj=kj, qj=qj, kj_ref=kj_ref, vj=vj, accj=accj, mj=mj,
                     lj=lj):
                res = it // nb
                b = it % nb
                kb = jnp.maximum(b - 1, 0) if kw > blk else b
                q_start = res + d * blk * b
                k_start = res + d * blk * kb
                q2 = qj[pl.ds(q_start, blk, stride=d), :] * (HEAD_DIM ** -0.5)
                k2 = kj_ref[pl.ds(k_start, kw, stride=d), :]
                v2 = vj[pl.ds(k_start, kw, stride=d), :]
                delta = (b - kb) * blk + qi - kj
                valid = (delta >= 0) & (delta <= blk)
                dist = (d * delta).astype(F32)
                accs, ms, ls = [], [], []
                for hh in range(2):
                    sl = slice(hh * HEAD_DIM, (hh + 1) * HEAD_DIM)
                    s = _dot_nt(q2[:, sl], k2[:, sl])
                    s = jnp.where(valid, s - _alibi_slope(2 * j + hh) * dist, NEG_INF)
                    m = jnp.max(s, axis=-1, keepdims=True)
                    pr = jnp.exp(s - m)
                    ls.append(jnp.broadcast_to(jnp.sum(pr, axis=-1, keepdims=True), (blk, HEAD_DIM)))
                    ms.append(jnp.broadcast_to(m, (blk, HEAD_DIM)))
                    accs.append(_dot(pr, v2[:, sl]))
                acc = jnp.concatenate(accs, axis=1)
                m = jnp.concatenate(ms, axis=1)
                l = jnp.concatenate(ls, axis=1)
                rows = pl.ds(q_start, blk, stride=d)
                if d == DILATIONS[0]:
                    accj[rows, :] = acc
                    mj[rows, :] = m
                    lj[rows, :] = l
                else:
                    m_old = mj[rows, :]
                    m_new = jnp.maximum(m_old, m)
                    e_old = jnp.exp(m_old - m_new)
                    e_new = jnp.exp(m - m_new)
                    accj[rows, :] = accj[rows, :] * e_old + acc * e_new
                    lj[rows, :] = lj[rows, :] * e_old + l * e_new
                    mj[rows, :] = m_new
                return carry

            lax.fori_loop(0, d * nb, step, 0)

    def finish(it, carry):
        rows = pl.ds(pl.multiple_of(it * blk, blk), blk)
        ys = [acc_ref[j, rows, :] / l_ref[j, rows, :] for j in range(ATT_PAIRS)]
        ms = sum(jnp.sum(y * y, axis=-1, keepdims=True) for y in ys) * (1.0 / ATT_DIM)
        scale = lax.rsqrt(ms + NORM_EPS)
        for j in range(ATT_PAIRS):
            o_ref[rows, j * LANES:(j + 1) * LANES] = ys[j] * scale * nw_ref[:, j * LANES:(j + 1) * LANES]
        return carry

    lax.fori_loop(0, t_len // blk, finish, 0)


def _att_prompt_call(q3, k3, v3, nw, n_batch, t_len):
    blk3 = pl.BlockSpec((ATT_PAIRS, t_len, LANES), lambda b: (0, b, 0))
    return pl.pallas_call(
        functools.partial(_att_prompt_body, t_len),
        grid=(n_batch,),
        in_specs=[blk3, blk3, blk3, pl.BlockSpec((1, ATT_DIM), lambda b: (0, 0))],
        out_specs=pl.BlockSpec((t_len, ATT_DIM), lambda b: (b, 0)),
        out_shape=jax.ShapeDtypeStruct((n_batch * t_len, ATT_DIM), F32),
        scratch_shapes=[pltpu.VMEM((ATT_PAIRS, t_len, LANES), F32)] * 3,
        compiler_params=_params(("arbitrary",)), name="att_prompt")(q3, k3, v3, nw)


def _multiplicity(dist):
    mult = jnp.zeros(dist.shape, F32)
    for d in DILATIONS:
        hit = (_mod(dist, d) == 0) & (dist <= d * ATT_STEPS)
        mult = mult + jnp.where(hit, 1.0, 0.0)
    return mult


def _att_sample_body(t_new, w_buf, q_ref, kn_ref, vn_ref, kb_ref, vb_ref, nw_ref, o_ref):
    RQ = 8
    rows = ATT_HEADS * RQ
    q = q_ref[0] * (HEAD_DIM ** -0.5)
    head_of_row = _div(lax.broadcasted_iota(jnp.int32, (rows, ATT_DIM), 0), RQ)
    head_of_lane = _div(lax.broadcasted_iota(jnp.int32, (rows, ATT_DIM), 1), HEAD_DIM)
    own = head_of_row == head_of_lane
    q_bd = jnp.where(own, jnp.concatenate([q] * ATT_HEADS, axis=0), 0.0)

    s_buf = _dot_nt(q_bd, kb_ref[0])
    s_new = _dot_nt(q_bd, kn_ref[0])

    def weights(s, key_pos):
        t = _mod(lax.broadcasted_iota(jnp.int32, s.shape, 0), RQ)
        h = _div(lax.broadcasted_iota(jnp.int32, s.shape, 0), RQ)
        dist = w_buf + t - key_pos
        slope = sum(jnp.where(h == hh, _alibi_slope(hh), 0.0) for hh in range(ATT_HEADS))
        mult = jnp.where(dist >= 0, _multiplicity(jnp.maximum(dist, 0)), 0.0)
        return jnp.where(mult > 0, s - slope * dist.astype(F32), NEG_INF), mult

    sb, mult_b = weights(s_buf, lax.broadcasted_iota(jnp.int32, s_buf.shape, 1))
    sn, mult_n = weights(s_new, w_buf + lax.broadcasted_iota(jnp.int32, s_new.shape, 1))
    m = jnp.maximum(jnp.max(sb, axis=-1, keepdims=True), jnp.max(sn, axis=-1, keepdims=True))
    pb = mult_b * jnp.exp(sb - m)
    pn = mult_n * jnp.exp(sn - m)
    den = jnp.sum(pb, axis=-1, keepdims=True) + jnp.sum(pn, axis=-1, keepdims=True)
    num = _dot(pb, vb_ref[0]) + _dot(pn, vn_ref[0])
    out = jnp.where(own, num / den, 0.0).reshape(ATT_HEADS, RQ, ATT_DIM)
    y = jnp.sum(out, axis=0)
    o_ref[0] = _rms(y, nw_ref[...])


def _att_sample_call(q, k_new, v_new, k_buf, v_buf, nw, t_new):
    n_batch, w_buf, _ = k_buf.shape
    small = pl.BlockSpec((1, 8, ATT_DIM), lambda b: (b, 0, 0))
    big = pl.BlockSpec((1, w_buf, ATT_DIM), lambda b: (b, 0, 0))
    return pl.pallas_call(
        functools.partial(_att_sample_body, t_new, w_buf),
        grid=(n_batch,),
        in_specs=[small, small, small, big, big, pl.BlockSpec((1, ATT_DIM), lambda b: (0, 0))],
        out_specs=small, out_shape=jax.ShapeDtypeStruct((n_batch, 8, ATT_DIM), F32),
        compiler_params=_params(("arbitrary",)), name="att_sample")(q, k_new, v_new, k_buf, v_buf, nw)


def _ssd_body(chunk, t_valid, t_total, z_ref, xbc_ref, dt_ref, cp_ref, h0_ref, cw_ref, cb_ref, dtb_ref, alog_ref,
              dskip_ref, nw_ref, y_ref, ho_ref, hs_ref, xc_ref):
    L = chunk
    P = HEAD_DIM
    NS = SSM_STATE
    c = pl.program_id(1)
    nc = pl.num_programs(1)

    @pl.when(c == 0)
    def _():
        hs_ref[...] = h0_ref[0]
        xc_ref[0:8, :] = cp_ref[0]

    xc_ref[8:8 + L, :] = xbc_ref[...]
    conv = cb_ref[...]
    for i in range(SSM_CONV):
        conv = conv + xc_ref[5 + i:5 + i + L, :] * cw_ref[i:i + 1, :]
    xc_ref[0:8, :] = xc_ref[L:L + 8, :]
    act = conv * _sigmoid(conv)
    xs = act[:, 0:SSM_DIM]
    bm = act[:, SSM_DIM:SSM_DIM + SSM_GROUPS * NS]
    cm = act[:, SSM_DIM + SSM_GROUPS * NS:]

    dt = _softplus(dt_ref[...] + dtb_ref[...])
    if t_valid < t_total:
        live = (c * L + lax.broadcasted_iota(jnp.int32, dt.shape, 0)) < t_valid
        dt = jnp.where(live, dt, 0.0)
    a_neg = -jnp.exp(alog_ref[...])
    acum = _cumsum_rows(dt * a_neg)
    acum_t = acum.T
    dt_t = dt.T
    li = lax.broadcasted_iota(jnp.int32, (L, L), 0)
    si = lax.broadcasted_iota(jnp.int32, (L, L), 1)
    causal = li >= si
    cb = [_dot_nt(cm[:, g * NS:(g + 1) * NS], bm[:, g * NS:(g + 1) * NS]) for g in range(SSM_GROUPS)]
    rep = SSM_HEADS // SSM_GROUPS
    ys = []
    for h in range(SSM_HEADS):
        g = h // rep
        x_h = xs[:, h * P:(h + 1) * P]
        b_g = bm[:, g * NS:(g + 1) * NS]
        c_g = cm[:, g * NS:(g + 1) * NS]
        ac = acum[:, h:h + 1]
        a_last = acum[L - 1:L, h:h + 1]
        decay = jnp.exp(jnp.where(causal, ac - acum_t[h:h + 1, :], NEG_INF))
        scores = cb[g] * decay * dt_t[h:h + 1, :]
        h_st = hs_ref[h]
        y = _dot(scores, x_h) + _dot_nt(c_g, h_st) * jnp.exp(ac) + dskip_ref[:, h:h + 1] * x_h
        to_end = jnp.exp(a_last - ac) * dt[:, h:h + 1]
        hs_ref[h] = h_st * jnp.exp(a_last) + _dot_tn(x_h * to_end, b_g)
        ys.append(y)
    y = jnp.concatenate(ys, axis=1)
    z = z_ref[...]
    y = y * (z * _sigmoid(z))
    y_ref[...] = _rms(y, nw_ref[...])

    @pl.when(c == nc - 1)
    def _():
        ho_ref[0] = hs_ref[...]


def _ssd_call(z, xbc, dt, conv_prev8, h0, lp, n_batch, t_total, t_valid, chunk):
    L = chunk
    nc = t_total // L
    rows = lambda w: pl.BlockSpec((L, w), lambda b, c: (b * nc + c, 0))
    vec = lambda a, n: pl.BlockSpec((a, n), lambda b, c: (0, 0))
    in_specs = [rows(SSM_DIM), rows(SSM_CONV_DIM), rows(LANES),
                pl.BlockSpec((1, 8, SSM_CONV_DIM), lambda b, c: (b, 0, 0)),
                pl.BlockSpec((1, SSM_HEADS, HEAD_DIM, SSM_STATE), lambda b, c: (b, 0, 0, 0)),
                vec(SSM_CONV, SSM_CONV_DIM), vec(1, SSM_CONV_DIM), vec(1, LANES), vec(1, LANES), vec(1, LANES),
                vec(1, SSM_DIM)]
    out_shape = (jax.ShapeDtypeStruct((n_batch * t_total, SSM_DIM), F32),
                 jax.ShapeDtypeStruct((n_batch, SSM_HEADS, HEAD_DIM, SSM_STATE), F32))
    out_specs = (rows(SSM_DIM), pl.BlockSpec((1, SSM_HEADS, HEAD_DIM, SSM_STATE), lambda b, c: (b, 0, 0, 0)))
    return pl.pallas_call(
        functools.partial(_ssd_body, L, t_valid, t_total),
        grid=(n_batch, nc), in_specs=in_specs, out_specs=out_specs, out_shape=out_shape,
        scratch_shapes=[pltpu.VMEM((SSM_HEADS, HEAD_DIM, SSM_STATE), F32), pltpu.VMEM((L + 8, SSM_CONV_DIM), F32)],
        compiler_params=_params(("arbitrary", "arbitrary")), name="ssd")(
            z, xbc, dt, conv_prev8, h0, lp["ssm_conv_w"], lp["ssm_conv_b"], lp["ssm_dt_bias"], lp["ssm_A_log"],
            lp["ssm_D"], lp["ssm_norm_w"])


def _pad_lanes(v, n):
    return jnp.pad(v, ((0, 0), (0, n - v.shape[-1])))


def _pad_rows(x, n):
    return jnp.pad(x, ((0, n - x.shape[0]),) + ((0, 0),) * (x.ndim - 1))


def kernel(x_prompt, x_sample, cache_att_k, cache_att_v, state_rwkv_shift, state_rwkv, state_ssm_conv, state_ssm, ffn1_norm, ffn1_w_gate, ffn1_w_up, ffn1_w_down, mix_norm, w_in, rwkv_mu, rwkv_w0, rwkv_w2, rwkv_a0, rwkv_a2, rwkv_g2, rwkv_k_k, rwkv_k_a, rwkv_r_k, rwkv_ln_w, rwkv_ln_b, att_norm_w, ssm_conv_w, ssm_conv_b, ssm_dt_bias, ssm_A_log, ssm_D, ssm_norm_w, w_out, ffn2_norm, ffn2_w_gate, ffn2_w_up, ffn2_w_down, final_norm):
    depth = w_in.shape[0]
    bp, tp, _ = x_prompt.shape
    bs, ts, _ = x_sample.shape
    w_buf = cache_att_k.shape[2]
    mp = bp * tp
    ms = bs * ts
    assert mp % ROW_TILE == 0 and ms <= ROW_TILE and tp % (ATT_STEPS * DILATIONS[-1]) == 0 and ts <= 8
    n_ptiles = mp // ROW_TILE
    ts_rwkv = RWKV_CHUNK
    ts_ssd = 8

    x = jnp.concatenate([x_prompt.reshape(mp, D_MODEL), _pad_rows(x_sample.reshape(ms, D_MODEL), ROW_TILE)], axis=0)
    zeros_shift = jnp.zeros((bp, RWKV_PROJ), F32)
    zeros_s = jnp.zeros((bp, RWKV_HEADS, HEAD_DIM, HEAD_DIM), F32)
    zeros_conv = jnp.zeros((bp, 8, SSM_CONV_DIM), F32)
    zeros_h = jnp.zeros((bp, SSM_HEADS, HEAD_DIM, SSM_STATE), F32)

    outs = {k: [] for k in ("p_k", "p_v", "p_shift", "p_s", "p_conv", "p_h",
                            "s_k", "s_v", "s_shift", "s_s", "s_conv", "s_h")}
    mix = None
    wo = None
    y_prompt = y_sample = None
    for l in range(depth):
        row1 = lambda v: v[l].reshape(1, -1)
        lp = dict(rwkv_mu=row1(rwkv_mu), rwkv_w0=row1(rwkv_w0), rwkv_w2=rwkv_w2[l], rwkv_a0=row1(rwkv_a0),
                  rwkv_a2=rwkv_a2[l], rwkv_g2=rwkv_g2[l], rwkv_k_k=row1(rwkv_k_k), rwkv_k_a=row1(rwkv_k_a),
                  rwkv_r_k=row1(rwkv_r_k), rwkv_ln_w=row1(rwkv_ln_w), rwkv_ln_b=row1(rwkv_ln_b),
                  ssm_conv_w=ssm_conv_w[l], ssm_conv_b=row1(ssm_conv_b),
                  ssm_dt_bias=_pad_lanes(row1(ssm_dt_bias), LANES), ssm_A_log=_pad_lanes(row1(ssm_A_log), LANES),
                  ssm_D=_pad_lanes(row1(ssm_D), LANES), ssm_norm_w=row1(ssm_norm_w))
        bf = lambda w: w[l].astype(BF16)
        if l == 0:
            x = _ffn_call(x, None, None, row1(ffn1_norm), bf(ffn1_w_gate), bf(ffn1_w_up), bf(ffn1_w_down), None,
                          n_ptiles)
        (p_rwkv, q3, k3, v3, k_nat, v_nat, q_s, k_s, v_s, z, xbc, dt) = _proj_call(
            x, row1(mix_norm), _pad_lanes(w_in[l], PROJ_PAD).astype(BF16), n_ptiles)

        ya_p, s_p = _rwkv_call(p_rwkv, zeros_shift, zeros_s, lp, bp, tp, tp)
        yb_p = _att_prompt_call(q3, k3, v3, row1(att_norm_w), bp, tp)
        yc_p, h_p = _ssd_call(z, xbc, dt, zeros_conv, zeros_h, lp, bp, tp, tp, SSD_CHUNK)

        def sample_rows(a, t_pad):
            first = a.shape[0] - ROW_TILE
            a = a[first:first + ms].reshape(bs, ts, -1)
            return jnp.pad(a, ((0, 0), (0, t_pad - ts), (0, 0)))

        pr_s = sample_rows(p_rwkv, ts_rwkv)
        ya_s, s_s = _rwkv_call(pr_s.reshape(bs * ts_rwkv, RWKV_PROJ), state_rwkv_shift[l], state_rwkv[l], lp,
                               bs, ts_rwkv, ts)
        yb_s = _att_sample_call(sample_rows(q_s, 8), sample_rows(k_s, 8), sample_rows(v_s, 8),
                                cache_att_k[l].reshape(bs, w_buf, ATT_DIM), cache_att_v[l].reshape(bs, w_buf, ATT_DIM),
                                row1(att_norm_w), ts)
        conv8 = jnp.pad(state_ssm_conv[l], ((0, 0), (8 - (SSM_CONV - 1), 0), (0, 0)))
        flat8 = lambda a: sample_rows(a, ts_ssd).reshape(bs * ts_ssd, -1)
        yc_s, h_s = _ssd_call(flat8(z), flat8(xbc), flat8(dt), conv8, state_ssm[l], lp, bs, ts_ssd, ts, ts_ssd)

        def tile_rows(a, t_pad):
            return _pad_rows(a.reshape(bs, t_pad, -1)[:, :ts].reshape(ms, -1), ROW_TILE)

        mix = ((ya_p, tile_rows(ya_s, ts_rwkv)), (yb_p, tile_rows(yb_s, 8)), (yc_p, tile_rows(yc_s, ts_ssd)))
        wo = w_out[l].astype(BF16)

        pr_p = p_rwkv[:mp].reshape(bp, tp, RWKV_PROJ)
        xbc_p = xbc[:mp].reshape(bp, tp, SSM_CONV_DIM)
        keep = min(2048, tp)
        outs["p_k"].append(k_nat.reshape(bp, tp, ATT_HEADS, HEAD_DIM)[:, tp - keep:])
        outs["p_v"].append(v_nat.reshape(bp, tp, ATT_HEADS, HEAD_DIM)[:, tp - keep:])
        outs["p_shift"].append(pr_p[:, -1])
        outs["p_s"].append(s_p)
        outs["p_conv"].append(xbc_p[:, tp - (SSM_CONV - 1):])
        outs["p_h"].append(h_p)
        outs["s_k"].append(k_s[:ms].reshape(bs, ts, ATT_HEADS, HEAD_DIM))
        outs["s_v"].append(v_s[:ms].reshape(bs, ts, ATT_HEADS, HEAD_DIM))
        outs["s_shift"].append(pr_s[:, ts - 1])
        ext = jnp.concatenate([state_ssm_conv[l], xbc[mp:mp + ms].reshape(bs, ts, SSM_CONV_DIM)], axis=1)
        outs["s_s"].append(s_s)
        outs["s_conv"].append(ext[:, ts:])
        outs["s_h"].append(h_s)

        x = _ffn_call(x, mix, wo, row1(ffn2_norm), bf(ffn2_w_gate), bf(ffn2_w_up), bf(ffn2_w_down),
                      final_norm.reshape(1, -1) if l == depth - 1 else None, n_ptiles)
        if l == depth - 1:
            y_prompt, y_sample = x
        else:
            lnext = l + 1
            x = _ffn_call(x, None, None, ffn1_norm[lnext].reshape(1, -1), ffn1_w_gate[lnext].astype(BF16),
                          ffn1_w_up[lnext].astype(BF16), ffn1_w_down[lnext].astype(BF16), None, n_ptiles)

    st = lambda k: jnp.stack(outs[k])
    return (y_prompt.reshape(bp, tp, D_MODEL), y_sample[:ms].reshape(bs, ts, D_MODEL),
            st("p_k"), st("p_v"), st("p_shift"), st("p_s"), st("p_conv"), st("p_h"),
            st("s_k"), st("s_v"), st("s_shift"), st("s_s"), st("s_conv"), st("s_h"))
```

```python
import functools

import jax
import jax.numpy as jnp
from jax import lax
from jax.experimental import pallas as pl
from jax.experimental.pallas import tpu as pltpu

F32 = jnp.float32
BF16 = jnp.bfloat16

D_MODEL = 1024
FFN_DIM = 2816
HEAD_DIM = 64
RWKV_HEADS = 4
RWKV_DIM = RWKV_HEADS * HEAD_DIM
RWKV_PROJ = 1024
RWKV_GN_EPS = 64e-5
ATT_HEADS = 6
ATT_DIM = ATT_HEADS * HEAD_DIM
ATT_PAIRS = ATT_HEADS // 2
DILATIONS = (1, 4, 16)
ATT_STEPS = 128
SSM_HEADS = 6
SSM_DIM = SSM_HEADS * HEAD_DIM
SSM_GROUPS = 2
SSM_STATE = 128
SSM_CONV = 4
SSM_CONV_DIM = SSM_DIM + 2 * SSM_GROUPS * SSM_STATE
PROJ_DIM = RWKV_PROJ + 3 * ATT_DIM + SSM_DIM + SSM_CONV_DIM + SSM_HEADS
PROJ_PAD = 3584
NORM_EPS = 1e-6
NEG_INF = -1e30

LANES = 128
ROW_TILE = 512
FFN_CHUNK = 256
RWKV_CHUNK = 64
RWKV_GROUP = 4
SSD_CHUNK = 128
VMEM_LIMIT = 56 * 1024 * 1024

OFF_ATT = RWKV_PROJ
OFF_Z = OFF_ATT + 3 * ATT_DIM
OFF_XBC = OFF_Z + SSM_DIM
OFF_DT = OFF_XBC + SSM_CONV_DIM


def _params(sem):
    return pltpu.CompilerParams(dimension_semantics=sem, vmem_limit_bytes=VMEM_LIMIT)


def _dot(a, b):
    return jnp.dot(a.astype(BF16), b.astype(BF16), preferred_element_type=F32)


def _dot_nt(a, b):
    return lax.dot_general(a.astype(BF16), b.astype(BF16), (((1,), (1,)), ((), ())),
                           preferred_element_type=F32)


def _dot_tn(a, b):
    return lax.dot_general(a.astype(BF16), b.astype(BF16), (((0,), (0,)), ((), ())),
                           preferred_element_type=F32)


def _dot_hi(a, b):
    return jnp.dot(a, b, preferred_element_type=F32, precision=lax.Precision.HIGHEST)


def _rms(x, w):
    return x * lax.rsqrt(jnp.mean(x * x, axis=-1, keepdims=True) + NORM_EPS) * w


def _sigmoid(x):
    return 1.0 / (1.0 + jnp.exp(-x))


def _softplus(x):
    return jnp.maximum(x, 0.0) + jnp.log(1.0 + jnp.exp(-jnp.abs(x)))


def _div(x, n):
    assert n & (n - 1) == 0
    return lax.shift_right_logical(x, n.bit_length() - 1)


def _mod(x, n):
    assert n & (n - 1) == 0
    return x & (n - 1)


def _cumsum_rows(x):
    n = x.shape[0]
    row = lax.broadcasted_iota(jnp.int32, x.shape, 0)
    s = 1
    while s < n:
        x = x + jnp.where(row >= s, pltpu.roll(x, s, axis=0), 0.0)
        s *= 2
    return x


def _ffn_body(has_mix, final, n_prompt_tiles, *refs):
    it = iter(refs)
    x_ref = next(it)
    if has_mix:
        yap, yas, ybp, ybs, ycp, ycs, wo_ref = (next(it) for _ in range(7))
    nw_ref, wg_ref, wu_ref, wd_ref = (next(it) for _ in range(4))
    if final:
        fn_ref = next(it)
        op_ref, os_ref = next(it), next(it)
    else:
        o_ref = next(it)
    h_ref, acc_ref = next(it), next(it)
    i = pl.program_id(0)
    is_sample = i >= n_prompt_tiles

    x = x_ref[...]
    if has_mix:
        ya = jnp.where(is_sample, yas[...], yap[...])
        yb = jnp.where(is_sample, ybs[...], ybp[...])
        yc = jnp.where(is_sample, ycs[...], ycp[...])
        x = (x + _dot(ya, wo_ref[0:RWKV_DIM, :])
             + _dot(yb, wo_ref[RWKV_DIM:RWKV_DIM + ATT_DIM, :])
             + _dot(yc, wo_ref[RWKV_DIM + ATT_DIM:, :]))
    h_ref[...] = _rms(x, nw_ref[...]).astype(BF16)
    acc_ref[...] = x

    def chunk(c, carry):
        off = pl.multiple_of(c * FFN_CHUNK, FFN_CHUNK)
        h = h_ref[...]
        g = jnp.dot(h, wg_ref[:, pl.ds(off, FFN_CHUNK)], preferred_element_type=F32)
        u = jnp.dot(h, wu_ref[:, pl.ds(off, FFN_CHUNK)], preferred_element_type=F32)
        a = (g * _sigmoid(g) * u).astype(BF16)
        acc_ref[...] += 0.5 * jnp.dot(a, wd_ref[pl.ds(off, FFN_CHUNK), :], preferred_element_type=F32)
        return carry

    lax.fori_loop(0, FFN_DIM // FFN_CHUNK, chunk, 0)
    if final:
        y = _rms(acc_ref[...], fn_ref[...])

        @pl.when(jnp.logical_not(is_sample))
        def _():
            op_ref[...] = y

        @pl.when(is_sample)
        def _():
            os_ref[...] = y
    else:
        o_ref[...] = acc_ref[...]


def _ffn_call(x, mix, wo, nw, wg, wu, wd, final_norm, n_prompt_tiles):
    m_pad = x.shape[0]
    tm = ROW_TILE
    has_mix = mix is not None
    final = final_norm is not None
    last_p = n_prompt_tiles - 1
    row = lambda i: (i, 0)
    fixed = lambda i: (0, 0)
    prow = lambda i: (jnp.minimum(i, last_p), 0)
    resident = functools.partial(pl.BlockSpec, index_map=fixed, pipeline_mode=pl.Buffered(1))
    in_specs = [pl.BlockSpec((tm, D_MODEL), row)]
    args = [x]
    if has_mix:
        for yp, ys in mix:
            w = yp.shape[1]
            in_specs += [pl.BlockSpec((tm, w), prow), pl.BlockSpec((tm, w), fixed)]
            args += [yp, ys]
        in_specs.append(resident((D_MODEL, D_MODEL)))
        args.append(wo)
    in_specs += [pl.BlockSpec((1, D_MODEL), fixed), resident((D_MODEL, FFN_DIM)),
                 resident((D_MODEL, FFN_DIM)), resident((FFN_DIM, D_MODEL))]
    args += [nw, wg, wu, wd]
    if final:
        in_specs.append(pl.BlockSpec((1, D_MODEL), fixed))
        args.append(final_norm)
        out_shape = (jax.ShapeDtypeStruct((n_prompt_tiles * tm, D_MODEL), F32),
                     jax.ShapeDtypeStruct((tm, D_MODEL), F32))
        out_specs = (pl.BlockSpec((tm, D_MODEL), prow), pl.BlockSpec((tm, D_MODEL), fixed))
    else:
        out_shape = jax.ShapeDtypeStruct((m_pad, D_MODEL), F32)
        out_specs = pl.BlockSpec((tm, D_MODEL), row)
    return pl.pallas_call(
        functools.partial(_ffn_body, has_mix, final, n_prompt_tiles),
        grid=(m_pad // tm,), in_specs=in_specs, out_specs=out_specs, out_shape=out_shape,
        scratch_shapes=[pltpu.VMEM((tm, D_MODEL), BF16), pltpu.VMEM((tm, D_MODEL), F32)],
        compiler_params=_params(("arbitrary",)), name="ffn")(*args)


def _proj_body(n_prompt_tiles, x_ref, nw_ref, w_ref, pr_ref, q3_ref, k3_ref, v3_ref, kn_ref, vn_ref,
               qs_ref, ks_ref, vs_ref, z_ref, xbc_ref, dt_ref):
    i = pl.program_id(0)
    is_sample = i >= n_prompt_tiles
    h = _rms(x_ref[...], nw_ref[...]).astype(BF16)

    def seg(a, b):
        return jnp.dot(h, w_ref[:, a:b], preferred_element_type=F32)

    pr_ref[...] = seg(0, OFF_ATT)
    q = seg(OFF_ATT, OFF_ATT + ATT_DIM)
    k = seg(OFF_ATT + ATT_DIM, OFF_ATT + 2 * ATT_DIM)
    v = seg(OFF_ATT + 2 * ATT_DIM, OFF_Z)
    z_ref[...] = seg(OFF_Z, OFF_XBC)
    xbc_ref[...] = seg(OFF_XBC, OFF_DT)
    dt_ref[...] = seg(OFF_DT, PROJ_PAD)

    @pl.when(jnp.logical_not(is_sample))
    def _():
        kn_ref[...] = k
        vn_ref[...] = v
        for j in range(ATT_PAIRS):
            q3_ref[j] = q[:, j * LANES:(j + 1) * LANES]
            k3_ref[j] = k[:, j * LANES:(j + 1) * LANES]
            v3_ref[j] = v[:, j * LANES:(j + 1) * LANES]

    @pl.when(is_sample)
    def _():
        qs_ref[...] = q
        ks_ref[...] = k
        vs_ref[...] = v


def _proj_call(x, nw, w, n_prompt_tiles):
    m_pad = x.shape[0]
    tm = ROW_TILE
    mp = n_prompt_tiles * tm
    last_p = n_prompt_tiles - 1
    row = lambda i: (i, 0)
    fixed = lambda i: (0, 0)
    prow = lambda i: (jnp.minimum(i, last_p), 0)
    prow3 = lambda i: (0, jnp.minimum(i, last_p), 0)
    sds = jax.ShapeDtypeStruct
    out_shape = (sds((m_pad, RWKV_PROJ), F32),
                 sds((ATT_PAIRS, mp, LANES), F32), sds((ATT_PAIRS, mp, LANES), F32), sds((ATT_PAIRS, mp, LANES), F32),
                 sds((mp, ATT_DIM), F32), sds((mp, ATT_DIM), F32),
                 sds((tm, ATT_DIM), F32), sds((tm, ATT_DIM), F32), sds((tm, ATT_DIM), F32),
                 sds((m_pad, SSM_DIM), F32), sds((m_pad, SSM_CONV_DIM), F32), sds((m_pad, LANES), F32))
    p3 = pl.BlockSpec((ATT_PAIRS, tm, LANES), prow3)
    out_specs = (pl.BlockSpec((tm, RWKV_PROJ), row), p3, p3, p3,
                 pl.BlockSpec((tm, ATT_DIM), prow), pl.BlockSpec((tm, ATT_DIM), prow),
                 pl.BlockSpec((tm, ATT_DIM), fixed), pl.BlockSpec((tm, ATT_DIM), fixed),
                 pl.BlockSpec((tm, ATT_DIM), fixed),
                 pl.BlockSpec((tm, SSM_DIM), row), pl.BlockSpec((tm, SSM_CONV_DIM), row),
                 pl.BlockSpec((tm, LANES), row))
    return pl.pallas_call(
        functools.partial(_proj_body, n_prompt_tiles),
        grid=(m_pad // tm,),
        in_specs=[pl.BlockSpec((tm, D_MODEL), row), pl.BlockSpec((1, D_MODEL), fixed),
                  pl.BlockSpec((D_MODEL, PROJ_PAD), fixed, pipeline_mode=pl.Buffered(1))],
        out_specs=out_specs, out_shape=out_shape,
        compiler_params=_params(("arbitrary",)), name="proj")(x, nw, w)


def _rwkv_body(chunk, group, t_valid, t_total, p_ref, sh_ref, s0_ref, mu_ref, w0_ref, w2_ref, a0_ref, a2_ref, g2_ref,
               kk_ref, ka_ref, rk_ref, lnw_ref, lnb_ref, y_ref, so_ref, sb_ref, prev_ref):
    C = chunk
    G = group
    H = RWKV_HEADS
    N = HEAD_DIM
    R = RWKV_DIM
    GC = G * C
    c = pl.program_id(1)
    nc = pl.num_programs(1)

    @pl.when(c == 0)
    def _():
        sb_ref[...] = jnp.zeros((R, R), F32)
        for h in range(H):
            sb_ref[h * N:(h + 1) * N, h * N:(h + 1) * N] = s0_ref[0, h]
        prev_ref[...] = jnp.broadcast_to(sh_ref[0], (8, RWKV_PROJ))

    p = p_ref[...]
    row_p = lax.broadcasted_iota(jnp.int32, p.shape, 0)
    p_prev = jnp.where(row_p == 0, prev_ref[0:1, :], pltpu.roll(p, 1, axis=0))
    prev_ref[...] = jnp.broadcast_to(p[GC - 1:GC, :], (8, RWKV_PROJ))
    ps = p + mu_ref[...] * (p_prev - p)
    r = ps[:, 0:R]
    k = ps[:, R:2 * R]
    v = ps[:, 2 * R:3 * R]
    w_lr = ps[:, 3 * R:3 * R + 64]
    a_lr = ps[:, 3 * R + 64:3 * R + 128]
    g_lr = ps[:, 3 * R + 128:]

    w_log = -_softplus(-(w0_ref[...] + _dot(jnp.tanh(w_lr), w2_ref[...]))) - 0.5
    ld = -jnp.exp(w_log)
    a = _sigmoid(a0_ref[...] + _dot(a_lr, a2_ref[...]))
    g = _dot(_sigmoid(g_lr), g2_ref[...])

    lane_r = lax.broadcasted_iota(jnp.int32, (R, R), 1)
    row_r = lax.broadcasted_iota(jnp.int32, (R, R), 0)
    own_head = _div(lane_r, N) == _div(row_r, N)
    head_ones = jnp.where(own_head, 1.0, 0.0).astype(BF16)

    def head_sum(x):
        hi = x.astype(BF16)
        lo = (x - hi.astype(F32)).astype(BF16)
        return (jnp.dot(hi, head_ones, preferred_element_type=F32)
                + jnp.dot(lo, head_ones, preferred_element_type=F32))

    kk = k * kk_ref[...]
    kk = kk / jnp.maximum(jnp.sqrt(head_sum(kk * kk)), 1e-12)
    kp = k * (1.0 + (a - 1.0) * ka_ref[...])
    if t_valid < t_total:
        live = (c * GC + lax.broadcasted_iota(jnp.int32, (GC, R), 0)) < t_valid
        ld = jnp.where(live, ld, 0.0)
        kk = jnp.where(live, kk, 0.0)
        kp = jnp.where(live, kp, 0.0)
        v = jnp.where(live, v, 0.0)

    HC = H * C
    bd_row = _div(lax.broadcasted_iota(jnp.int32, (HC, R), 0), C)
    bd_lane = _div(lax.broadcasted_iota(jnp.int32, (HC, R), 1), N)

    def bd_rows(x):
        return jnp.where(bd_row == bd_lane, jnp.concatenate([x] * H, axis=0), 0.0)

    sq_row = _div(lax.broadcasted_iota(jnp.int32, (HC, HC), 0), C)
    sq_col = _div(lax.broadcasted_iota(jnp.int32, (HC, HC), 1), C)

    def mm(x, y):
        return _dot(x, jnp.where(sq_row == sq_col, jnp.concatenate([y] * H, axis=0), 0.0))

    t_idx = lax.broadcasted_iota(jnp.int32, (C, HC), 0)
    s_idx = _mod(lax.broadcasted_iota(jnp.int32, (C, HC), 1), C)
    strict = s_idx < t_idx
    incl = s_idx <= t_idx
    eye = jnp.where(s_idx == t_idx, 1.0, 0.0).astype(F32)
    same16 = _div(s_idx, 16) == _div(t_idx, 16)

    chunks = [slice(gi * C, (gi + 1) * C) for gi in range(G)]
    cum = [_cumsum_rows(ld[rows]) for rows in chunks]
    w_inc = [jnp.exp(x) for x in cum]
    w_inv = [jnp.exp(-x) for x in cum]
    w_end = [x[C - 1:C, :] for x in w_inc]
    b_t = [kk[rows] * a[rows] * wi for rows, wi in zip(chunks, w_inv)]
    k_t = [kp[rows] * wi for rows, wi in zip(chunks, w_inv)]
    lhs = [jnp.concatenate([kk[rows] * jnp.exp(cm - ld[rows]), r[rows] * wi], axis=0)
           for rows, cm, wi in zip(chunks, cum, w_inc)]
    sc = [_dot_nt(x, jnp.concatenate([bd_rows(bt), bd_rows(kt)], axis=0)) for x, bt, kt in zip(lhs, b_t, k_t)]
    a_ab = [jnp.where(strict, x[0:C, 0:HC], 0.0) for x in sc]
    a_ak = [jnp.where(strict, x[0:C, HC:], 0.0) for x in sc]
    p_rb = [jnp.where(incl, x[C:, 0:HC], 0.0) for x in sc]
    p_rk = [jnp.where(incl, x[C:, HC:], 0.0) for x in sc]
    e1 = [jnp.where(same16, -x, 0.0) for x in a_ab]
    n_low = [x + y for x, y in zip(a_ab, e1)]
    e2 = [mm(x, x) for x in e1]
    st = [mm(jnp.concatenate([eye + x, y], axis=0), y) for x, y in zip(e1, e2)]
    p4 = [eye + x + y[0:C] for x, y in zip(e1, st)]
    e4 = [y[C:] for y in st]
    st = [mm(jnp.concatenate([x, y], axis=0), y) for x, y in zip(p4, e4)]
    p8 = [x + y[0:C] for x, y in zip(p4, st)]
    e8 = [y[C:] for y in st]
    d_inv = [x + mm(x, y) for x, y in zip(p8, e8)]
    f1 = [-mm(x, y) for x, y in zip(d_inv, n_low)]
    f2 = [mm(x, x) for x in f1]
    q3 = [eye + x + mm(eye + x, y) for x, y in zip(f1, f2)]
    t_inv = [mm(x, y) for x, y in zip(q3, d_inv)]
    v_c = [v[rows] for rows in chunks]
    v_bd = [bd_rows(x) for x in v_c]
    avp = [_dot(jnp.concatenate([x, y], axis=0), z) for x, y, z in zip(a_ak, p_rk, v_bd)]
    tail = [jnp.concatenate([bt * we, kt * we], axis=0) for bt, kt, we in zip(b_t, k_t, w_end)]

    sb = sb_ref[...]
    ys = []
    for gi in range(G):
        ks_rs = _dot_nt(lhs[gi], sb)
        u = _dot(t_inv[gi], bd_rows(-ks_rs[0:C] - avp[gi][0:C]))
        upd = _dot_tn(jnp.concatenate([u, v_c[gi]], axis=0), tail[gi])
        sb = jnp.where(own_head, sb * w_end[gi] + upd, 0.0)
        ys.append(ks_rs[C:] + _dot(p_rb[gi], bd_rows(u)) + avp[gi][C:])
    sb_ref[...] = sb
    y = jnp.concatenate(ys, axis=0) if G > 1 else ys[0]

    mean = head_sum(y) * (1.0 / N)
    yc = y - mean
    var = head_sum(yc * yc) * (1.0 / N)
    yn = yc * lax.rsqrt(var + RWKV_GN_EPS) * lnw_ref[...] + lnb_ref[...]
    bonus = head_sum(r * kp * rk_ref[...]) * v
    y_ref[...] = (yn + bonus) * g

    @pl.when(c == nc - 1)
    def _():
        for h in range(H):
            so_ref[0, h] = sb_ref[h * N:(h + 1) * N, h * N:(h + 1) * N]


def _rwkv_call(p, shift_prev, s0, lp, n_batch, t_total, t_valid):
    C = RWKV_CHUNK
    G = min(RWKV_GROUP, t_total // C)
    nc = t_total // (G * C)
    vec = lambda n: pl.BlockSpec((1, n), lambda b, c: (0, 0))
    mat = lambda a, n: pl.BlockSpec((a, n), lambda b, c: (0, 0))
    in_specs = [pl.BlockSpec((G * C, RWKV_PROJ), lambda b, c: (b * nc + c, 0)),
                pl.BlockSpec((1, 1, RWKV_PROJ), lambda b, c: (b, 0, 0)),
                pl.BlockSpec((1, RWKV_HEADS, HEAD_DIM, HEAD_DIM), lambda b, c: (b, 0, 0, 0)),
                vec(RWKV_PROJ), vec(RWKV_DIM), mat(64, RWKV_DIM), vec(RWKV_DIM), mat(64, RWKV_DIM),
                mat(128, RWKV_DIM), vec(RWKV_DIM), vec(RWKV_DIM), vec(RWKV_DIM), vec(RWKV_DIM), vec(RWKV_DIM)]
    out_shape = (jax.ShapeDtypeStruct((n_batch * t_total, RWKV_DIM), F32),
                 jax.ShapeDtypeStruct((n_batch, RWKV_HEADS, HEAD_DIM, HEAD_DIM), F32))
    out_specs = (pl.BlockSpec((G * C, RWKV_DIM), lambda b, c: (b * nc + c, 0)),
                 pl.BlockSpec((1, RWKV_HEADS, HEAD_DIM, HEAD_DIM), lambda b, c: (b, 0, 0, 0)))
    return pl.pallas_call(
        functools.partial(_rwkv_body, C, G, t_valid, t_total),
        grid=(n_batch, nc), in_specs=in_specs, out_specs=out_specs, out_shape=out_shape,
        scratch_shapes=[pltpu.VMEM((RWKV_DIM, RWKV_DIM), F32), pltpu.VMEM((8, RWKV_PROJ), F32)],
        compiler_params=_params(("arbitrary", "arbitrary")), name="rwkv")(
            p, shift_prev.reshape(n_batch, 1, RWKV_PROJ), s0,
            lp["rwkv_mu"], lp["rwkv_w0"], lp["rwkv_w2"], lp["rwkv_a0"], lp["rwkv_a2"], lp["rwkv_g2"],
            lp["rwkv_k_k"], lp["rwkv_k_a"], lp["rwkv_r_k"], lp["rwkv_ln_w"], lp["rwkv_ln_b"])


def _alibi_slope(h):
    return 2.0 ** (-8.0 * (h + 1) / ATT_HEADS)


def _att_prompt_body(t_len, q_ref, k_ref, v_ref, nw_ref, o_ref, acc_ref, m_ref, l_ref):
    blk = ATT_STEPS
    for d in DILATIONS:
        n = t_len // d
        kw = min(2 * blk, n)
        nb = n // blk
        qi = lax.broadcasted_iota(jnp.int32, (blk, kw), 0)
        kj = lax.broadcasted_iota(jnp.int32, (blk, kw), 1)

        def step(it, carry, d=d, kw=kw, nb=nb, qi=qi, kj=kj):
            res = it // nb
            b = it % nb
            kb = jnp.maximum(b - 1, 0) if kw > blk else b
            q_rows = pl.ds(res + d * blk * b, blk, stride=d)
            k_rows = pl.ds(res + d * blk * kb, kw, stride=d)
            delta = (b - kb) * blk + qi - kj
            valid = (delta >= 0) & (delta <= blk)
            dist = (d * delta).astype(F32)
            heads = [(j, slice(hh * HEAD_DIM, (hh + 1) * HEAD_DIM)) for j in range(ATT_PAIRS) for hh in range(2)]
            q2 = [q_ref.at[j][q_rows, :] * (HEAD_DIM ** -0.5) for j in range(ATT_PAIRS)]
            k2 = [k_ref.at[j][k_rows, :] for j in range(ATT_PAIRS)]
            v2 = [v_ref.at[j][k_rows, :] for j in range(ATT_PAIRS)]
            s = [_dot_nt(q2[j][:, sl], k2[j][:, sl]) for j, sl in heads]
            s = [jnp.where(valid, x - _alibi_slope(h) * dist, NEG_INF) for h, x in enumerate(s)]
            mh = [jnp.max(x, axis=-1, keepdims=True) for x in s]
            pr = [jnp.exp(x - m) for x, m in zip(s, mh)]
            lh = [jnp.sum(x, axis=-1, keepdims=True) for x in pr]
            ah = [_dot(x, v2[j][:, sl]) for x, (j, sl) in zip(pr, heads)]
            results = []
            for j in range(ATT_PAIRS):
                acc = jnp.concatenate(ah[2 * j:2 * j + 2], axis=1)
                m = jnp.concatenate([jnp.broadcast_to(x, (blk, HEAD_DIM)) for x in mh[2 * j:2 * j + 2]], axis=1)
                l = jnp.concatenate([jnp.broadcast_to(x, (blk, HEAD_DIM)) for x in lh[2 * j:2 * j + 2]], axis=1)
                if d != DILATIONS[0]:
                    m_old = m_ref.at[j][q_rows, :]
                    m_new = jnp.maximum(m_old, m)
                    e_old = jnp.exp(m_old - m_new)
                    e_new = jnp.exp(m - m_new)
                    acc = acc_ref.at[j][q_rows, :] * e_old + acc * e_new
                    l = l_ref.at[j][q_rows, :] * e_old + l * e_new
                    m = m_new
                results.append((acc, m, l))
            for j, (acc, m, l) in enumerate(results):
                acc_ref.at[j][q_rows, :] = acc
                m_ref.at[j][q_rows, :] = m
                l_ref.at[j][q_rows, :] = l
            return carry

        lax.fori_loop(0, d * nb, step, 0)

    def finish(it, carry):
        rows = pl.ds(pl.multiple_of(it * blk, blk), blk)
        ys = [acc_ref[j, rows, :] / l_ref[j, rows, :] for j in range(ATT_PAIRS)]
        ms = sum(jnp.sum(y * y, axis=-1, keepdims=True) for y in ys) * (1.0 / ATT_DIM)
        scale = lax.rsqrt(ms + NORM_EPS)
        for j in range(ATT_PAIRS):
            o_ref[rows, j * LANES:(j + 1) * LANES] = ys[j] * scale * nw_ref[:, j * LANES:(j + 1) * LANES]
        return carry

    lax.fori_loop(0, t_len // blk, finish, 0)


def _att_prompt_call(q3, k3, v3, nw, n_batch, t_len):
    blk3 = pl.BlockSpec((ATT_PAIRS, t_len, LANES), lambda b: (0, b, 0))
    return pl.pallas_call(
        functools.partial(_att_prompt_body, t_len),
        grid=(n_batch,),
        in_specs=[blk3, blk3, blk3, pl.BlockSpec((1, ATT_DIM), lambda b: (0, 0))],
        out_specs=pl.BlockSpec((t_len, ATT_DIM), lambda b: (b, 0)),
        out_shape=jax.ShapeDtypeStruct((n_batch * t_len, ATT_DIM), F32),
        scratch_shapes=[pltpu.VMEM((ATT_PAIRS, t_len, LANES), F32)] * 3,
        compiler_params=_params(("arbitrary",)), name="att_prompt")(q3, k3, v3, nw)


def _multiplicity(dist):
    mult = jnp.zeros(dist.shape, F32)
    for d in DILATIONS:
        hit = (_mod(dist, d) == 0) & (dist <= d * ATT_STEPS)
        mult = mult + jnp.where(hit, 1.0, 0.0)
    return mult


def _att_sample_body(t_new, w_buf, q_ref, kn_ref, vn_ref, kb_ref, vb_ref, nw_ref, o_ref):
    RQ = 8
    rows = ATT_HEADS * RQ
    q = q_ref[0] * (HEAD_DIM ** -0.5)
    head_of_row = _div(lax.broadcasted_iota(jnp.int32, (rows, ATT_DIM), 0), RQ)
    head_of_lane = _div(lax.broadcasted_iota(jnp.int32, (rows, ATT_DIM), 1), HEAD_DIM)
    own = head_of_row == head_of_lane
    q_bd = jnp.where(own, jnp.concatenate([q] * ATT_HEADS, axis=0), 0.0)

    s_buf = _dot_nt(q_bd, kb_ref[0])
    s_new = _dot_nt(q_bd, kn_ref[0])

    def weights(s, key_pos):
        t = _mod(lax.broadcasted_iota(jnp.int32, s.shape, 0), RQ)
        h = _div(lax.broadcasted_iota(jnp.int32, s.shape, 0), RQ)
        dist = w_buf + t - key_pos
        slope = sum(jnp.where(h == hh, _alibi_slope(hh), 0.0) for hh in range(ATT_HEADS))
        mult = jnp.where(dist >= 0, _multiplicity(jnp.maximum(dist, 0)), 0.0)
        return jnp.where(mult > 0, s - slope * dist.astype(F32), NEG_INF), mult

    sb, mult_b = weights(s_buf, lax.broadcasted_iota(jnp.int32, s_buf.shape, 1))
    sn, mult_n = weights(s_new, w_buf + lax.broadcasted_iota(jnp.int32, s_new.shape, 1))
    m = jnp.maximum(jnp.max(sb, axis=-1, keepdims=True), jnp.max(sn, axis=-1, keepdims=True))
    pb = mult_b * jnp.exp(sb - m)
    pn = mult_n * jnp.exp(sn - m)
    den = jnp.sum(pb, axis=-1, keepdims=True) + jnp.sum(pn, axis=-1, keepdims=True)
    num = _dot(pb, vb_ref[0]) + _dot(pn, vn_ref[0])
    out = jnp.where(own, num / den, 0.0).reshape(ATT_HEADS, RQ, ATT_DIM)
    y = jnp.sum(out, axis=0)
    o_ref[0] = _rms(y, nw_ref[...])


def _att_sample_call(q, k_new, v_new, k_buf, v_buf, nw, t_new):
    n_batch, w_buf, _ = k_buf.shape
    small = pl.BlockSpec((1, 8, ATT_DIM), lambda b: (b, 0, 0))
    big = pl.BlockSpec((1, w_buf, ATT_DIM), lambda b: (b, 0, 0))
    return pl.pallas_call(
        functools.partial(_att_sample_body, t_new, w_buf),
        grid=(n_batch,),
        in_specs=[small, small, small, big, big, pl.BlockSpec((1, ATT_DIM), lambda b: (0, 0))],
        out_specs=small, out_shape=jax.ShapeDtypeStruct((n_batch, 8, ATT_DIM), F32),
        compiler_params=_params(("arbitrary",)), name="att_sample")(q, k_new, v_new, k_buf, v_buf, nw)


def _ssd_body(chunk, t_valid, t_total, z_ref, xbc_ref, dt_ref, cp_ref, h0_ref, cw_ref, cb_ref, dtb_ref, alog_ref,
              dskip_ref, nw_ref, y_ref, ho_ref, hs_ref, xc_ref):
    L = chunk
    P = HEAD_DIM
    NS = SSM_STATE
    c = pl.program_id(1)
    nc = pl.num_programs(1)

    @pl.when(c == 0)
    def _():
        hs_ref[...] = h0_ref[0]
        xc_ref[0:8, :] = cp_ref[0]

    xc_ref[8:8 + L, :] = xbc_ref[...]
    conv = cb_ref[...]
    for i in range(SSM_CONV):
        conv = conv + xc_ref[5 + i:5 + i + L, :] * cw_ref[i:i + 1, :]
    xc_ref[0:8, :] = xc_ref[L:L + 8, :]
    act = conv * _sigmoid(conv)
    xs = act[:, 0:SSM_DIM]
    bm = act[:, SSM_DIM:SSM_DIM + SSM_GROUPS * NS]
    cm = act[:, SSM_DIM + SSM_GROUPS * NS:]

    dt = _softplus(dt_ref[...] + dtb_ref[...])
    if t_valid < t_total:
        live = (c * L + lax.broadcasted_iota(jnp.int32, dt.shape, 0)) < t_valid
        dt = jnp.where(live, dt, 0.0)
    a_neg = -jnp.exp(alog_ref[...])
    acum = _cumsum_rows(dt * a_neg)
    acum_t = acum.T
    dt_t = dt.T
    li = lax.broadcasted_iota(jnp.int32, (L, L), 0)
    si = lax.broadcasted_iota(jnp.int32, (L, L), 1)
    causal = li >= si
    cb = [_dot_nt(cm[:, g * NS:(g + 1) * NS], bm[:, g * NS:(g + 1) * NS]) for g in range(SSM_GROUPS)]
    rep = SSM_HEADS // SSM_GROUPS
    ys = []
    for h in range(SSM_HEADS):
        g = h // rep
        x_h = xs[:, h * P:(h + 1) * P]
        b_g = bm[:, g * NS:(g + 1) * NS]
        c_g = cm[:, g * NS:(g + 1) * NS]
        ac = acum[:, h:h + 1]
        a_last = acum[L - 1:L, h:h + 1]
        decay = jnp.exp(jnp.where(causal, ac - acum_t[h:h + 1, :], NEG_INF))
        scores = cb[g] * decay * dt_t[h:h + 1, :]
        h_st = hs_ref[h]
        y = _dot(scores, x_h) + _dot_nt(c_g, h_st) * jnp.exp(ac) + dskip_ref[:, h:h + 1] * x_h
        to_end = jnp.exp(a_last - ac) * dt[:, h:h + 1]
        hs_ref[h] = h_st * jnp.exp(a_last) + _dot_tn(x_h * to_end, b_g)
        ys.append(y)
    y = jnp.concatenate(ys, axis=1)
    z = z_ref[...]
    y = y * (z * _sigmoid(z))
    y_ref[...] = _rms(y, nw_ref[...])

    @pl.when(c == nc - 1)
    def _():
        ho_ref[0] = hs_ref[...]


def _ssd_call(z, xbc, dt, conv_prev8, h0, lp, n_batch, t_total, t_valid, chunk):
    L = chunk
    nc = t_total // L
    rows = lambda w: pl.BlockSpec((L, w), lambda b, c: (b * nc + c, 0))
    vec = lambda a, n: pl.BlockSpec((a, n), lambda b, c: (0, 0))
    in_specs = [rows(SSM_DIM), rows(SSM_CONV_DIM), rows(LANES),
                pl.BlockSpec((1, 8, SSM_CONV_DIM), lambda b, c: (b, 0, 0)),
                pl.BlockSpec((1, SSM_HEADS, HEAD_DIM, SSM_STATE), lambda b, c: (b, 0, 0, 0)),
                vec(SSM_CONV, SSM_CONV_DIM), vec(1, SSM_CONV_DIM), vec(1, LANES), vec(1, LANES), vec(1, LANES),
                vec(1, SSM_DIM)]
    out_shape = (jax.ShapeDtypeStruct((n_batch * t_total, SSM_DIM), F32),
                 jax.ShapeDtypeStruct((n_batch, SSM_HEADS, HEAD_DIM, SSM_STATE), F32))
    out_specs = (rows(SSM_DIM), pl.BlockSpec((1, SSM_HEADS, HEAD_DIM, SSM_STATE), lambda b, c: (b, 0, 0, 0)))
    return pl.pallas_call(
        functools.partial(_ssd_body, L, t_valid, t_total),
        grid=(n_batch, nc), in_specs=in_specs, out_specs=out_specs, out_shape=out_shape,
        scratch_shapes=[pltpu.VMEM((SSM_HEADS, HEAD_DIM, SSM_STATE), F32), pltpu.VMEM((L + 8, SSM_CONV_DIM), F32)],
        compiler_params=_params(("arbitrary", "arbitrary")), name="ssd")(
            z, xbc, dt, conv_prev8, h0, lp["ssm_conv_w"], lp["ssm_conv_b"], lp["ssm_dt_bias"], lp["ssm_A_log"],
            lp["ssm_D"], lp["ssm_norm_w"])


def _pad_lanes(v, n):
    return jnp.pad(v, ((0, 0), (0, n - v.shape[-1])))


def _pad_rows(x, n):
    return jnp.pad(x, ((0, n - x.shape[0]),) + ((0, 0),) * (x.ndim - 1))


def kernel(x_prompt, x_sample, cache_att_k, cache_att_v, state_rwkv_shift, state_rwkv, state_ssm_conv, state_ssm, ffn1_norm, ffn1_w_gate, ffn1_w_up, ffn1_w_down, mix_norm, w_in, rwkv_mu, rwkv_w0, rwkv_w2, rwkv_a0, rwkv_a2, rwkv_g2, rwkv_k_k, rwkv_k_a, rwkv_r_k, rwkv_ln_w, rwkv_ln_b, att_norm_w, ssm_conv_w, ssm_conv_b, ssm_dt_bias, ssm_A_log, ssm_D, ssm_norm_w, w_out, ffn2_norm, ffn2_w_gate, ffn2_w_up, ffn2_w_down, final_norm):
    depth = w_in.shape[0]
    bp, tp, _ = x_prompt.shape
    bs, ts, _ = x_sample.shape
    w_buf = cache_att_k.shape[2]
    mp = bp * tp
    ms = bs * ts
    assert mp % ROW_TILE == 0 and ms <= ROW_TILE and tp % (ATT_STEPS * DILATIONS[-1]) == 0 and ts <= 8
    n_ptiles = mp // ROW_TILE
    ts_rwkv = RWKV_CHUNK
    ts_ssd = 8

    x = jnp.concatenate([x_prompt.reshape(mp, D_MODEL), _pad_rows(x_sample.reshape(ms, D_MODEL), ROW_TILE)], axis=0)
    zeros_shift = jnp.zeros((bp, RWKV_PROJ), F32)
    zeros_s = jnp.zeros((bp, RWKV_HEADS, HEAD_DIM, HEAD_DIM), F32)
    zeros_conv = jnp.zeros((bp, 8, SSM_CONV_DIM), F32)
    zeros_h = jnp.zeros((bp, SSM_HEADS, HEAD_DIM, SSM_STATE), F32)

    outs = {k: [] for k in ("p_k", "p_v", "p_shift", "p_s", "p_conv", "p_h",
                            "s_k", "s_v", "s_shift", "s_s", "s_conv", "s_h")}
    mix = None
    wo = None
    y_prompt = y_sample = None
    for l in range(depth):
        row1 = lambda v: v[l].reshape(1, -1)
        lp = dict(rwkv_mu=row1(rwkv_mu), rwkv_w0=row1(rwkv_w0), rwkv_w2=rwkv_w2[l], rwkv_a0=row1(rwkv_a0),
                  rwkv_a2=rwkv_a2[l], rwkv_g2=rwkv_g2[l], rwkv_k_k=row1(rwkv_k_k), rwkv_k_a=row1(rwkv_k_a),
                  rwkv_r_k=row1(rwkv_r_k), rwkv_ln_w=row1(rwkv_ln_w), rwkv_ln_b=row1(rwkv_ln_b),
                  ssm_conv_w=ssm_conv_w[l], ssm_conv_b=row1(ssm_conv_b),
                  ssm_dt_bias=_pad_lanes(row1(ssm_dt_bias), LANES), ssm_A_log=_pad_lanes(row1(ssm_A_log), LANES),
                  ssm_D=_pad_lanes(row1(ssm_D), LANES), ssm_norm_w=row1(ssm_norm_w))
        bf = lambda w: w[l].astype(BF16)
        if l == 0:
            x = _ffn_call(x, None, None, row1(ffn1_norm), bf(ffn1_w_gate), bf(ffn1_w_up), bf(ffn1_w_down), None,
                          n_ptiles)
        (p_rwkv, q3, k3, v3, k_nat, v_nat, q_s, k_s, v_s, z, xbc, dt) = _proj_call(
            x, row1(mix_norm), _pad_lanes(w_in[l], PROJ_PAD).astype(BF16), n_ptiles)

        ya_p, s_p = _rwkv_call(p_rwkv, zeros_shift, zeros_s, lp, bp, tp, tp)
        yb_p = _att_prompt_call(q3, k3, v3, row1(att_norm_w), bp, tp)
        yc_p, h_p = _ssd_call(z, xbc, dt, zeros_conv, zeros_h, lp, bp, tp, tp, SSD_CHUNK)

        def sample_rows(a, t_pad):
            first = a.shape[0] - ROW_TILE
            a = a[first:first + ms].reshape(bs, ts, -1)
            return jnp.pad(a, ((0, 0), (0, t_pad - ts), (0, 0)))

        pr_s = sample_rows(p_rwkv, ts_rwkv)
        ya_s, s_s = _rwkv_call(pr_s.reshape(bs * ts_rwkv, RWKV_PROJ), state_rwkv_shift[l], state_rwkv[l], lp,
                               bs, ts_rwkv, ts)
        yb_s = _att_sample_call(sample_rows(q_s, 8), sample_rows(k_s, 8), sample_rows(v_s, 8),
                                cache_att_k[l].reshape(bs, w_buf, ATT_DIM), cache_att_v[l].reshape(bs, w_buf, ATT_DIM),
                                row1(att_norm_w), ts)
        conv8 = jnp.pad(state_ssm_conv[l], ((0, 0), (8 - (SSM_CONV - 1), 0), (0, 0)))
        flat8 = lambda a: sample_rows(a, ts_ssd).reshape(bs * ts_ssd, -1)
        yc_s, h_s = _ssd_call(flat8(z), flat8(xbc), flat8(dt), conv8, state_ssm[l], lp, bs, ts_ssd, ts, ts_ssd)

        def tile_rows(a, t_pad):
            return _pad_rows(a.reshape(bs, t_pad, -1)[:, :ts].reshape(ms, -1), ROW_TILE)

        mix = ((ya_p, tile_rows(ya_s, ts_rwkv)), (yb_p, tile_rows(yb_s, 8)), (yc_p, tile_rows(yc_s, ts_ssd)))
        wo = w_out[l].astype(BF16)

        pr_p = p_rwkv[:mp].reshape(bp, tp, RWKV_PROJ)
        xbc_p = xbc[:mp].reshape(bp, tp, SSM_CONV_DIM)
        keep = min(2048, tp)
        outs["p_k"].append(k_nat.reshape(bp, tp, ATT_HEADS, HEAD_DIM)[:, tp - keep:])
        outs["p_v"].append(v_nat.reshape(bp, tp, ATT_HEADS, HEAD_DIM)[:, tp - keep:])
        outs["p_shift"].append(pr_p[:, -1])
        outs["p_s"].append(s_p)
        outs["p_conv"].append(xbc_p[:, tp - (SSM_CONV - 1):])
        outs["p_h"].append(h_p)
        outs["s_k"].append(k_s[:ms].reshape(bs, ts, ATT_HEADS, HEAD_DIM))
        outs["s_v"].append(v_s[:ms].reshape(bs, ts, ATT_HEADS, HEAD_DIM))
        outs["s_shift"].append(pr_s[:, ts - 1])
        ext = jnp.concatenate([state_ssm_conv[l], xbc[mp:mp + ms].reshape(bs, ts, SSM_CONV_DIM)], axis=1)
        outs["s_s"].append(s_s)
        outs["s_conv"].append(ext[:, ts:])
        outs["s_h"].append(h_s)

        x = _ffn_call(x, mix, wo, row1(ffn2_norm), bf(ffn2_w_gate), bf(ffn2_w_up), bf(ffn2_w_down),
                      final_norm.reshape(1, -1) if l == depth - 1 else None, n_ptiles)
        if l == depth - 1:
            y_prompt, y_sample = x
        else:
            lnext = l + 1
            x = _ffn_call(x, None, None, ffn1_norm[lnext].reshape(1, -1), ffn1_w_gate[lnext].astype(BF16),
                          ffn1_w_up[lnext].astype(BF16), ffn1_w_down[lnext].astype(BF16), None, n_ptiles)

    st = lambda k: jnp.stack(outs[k])
    return (y_prompt.reshape(bp, tp, D_MODEL), y_sample[:ms].reshape(bs, ts, D_MODEL),
            st("p_k"), st("p_v"), st("p_shift"), st("p_s"), st("p_conv"), st("p_h"),
            st("s_k"), st("s_v"), st("s_shift"), st("s_s"), st("s_conv"), st("s_h"))
```

```python
import functools

import jax
import jax.numpy as jnp
from jax import lax
from jax.experimental import pallas as pl
from jax.experimental.pallas import tpu as pltpu

F32 = jnp.float32
BF16 = jnp.bfloat16

D_MODEL = 1024
FFN_DIM = 2816
HEAD_DIM = 64
RWKV_HEADS = 4
RWKV_DIM = RWKV_HEADS * HEAD_DIM
RWKV_PROJ = 1024
RWKV_GN_EPS = 64e-5
ATT_HEADS = 6
ATT_DIM = ATT_HEADS * HEAD_DIM
ATT_PAIRS = ATT_HEADS // 2
DILATIONS = (1, 4, 16)
ATT_STEPS = 128
SSM_HEADS = 6
SSM_DIM = SSM_HEADS * HEAD_DIM
SSM_GROUPS = 2
SSM_STATE = 128
SSM_CONV = 4
SSM_CONV_DIM = SSM_DIM + 2 * SSM_GROUPS * SSM_STATE
PROJ_DIM = RWKV_PROJ + 3 * ATT_DIM + SSM_DIM + SSM_CONV_DIM + SSM_HEADS
PROJ_PAD = 3584
NORM_EPS = 1e-6
NEG_INF = -1e30

LANES = 128
ROW_TILE = 512
FFN_CHUNK = 256
RWKV_CHUNK = 64
RWKV_GROUP = 4
SSD_CHUNK = 128
VMEM_LIMIT = 56 * 1024 * 1024

OFF_ATT = RWKV_PROJ
OFF_Z = OFF_ATT + 3 * ATT_DIM
OFF_XBC = OFF_Z + SSM_DIM
OFF_DT = OFF_XBC + SSM_CONV_DIM


def _params(sem):
    return pltpu.CompilerParams(dimension_semantics=sem, vmem_limit_bytes=VMEM_LIMIT)


def _dot(a, b):
    return jnp.dot(a.astype(BF16), b.astype(BF16), preferred_element_type=F32)


def _dot_nt(a, b):
    return lax.dot_general(a.astype(BF16), b.astype(BF16), (((1,), (1,)), ((), ())),
                           preferred_element_type=F32)


def _dot_tn(a, b):
    return lax.dot_general(a.astype(BF16), b.astype(BF16), (((0,), (0,)), ((), ())),
                           preferred_element_type=F32)


def _dot_hi(a, b):
    return jnp.dot(a, b, preferred_element_type=F32, precision=lax.Precision.HIGHEST)


def _rms(x, w):
    return x * lax.rsqrt(jnp.mean(x * x, axis=-1, keepdims=True) + NORM_EPS) * w


def _sigmoid(x):
    return 1.0 / (1.0 + jnp.exp(-x))


def _softplus(x):
    return jnp.maximum(x, 0.0) + jnp.log(1.0 + jnp.exp(-jnp.abs(x)))


def _div(x, n):
    assert n & (n - 1) == 0
    return lax.shift_right_logical(x, n.bit_length() - 1)


def _mod(x, n):
    assert n & (n - 1) == 0
    return x & (n - 1)


def _cumsum_rows(x):
    n = x.shape[0]
    row = lax.broadcasted_iota(jnp.int32, x.shape, 0)
    s = 1
    while s < n:
        x = x + jnp.where(row >= s, pltpu.roll(x, s, axis=0), 0.0)
        s *= 2
    return x


def _ffn_body(has_mix, final, n_prompt_tiles, *refs):
    it = iter(refs)
    x_ref = next(it)
    if has_mix:
        yap, yas, ybp, ybs, ycp, ycs, wo_ref = (next(it) for _ in range(7))
    nw_ref, wg_ref, wu_ref, wd_ref = (next(it) for _ in range(4))
    if final:
        fn_ref = next(it)
        op_ref, os_ref = next(it), next(it)
    else:
        o_ref = next(it)
    h_ref, acc_ref = next(it), next(it)
    i = pl.program_id(0)
    is_sample = i >= n_prompt_tiles

    x = x_ref[...]
    if has_mix:
        ya = jnp.where(is_sample, yas[...], yap[...])
        yb = jnp.where(is_sample, ybs[...], ybp[...])
        yc = jnp.where(is_sample, ycs[...], ycp[...])
        x = (x + _dot(ya, wo_ref[0:RWKV_DIM, :])
             + _dot(yb, wo_ref[RWKV_DIM:RWKV_DIM + ATT_DIM, :])
             + _dot(yc, wo_ref[RWKV_DIM + ATT_DIM:, :]))
    h_ref[...] = _rms(x, nw_ref[...]).astype(BF16)
    acc_ref[...] = x

    def chunk(c, carry):
        off = pl.multiple_of(c * FFN_CHUNK, FFN_CHUNK)
        h = h_ref[...]
        g = jnp.dot(h, wg_ref[:, pl.ds(off, FFN_CHUNK)], preferred_element_type=F32)
        u = jnp.dot(h, wu_ref[:, pl.ds(off, FFN_CHUNK)], preferred_element_type=F32)
        a = (g * _sigmoid(g) * u).astype(BF16)
        acc_ref[...] += 0.5 * jnp.dot(a, wd_ref[pl.ds(off, FFN_CHUNK), :], preferred_element_type=F32)
        return carry

    lax.fori_loop(0, FFN_DIM // FFN_CHUNK, chunk, 0)
    if final:
        y = _rms(acc_ref[...], fn_ref[...])

        @pl.when(jnp.logical_not(is_sample))
        def _():
            op_ref[...] = y

        @pl.when(is_sample)
        def _():
            os_ref[...] = y
    else:
        o_ref[...] = acc_ref[...]


def _ffn_call(x, mix, wo, nw, wg, wu, wd, final_norm, n_prompt_tiles):
    m_pad = x.shape[0]
    tm = ROW_TILE
    has_mix = mix is not None
    final = final_norm is not None
    last_p = n_prompt_tiles - 1
    row = lambda i: (i, 0)
    fixed = lambda i: (0, 0)
    prow = lambda i: (jnp.minimum(i, last_p), 0)
    resident = functools.partial(pl.BlockSpec, index_map=fixed, pipeline_mode=pl.Buffered(1))
    in_specs = [pl.BlockSpec((tm, D_MODEL), row)]
    args = [x]
    if has_mix:
        for yp, ys in mix:
            w = yp.shape[1]
            in_specs += [pl.BlockSpec((tm, w), prow), pl.BlockSpec((tm, w), fixed)]
            args += [yp, ys]
        in_specs.append(resident((D_MODEL, D_MODEL)))
        args.append(wo)
    in_specs += [pl.BlockSpec((1, D_MODEL), fixed), resident((D_MODEL, FFN_DIM)),
                 resident((D_MODEL, FFN_DIM)), resident((FFN_DIM, D_MODEL))]
    args += [nw, wg, wu, wd]
    if final:
        in_specs.append(pl.BlockSpec((1, D_MODEL), fixed))
        args.append(final_norm)
        out_shape = (jax.ShapeDtypeStruct((n_prompt_tiles * tm, D_MODEL), F32),
                     jax.ShapeDtypeStruct((tm, D_MODEL), F32))
        out_specs = (pl.BlockSpec((tm, D_MODEL), prow), pl.BlockSpec((tm, D_MODEL), fixed))
    else:
        out_shape = jax.ShapeDtypeStruct((m_pad, D_MODEL), F32)
        out_specs = pl.BlockSpec((tm, D_MODEL), row)
    return pl.pallas_call(
        functools.partial(_ffn_body, has_mix, final, n_prompt_tiles),
        grid=(m_pad // tm,), in_specs=in_specs, out_specs=out_specs, out_shape=out_shape,
        scratch_shapes=[pltpu.VMEM((tm, D_MODEL), BF16), pltpu.VMEM((tm, D_MODEL), F32)],
        compiler_params=_params(("arbitrary",)), name="ffn")(*args)


def _proj_body(n_prompt_tiles, x_ref, nw_ref, w_ref, pr_ref, q3_ref, k3_ref, v3_ref, kn_ref, vn_ref,
               qs_ref, ks_ref, vs_ref, z_ref, xbc_ref, dt_ref):
    i = pl.program_id(0)
    is_sample = i >= n_prompt_tiles
    h = _rms(x_ref[...], nw_ref[...]).astype(BF16)

    def seg(a, b):
        return jnp.dot(h, w_ref[:, a:b], preferred_element_type=F32)

    pr_ref[...] = seg(0, OFF_ATT)
    q = seg(OFF_ATT, OFF_ATT + ATT_DIM)
    k = seg(OFF_ATT + ATT_DIM, OFF_ATT + 2 * ATT_DIM)
    v = seg(OFF_ATT + 2 * ATT_DIM, OFF_Z)
    z_ref[...] = seg(OFF_Z, OFF_XBC)
    xbc_ref[...] = seg(OFF_XBC, OFF_DT)
    dt_ref[...] = seg(OFF_DT, PROJ_PAD)

    @pl.when(jnp.logical_not(is_sample))
    def _():
        kn_ref[...] = k
        vn_ref[...] = v
        for j in range(ATT_PAIRS):
            q3_ref[j] = q[:, j * LANES:(j + 1) * LANES]
            k3_ref[j] = k[:, j * LANES:(j + 1) * LANES]
            v3_ref[j] = v[:, j * LANES:(j + 1) * LANES]

    @pl.when(is_sample)
    def _():
        qs_ref[...] = q
        ks_ref[...] = k
        vs_ref[...] = v


def _proj_call(x, nw, w, n_prompt_tiles):
    m_pad = x.shape[0]
    tm = ROW_TILE
    mp = n_prompt_tiles * tm
    last_p = n_prompt_tiles - 1
    row = lambda i: (i, 0)
    fixed = lambda i: (0, 0)
    prow = lambda i: (jnp.minimum(i, last_p), 0)
    prow3 = lambda i: (0, jnp.minimum(i, last_p), 0)
    sds = jax.ShapeDtypeStruct
    out_shape = (sds((m_pad, RWKV_PROJ), F32),
                 sds((ATT_PAIRS, mp, LANES), F32), sds((ATT_PAIRS, mp, LANES), F32), sds((ATT_PAIRS, mp, LANES), F32),
                 sds((mp, ATT_DIM), F32), sds((mp, ATT_DIM), F32),
                 sds((tm, ATT_DIM), F32), sds((tm, ATT_DIM), F32), sds((tm, ATT_DIM), F32),
                 sds((m_pad, SSM_DIM), F32), sds((m_pad, SSM_CONV_DIM), F32), sds((m_pad, LANES), F32))
    p3 = pl.BlockSpec((ATT_PAIRS, tm, LANES), prow3)
    out_specs = (pl.BlockSpec((tm, RWKV_PROJ), row), p3, p3, p3,
                 pl.BlockSpec((tm, ATT_DIM), prow), pl.BlockSpec((tm, ATT_DIM), prow),
                 pl.BlockSpec((tm, ATT_DIM), fixed), pl.BlockSpec((tm, ATT_DIM), fixed),
                 pl.BlockSpec((tm, ATT_DIM), fixed),
                 pl.BlockSpec((tm, SSM_DIM), row), pl.BlockSpec((tm, SSM_CONV_DIM), row),
                 pl.BlockSpec((tm, LANES), row))
    return pl.pallas_call(
        functools.partial(_proj_body, n_prompt_tiles),
        grid=(m_pad // tm,),
        in_specs=[pl.BlockSpec((tm, D_MODEL), row), pl.BlockSpec((1, D_MODEL), fixed),
                  pl.BlockSpec((D_MODEL, PROJ_PAD), fixed, pipeline_mode=pl.Buffered(1))],
        out_specs=out_specs, out_shape=out_shape,
        compiler_params=_params(("arbitrary",)), name="proj")(x, nw, w)


def _rwkv_body(chunk, group, by_batch, t_valid, t_total, p_ref, sh_ref, s0_ref, mu_ref, w0_ref, w2_ref, a0_ref, a2_ref,
               g2_ref, kk_ref, ka_ref, rk_ref, lnw_ref, lnb_ref, y_ref, so_ref, sb_ref, prev_ref):
    C = chunk
    G = group
    H = RWKV_HEADS
    N = HEAD_DIM
    R = RWKV_DIM
    GC = G * C
    c = pl.program_id(1)
    nc = pl.num_programs(1)
    p = p_ref[...]
    row_p = lax.broadcasted_iota(jnp.int32, p.shape, 0)
    p_prev = pltpu.roll(p, 1, axis=0)

    if by_batch:
        for gi in range(G):
            p_prev = jnp.where(row_p == gi * C, sh_ref[gi], p_prev)
        zero = jnp.zeros((N, N), F32)
        states = [jnp.concatenate([jnp.concatenate([s0_ref[gi, h] if hh == h else zero for hh in range(H)], axis=1)
                                   for h in range(H)], axis=0) for gi in range(G)]
    else:
        @pl.when(c == 0)
        def _():
            sb_ref[...] = jnp.zeros((R, R), F32)
            for h in range(H):
                sb_ref[h * N:(h + 1) * N, h * N:(h + 1) * N] = s0_ref[0, h]
            prev_ref[...] = jnp.broadcast_to(sh_ref[0], (8, RWKV_PROJ))

        p_prev = jnp.where(row_p == 0, prev_ref[0:1, :], p_prev)
        prev_ref[...] = jnp.broadcast_to(p[GC - 1:GC, :], (8, RWKV_PROJ))
    ps = p + mu_ref[...] * (p_prev - p)
    r = ps[:, 0:R]
    k = ps[:, R:2 * R]
    v = ps[:, 2 * R:3 * R]
    w_lr = ps[:, 3 * R:3 * R + 64]
    a_lr = ps[:, 3 * R + 64:3 * R + 128]
    g_lr = ps[:, 3 * R + 128:]

    w_log = -_softplus(-(w0_ref[...] + _dot(jnp.tanh(w_lr), w2_ref[...]))) - 0.5
    ld = -jnp.exp(w_log)
    a = _sigmoid(a0_ref[...] + _dot(a_lr, a2_ref[...]))
    g = _dot(_sigmoid(g_lr), g2_ref[...])

    lane_r = lax.broadcasted_iota(jnp.int32, (R, R), 1)
    row_r = lax.broadcasted_iota(jnp.int32, (R, R), 0)
    own_head = _div(lane_r, N) == _div(row_r, N)
    head_ones = jnp.where(own_head, 1.0, 0.0).astype(BF16)

    def head_sum(x):
        hi = x.astype(BF16)
        lo = (x - hi.astype(F32)).astype(BF16)
        return (jnp.dot(hi, head_ones, preferred_element_type=F32)
                + jnp.dot(lo, head_ones, preferred_element_type=F32))

    kk = k * kk_ref[...]
    kk = kk / jnp.maximum(jnp.sqrt(head_sum(kk * kk)), 1e-12)
    kp = k * (1.0 + (a - 1.0) * ka_ref[...])
    if t_valid < t_total:
        row_r2 = lax.broadcasted_iota(jnp.int32, (GC, R), 0)
        live = (_mod(row_r2, C) if by_batch else c * GC + row_r2) < t_valid
        ld = jnp.where(live, ld, 0.0)
        kk = jnp.where(live, kk, 0.0)
        kp = jnp.where(live, kp, 0.0)
        v = jnp.where(live, v, 0.0)

    HC = H * C
    bd_row = _div(lax.broadcasted_iota(jnp.int32, (HC, R), 0), C)
    bd_lane = _div(lax.broadcasted_iota(jnp.int32, (HC, R), 1), N)

    def bd_rows(x):
        return jnp.where(bd_row == bd_lane, jnp.concatenate([x] * H, axis=0), 0.0)

    sq_row = _div(lax.broadcasted_iota(jnp.int32, (HC, HC), 0), C)
    sq_col = _div(lax.broadcasted_iota(jnp.int32, (HC, HC), 1), C)

    def mm(x, y):
        return _dot(x, jnp.where(sq_row == sq_col, jnp.concatenate([y] * H, axis=0), 0.0))

    t_idx = lax.broadcasted_iota(jnp.int32, (C, HC), 0)
    s_idx = _mod(lax.broadcasted_iota(jnp.int32, (C, HC), 1), C)
    strict = s_idx < t_idx
    incl = s_idx <= t_idx
    eye = jnp.where(s_idx == t_idx, 1.0, 0.0).astype(F32)
    same16 = _div(s_idx, 16) == _div(t_idx, 16)

    chunks = [slice(gi * C, (gi + 1) * C) for gi in range(G)]
    cum = [_cumsum_rows(ld[rows]) for rows in chunks]
    w_inc = [jnp.exp(x) for x in cum]
    w_inv = [jnp.exp(-x) for x in cum]
    w_end = [x[C - 1:C, :] for x in w_inc]
    b_t = [kk[rows] * a[rows] * wi for rows, wi in zip(chunks, w_inv)]
    k_t = [kp[rows] * wi for rows, wi in zip(chunks, w_inv)]
    lhs = [jnp.concatenate([kk[rows] * jnp.exp(cm - ld[rows]), r[rows] * wi], axis=0)
           for rows, cm, wi in zip(chunks, cum, w_inc)]
    sc = [_dot_nt(x, jnp.concatenate([bd_rows(bt), bd_rows(kt)], axis=0)) for x, bt, kt in zip(lhs, b_t, k_t)]
    a_ab = [jnp.where(strict, x[0:C, 0:HC], 0.0) for x in sc]
    a_ak = [jnp.where(strict, x[0:C, HC:], 0.0) for x in sc]
    p_rb = [jnp.where(incl, x[C:, 0:HC], 0.0) for x in sc]
    p_rk = [jnp.where(incl, x[C:, HC:], 0.0) for x in sc]
    e1 = [jnp.where(same16, -x, 0.0) for x in a_ab]
    n_low = [x + y for x, y in zip(a_ab, e1)]
    e2 = [mm(x, x) for x in e1]
    st = [mm(jnp.concatenate([eye + x, y], axis=0), y) for x, y in zip(e1, e2)]
    p4 = [eye + x + y[0:C] for x, y in zip(e1, st)]
    e4 = [y[C:] for y in st]
    st = [mm(jnp.concatenate([x, y], axis=0), y) for x, y in zip(p4, e4)]
    p8 = [x + y[0:C] for x, y in zip(p4, st)]
    e8 = [y[C:] for y in st]
    d_inv = [x + mm(x, y) for x, y in zip(p8, e8)]
    f1 = [-mm(x, y) for x, y in zip(d_inv, n_low)]
    f2 = [mm(x, x) for x in f1]
    q3 = [eye + x + mm(eye + x, y) for x, y in zip(f1, f2)]
    t_inv = [mm(x, y) for x, y in zip(q3, d_inv)]
    v_c = [v[rows] for rows in chunks]
    v_bd = [bd_rows(x) for x in v_c]
    avp = [_dot(jnp.concatenate([x, y], axis=0), z) for x, y, z in zip(a_ak, p_rk, v_bd)]
    tail = [jnp.concatenate([bt * we, kt * we], axis=0) for bt, kt, we in zip(b_t, k_t, w_end)]

    def chain(gi, sb):
        ks_rs = _dot_nt(lhs[gi], sb)
        u = _dot(t_inv[gi], bd_rows(-ks_rs[0:C] - avp[gi][0:C]))
        upd = _dot_tn(jnp.concatenate([u, v_c[gi]], axis=0), tail[gi])
        y_c = ks_rs[C:] + _dot(p_rb[gi], bd_rows(u)) + avp[gi][C:]
        return jnp.where(own_head, sb * w_end[gi] + upd, 0.0), y_c

    ys = []
    if by_batch:
        for gi in range(G):
            sb, y_c = chain(gi, states[gi])
            ys.append(y_c)
            for h in range(H):
                so_ref[gi, h] = sb[h * N:(h + 1) * N, h * N:(h + 1) * N]
    else:
        sb = sb_ref[...]
        for gi in range(G):
            sb, y_c = chain(gi, sb)
            ys.append(y_c)
        sb_ref[...] = sb

        @pl.when(c == nc - 1)
        def _():
            for h in range(H):
                so_ref[0, h] = sb_ref[h * N:(h + 1) * N, h * N:(h + 1) * N]
    y = jnp.concatenate(ys, axis=0) if G > 1 else ys[0]

    mean = head_sum(y) * (1.0 / N)
    yc = y - mean
    var = head_sum(yc * yc) * (1.0 / N)
    yn = yc * lax.rsqrt(var + RWKV_GN_EPS) * lnw_ref[...] + lnb_ref[...]
    bonus = head_sum(r * kp * rk_ref[...]) * v
    y_ref[...] = (yn + bonus) * g


def _rwkv_call(p, shift_prev, s0, lp, n_batch, t_total, t_valid):
    C = RWKV_CHUNK
    by_batch = t_total == C
    G = RWKV_GROUP
    assert (n_batch if by_batch else t_total // C) % G == 0
    nc = 1 if by_batch else t_total // (G * C)
    n_outer = n_batch // G if by_batch else n_batch
    gb = G if by_batch else 1
    vec = lambda n: pl.BlockSpec((1, n), lambda b, c: (0, 0))
    mat = lambda a, n: pl.BlockSpec((a, n), lambda b, c: (0, 0))
    in_specs = [pl.BlockSpec((G * C, RWKV_PROJ), lambda b, c: (b * nc + c, 0)),
                pl.BlockSpec((gb, 1, RWKV_PROJ), lambda b, c: (b, 0, 0)),
                pl.BlockSpec((gb, RWKV_HEADS, HEAD_DIM, HEAD_DIM), lambda b, c: (b, 0, 0, 0)),
                vec(RWKV_PROJ), vec(RWKV_DIM), mat(64, RWKV_DIM), vec(RWKV_DIM), mat(64, RWKV_DIM),
                mat(128, RWKV_DIM), vec(RWKV_DIM), vec(RWKV_DIM), vec(RWKV_DIM), vec(RWKV_DIM), vec(RWKV_DIM)]
    out_shape = (jax.ShapeDtypeStruct((n_batch * t_total, RWKV_DIM), F32),
                 jax.ShapeDtypeStruct((n_batch, RWKV_HEADS, HEAD_DIM, HEAD_DIM), F32))
    out_specs = (pl.BlockSpec((G * C, RWKV_DIM), lambda b, c: (b * nc + c, 0)),
                 pl.BlockSpec((gb, RWKV_HEADS, HEAD_DIM, HEAD_DIM), lambda b, c: (b, 0, 0, 0)))
    return pl.pallas_call(
        functools.partial(_rwkv_body, C, G, by_batch, t_valid, t_total),
        grid=(n_outer, nc), in_specs=in_specs, out_specs=out_specs, out_shape=out_shape,
        scratch_shapes=[pltpu.VMEM((RWKV_DIM, RWKV_DIM), F32), pltpu.VMEM((8, RWKV_PROJ), F32)],
        compiler_params=_params(("arbitrary", "arbitrary")), name="rwkv")(
            p, shift_prev.reshape(n_batch, 1, RWKV_PROJ), s0,
            lp["rwkv_mu"], lp["rwkv_w0"], lp["rwkv_w2"], lp["rwkv_a0"], lp["rwkv_a2"], lp["rwkv_g2"],
            lp["rwkv_k_k"], lp["rwkv_k_a"], lp["rwkv_r_k"], lp["rwkv_ln_w"], lp["rwkv_ln_b"])


def _alibi_slope(h):
    return 2.0 ** (-8.0 * (h + 1) / ATT_HEADS)


def _att_prompt_body(t_len, q_ref, k_ref, v_ref, nw_ref, o_ref, acc_ref, m_ref, l_ref):
    blk = ATT_STEPS
    for d in DILATIONS:
        n = t_len // d
        kw = min(2 * blk, n)
        nb = n // blk
        qi = _mod(lax.broadcasted_iota(jnp.int32, (2 * blk, kw), 0), blk)
        kj = lax.broadcasted_iota(jnp.int32, (2 * blk, kw), 1)
        top = lax.broadcasted_iota(jnp.int32, (2 * blk, kw), 0) < blk
        lane_top = lax.broadcasted_iota(jnp.int32, (blk, LANES), 1) < HEAD_DIM

        def step(it, carry, d=d, kw=kw, nb=nb, qi=qi, kj=kj, top=top, lane_top=lane_top):
            res = it // nb
            b = it % nb
            kb = jnp.maximum(b - 1, 0) if kw > blk else b
            q_rows = pl.ds(res + d * blk * b, blk, stride=d)
            k_rows = pl.ds(res + d * blk * kb, kw, stride=d)
            delta = (b - kb) * blk + qi - kj
            valid = (delta >= 0) & (delta <= blk)
            dist = (d * delta).astype(F32)
            pairs = range(ATT_PAIRS)
            q2 = [q_ref.at[j][q_rows, :] * (HEAD_DIM ** -0.5) for j in pairs]
            k2 = [k_ref.at[j][k_rows, :] for j in pairs]
            v2 = [v_ref.at[j][k_rows, :] for j in pairs]
            q_st = [jnp.concatenate([jnp.where(lane_top, x, 0.0), jnp.where(lane_top, 0.0, x)], axis=0) for x in q2]
            s = [_dot_nt(x, y) for x, y in zip(q_st, k2)]
            slope = [jnp.where(top, _alibi_slope(2 * j), _alibi_slope(2 * j + 1)) for j in pairs]
            s = [jnp.where(valid, x - sl * dist, NEG_INF) for x, sl in zip(s, slope)]
            mh = [jnp.max(x, axis=-1, keepdims=True) for x in s]
            pr = [jnp.exp(x - m) for x, m in zip(s, mh)]
            lh = [jnp.sum(x, axis=-1, keepdims=True) for x in pr]
            ah = [_dot(x, y) for x, y in zip(pr, v2)]

            def unstack(x):
                x = jnp.broadcast_to(x, (2 * blk, LANES))
                return jnp.where(lane_top, x[0:blk], x[blk:])

            results = []
            for j in pairs:
                acc, m, l = unstack(ah[j]), unstack(mh[j]), unstack(lh[j])
                if d != DILATIONS[0]:
                    m_old = m_ref.at[j][q_rows, :]
                    m_new = jnp.maximum(m_old, m)
                    e_old = jnp.exp(m_old - m_new)
                    e_new = jnp.exp(m - m_new)
                    acc = acc_ref.at[j][q_rows, :] * e_old + acc * e_new
                    l = l_ref.at[j][q_rows, :] * e_old + l * e_new
                    m = m_new
                results.append((acc, m, l))
            for j, (acc, m, l) in enumerate(results):
                acc_ref.at[j][q_rows, :] = acc
                m_ref.at[j][q_rows, :] = m
                l_ref.at[j][q_rows, :] = l
            return carry

        lax.fori_loop(0, d * nb, step, 0)

    def finish(it, carry):
        rows = pl.ds(pl.multiple_of(it * blk, blk), blk)
        ys = [acc_ref[j, rows, :] / l_ref[j, rows, :] for j in range(ATT_PAIRS)]
        ms = sum(jnp.sum(y * y, axis=-1, keepdims=True) for y in ys) * (1.0 / ATT_DIM)
        scale = lax.rsqrt(ms + NORM_EPS)
        for j in range(ATT_PAIRS):
            o_ref[rows, j * LANES:(j + 1) * LANES] = ys[j] * scale * nw_ref[:, j * LANES:(j + 1) * LANES]
        return carry

    lax.fori_loop(0, t_len // blk, finish, 0)


def _att_prompt_call(q3, k3, v3, nw, n_batch, t_len):
    blk3 = pl.BlockSpec((ATT_PAIRS, t_len, LANES), lambda b: (0, b, 0))
    return pl.pallas_call(
        functools.partial(_att_prompt_body, t_len),
        grid=(n_batch,),
        in_specs=[blk3, blk3, blk3, pl.BlockSpec((1, ATT_DIM), lambda b: (0, 0))],
        out_specs=pl.BlockSpec((t_len, ATT_DIM), lambda b: (b, 0)),
        out_shape=jax.ShapeDtypeStruct((n_batch * t_len, ATT_DIM), F32),
        scratch_shapes=[pltpu.VMEM((ATT_PAIRS, t_len, LANES), F32)] * 3,
        compiler_params=_params(("arbitrary",)), name="att_prompt")(q3, k3, v3, nw)


def _multiplicity(dist):
    mult = jnp.zeros(dist.shape, F32)
    for d in DILATIONS:
        hit = (_mod(dist, d) == 0) & (dist <= d * ATT_STEPS)
        mult = mult + jnp.where(hit, 1.0, 0.0)
    return mult


def _att_sample_body(t_new, w_buf, q_ref, kn_ref, vn_ref, ka_ref, kb_ref, va_ref, vb_ref, nw_ref, o_ref):
    RQ = 8
    rows = ATT_HEADS * RQ
    near = ATT_STEPS * DILATIONS[1]
    far = DILATIONS[2]
    n_res = kb_ref.shape[3]
    n_b = kb_ref.shape[2] * n_res
    pad_new = LANES
    q = q_ref[0] * (HEAD_DIM ** -0.5)
    head_of_row = _div(lax.broadcasted_iota(jnp.int32, (rows, ATT_DIM), 0), RQ)
    head_of_lane = _div(lax.broadcasted_iota(jnp.int32, (rows, ATT_DIM), 1), HEAD_DIM)
    own = head_of_row == head_of_lane
    q_bd = jnp.where(own, jnp.concatenate([q] * ATT_HEADS, axis=0), 0.0)

    def rows_of(a_ref, b_ref, new_ref):
        return jnp.concatenate([a_ref[0, 0].reshape(near, ATT_DIM), b_ref[0, 0].reshape(n_b, ATT_DIM), new_ref[0],
                                jnp.zeros((pad_new - RQ, ATT_DIM), F32)], axis=0)

    s = _dot_nt(q_bd, rows_of(ka_ref, kb_ref, kn_ref))
    col = lax.broadcasted_iota(jnp.int32, s.shape, 1)
    t = _mod(lax.broadcasted_iota(jnp.int32, s.shape, 0), RQ)
    h = _div(lax.broadcasted_iota(jnp.int32, s.shape, 0), RQ)
    in_a = col < near
    in_b = jnp.logical_and(col >= near, col < near + n_b)
    cb = jnp.maximum(col - near, 0)
    pos = jnp.where(in_a, w_buf - near + col,
                    jnp.where(in_b, far * _div(cb, n_res) + _mod(cb, n_res), w_buf + col - (near + n_b)))
    dist = w_buf + t - pos
    seen_in_a = jnp.logical_and(in_b, pos >= w_buf - near)
    reach = jnp.logical_and(dist >= 0, jnp.logical_not(seen_in_a))
    mult = jnp.where(reach, _multiplicity(jnp.maximum(dist, 0)), 0.0)
    slope = sum(jnp.where(h == hh, _alibi_slope(hh), 0.0) for hh in range(ATT_HEADS))
    s = jnp.where(mult > 0, s - slope * dist.astype(F32), NEG_INF)
    m = jnp.max(s, axis=-1, keepdims=True)
    pr = mult * jnp.exp(s - m)
    den = jnp.sum(pr, axis=-1, keepdims=True)
    num = _dot(pr, rows_of(va_ref, vb_ref, vn_ref))
    out = jnp.where(own, num / den, 0.0).reshape(ATT_HEADS, RQ, ATT_DIM)
    o_ref[0] = _rms(jnp.sum(out, axis=0), nw_ref[...])


def _att_sample_call(q, k_new, v_new, cache_k, cache_v, layer, nw, t_new):
    depth, n_batch, w_buf, _, _ = cache_k.shape
    near = ATT_STEPS * DILATIONS[1]
    far = DILATIONS[2]
    n_far = w_buf // far
    assert t_new <= DILATIONS[1] and w_buf % near == 0 and n_far & (n_far - 1) == 0
    small = pl.BlockSpec((1, 8, ATT_DIM), lambda b: (b, 0, 0))
    blk_a = pl.BlockSpec((1, 1, near, ATT_HEADS, HEAD_DIM), lambda b: (layer, b, w_buf // near - 1, 0, 0))
    blk_b = pl.BlockSpec((1, 1, n_far, DILATIONS[1], ATT_HEADS, HEAD_DIM), lambda b: (layer, b, 0, 0, 0, 0))
    by_res = lambda c: c.reshape(depth, n_batch, n_far, far, ATT_HEADS, HEAD_DIM)
    return pl.pallas_call(
        functools.partial(_att_sample_body, t_new, w_buf),
        grid=(n_batch,),
        in_specs=[small, small, small, blk_a, blk_b, blk_a, blk_b, pl.BlockSpec((1, ATT_DIM), lambda b: (0, 0))],
        out_specs=small, out_shape=jax.ShapeDtypeStruct((n_batch, 8, ATT_DIM), F32),
        compiler_params=_params(("arbitrary",)), name="att_sample")(
            q, k_new, v_new, cache_k, by_res(cache_k), cache_v, by_res(cache_v), nw)


def _ssd_body(chunk, t_valid, t_total, z_ref, xbc_ref, dt_ref, cp_ref, h0_ref, cw_ref, cb_ref, dtb_ref, alog_ref,
              dskip_ref, nw_ref, y_ref, ho_ref, hs_ref, xc_ref):
    L = chunk
    P = HEAD_DIM
    NS = SSM_STATE
    c = pl.program_id(1)
    nc = pl.num_programs(1)

    @pl.when(c == 0)
    def _():
        hs_ref[...] = h0_ref[0]
        xc_ref[0:8, :] = cp_ref[0]

    xc_ref[8:8 + L, :] = xbc_ref[...]
    conv = cb_ref[...]
    for i in range(SSM_CONV):
        conv = conv + xc_ref[5 + i:5 + i + L, :] * cw_ref[i:i + 1, :]
    xc_ref[0:8, :] = xc_ref[L:L + 8, :]
    act = conv * _sigmoid(conv)
    xs = act[:, 0:SSM_DIM]
    bm = act[:, SSM_DIM:SSM_DIM + SSM_GROUPS * NS]
    cm = act[:, SSM_DIM + SSM_GROUPS * NS:]

    dt = _softplus(dt_ref[...] + dtb_ref[...])
    if t_valid < t_total:
        live = (c * L + lax.broadcasted_iota(jnp.int32, dt.shape, 0)) < t_valid
        dt = jnp.where(live, dt, 0.0)
    a_neg = -jnp.exp(alog_ref[...])
    acum = _cumsum_rows(dt * a_neg)
    acum_t = acum.T
    dt_t = dt.T
    li = lax.broadcasted_iota(jnp.int32, (L, L), 0)
    si = lax.broadcasted_iota(jnp.int32, (L, L), 1)
    causal = li >= si
    cb = [_dot_nt(cm[:, g * NS:(g + 1) * NS], bm[:, g * NS:(g + 1) * NS]) for g in range(SSM_GROUPS)]
    rep = SSM_HEADS // SSM_GROUPS
    ys = []
    for h in range(SSM_HEADS):
        g = h // rep
        x_h = xs[:, h * P:(h + 1) * P]
        b_g = bm[:, g * NS:(g + 1) * NS]
        c_g = cm[:, g * NS:(g + 1) * NS]
        ac = acum[:, h:h + 1]
        a_last = acum[L - 1:L, h:h + 1]
        decay = jnp.exp(jnp.where(causal, ac - acum_t[h:h + 1, :], NEG_INF))
        scores = cb[g] * decay * dt_t[h:h + 1, :]
        h_st = hs_ref[h]
        y = _dot(scores, x_h) + _dot_nt(c_g, h_st) * jnp.exp(ac) + dskip_ref[:, h:h + 1] * x_h
        to_end = jnp.exp(a_last - ac) * dt[:, h:h + 1]
        hs_ref[h] = h_st * jnp.exp(a_last) + _dot_tn(x_h * to_end, b_g)
        ys.append(y)
    y = jnp.concatenate(ys, axis=1)
    z = z_ref[...]
    y = y * (z * _sigmoid(z))
    y_ref[...] = _rms(y, nw_ref[...])

    @pl.when(c == nc - 1)
    def _():
        ho_ref[0] = hs_ref[...]


def _ssd_call(z, xbc, dt, conv_prev8, h0, lp, n_batch, t_total, t_valid, chunk):
    L = chunk
    nc = t_total // L
    rows = lambda w: pl.BlockSpec((L, w), lambda b, c: (b * nc + c, 0))
    vec = lambda a, n: pl.BlockSpec((a, n), lambda b, c: (0, 0))
    in_specs = [rows(SSM_DIM), rows(SSM_CONV_DIM), rows(LANES),
                pl.BlockSpec((1, 8, SSM_CONV_DIM), lambda b, c: (b, 0, 0)),
                pl.BlockSpec((1, SSM_HEADS, HEAD_DIM, SSM_STATE), lambda b, c: (b, 0, 0, 0)),
                vec(SSM_CONV, SSM_CONV_DIM), vec(1, SSM_CONV_DIM), vec(1, LANES), vec(1, LANES), vec(1, LANES),
                vec(1, SSM_DIM)]
    out_shape = (jax.ShapeDtypeStruct((n_batch * t_total, SSM_DIM), F32),
                 jax.ShapeDtypeStruct((n_batch, SSM_HEADS, HEAD_DIM, SSM_STATE), F32))
    out_specs = (rows(SSM_DIM), pl.BlockSpec((1, SSM_HEADS, HEAD_DIM, SSM_STATE), lambda b, c: (b, 0, 0, 0)))
    return pl.pallas_call(
        functools.partial(_ssd_body, L, t_valid, t_total),
        grid=(n_batch, nc), in_specs=in_specs, out_specs=out_specs, out_shape=out_shape,
        scratch_shapes=[pltpu.VMEM((SSM_HEADS, HEAD_DIM, SSM_STATE), F32), pltpu.VMEM((L + 8, SSM_CONV_DIM), F32)],
        compiler_params=_params(("arbitrary", "arbitrary")), name="ssd")(
            z, xbc, dt, conv_prev8, h0, lp["ssm_conv_w"], lp["ssm_conv_b"], lp["ssm_dt_bias"], lp["ssm_A_log"],
            lp["ssm_D"], lp["ssm_norm_w"])


def _pad_lanes(v, n):
    return jnp.pad(v, ((0, 0), (0, n - v.shape[-1])))


def _pad_rows(x, n):
    return jnp.pad(x, ((0, n - x.shape[0]),) + ((0, 0),) * (x.ndim - 1))


def kernel(x_prompt, x_sample, cache_att_k, cache_att_v, state_rwkv_shift, state_rwkv, state_ssm_conv, state_ssm, ffn1_norm, ffn1_w_gate, ffn1_w_up, ffn1_w_down, mix_norm, w_in, rwkv_mu, rwkv_w0, rwkv_w2, rwkv_a0, rwkv_a2, rwkv_g2, rwkv_k_k, rwkv_k_a, rwkv_r_k, rwkv_ln_w, rwkv_ln_b, att_norm_w, ssm_conv_w, ssm_conv_b, ssm_dt_bias, ssm_A_log, ssm_D, ssm_norm_w, w_out, ffn2_norm, ffn2_w_gate, ffn2_w_up, ffn2_w_down, final_norm):
    depth = w_in.shape[0]
    bp, tp, _ = x_prompt.shape
    bs, ts, _ = x_sample.shape
    w_buf = cache_att_k.shape[2]
    mp = bp * tp
    ms = bs * ts
    assert mp % ROW_TILE == 0 and ms <= ROW_TILE and tp % (ATT_STEPS * DILATIONS[-1]) == 0 and ts <= 8
    n_ptiles = mp // ROW_TILE
    ts_rwkv = RWKV_CHUNK
    ts_ssd = 8

    x = jnp.concatenate([x_prompt.reshape(mp, D_MODEL), _pad_rows(x_sample.reshape(ms, D_MODEL), ROW_TILE)], axis=0)
    zeros_shift = jnp.zeros((bp, RWKV_PROJ), F32)
    zeros_s = jnp.zeros((bp, RWKV_HEADS, HEAD_DIM, HEAD_DIM), F32)
    zeros_conv = jnp.zeros((bp, 8, SSM_CONV_DIM), F32)
    zeros_h = jnp.zeros((bp, SSM_HEADS, HEAD_DIM, SSM_STATE), F32)

    outs = {k: [] for k in ("p_k", "p_v", "p_shift", "p_s", "p_conv", "p_h",
                            "s_k", "s_v", "s_shift", "s_s", "s_conv", "s_h")}
    mix = None
    wo = None
    y_prompt = y_sample = None
    for l in range(depth):
        row1 = lambda v: v[l].reshape(1, -1)
        lp = dict(rwkv_mu=row1(rwkv_mu), rwkv_w0=row1(rwkv_w0), rwkv_w2=rwkv_w2[l], rwkv_a0=row1(rwkv_a0),
                  rwkv_a2=rwkv_a2[l], rwkv_g2=rwkv_g2[l], rwkv_k_k=row1(rwkv_k_k), rwkv_k_a=row1(rwkv_k_a),
                  rwkv_r_k=row1(rwkv_r_k), rwkv_ln_w=row1(rwkv_ln_w), rwkv_ln_b=row1(rwkv_ln_b),
                  ssm_conv_w=ssm_conv_w[l], ssm_conv_b=row1(ssm_conv_b),
                  ssm_dt_bias=_pad_lanes(row1(ssm_dt_bias), LANES), ssm_A_log=_pad_lanes(row1(ssm_A_log), LANES),
                  ssm_D=_pad_lanes(row1(ssm_D), LANES), ssm_norm_w=row1(ssm_norm_w))
        bf = lambda w: w[l].astype(BF16)
        if l == 0:
            x = _ffn_call(x, None, None, row1(ffn1_norm), bf(ffn1_w_gate), bf(ffn1_w_up), bf(ffn1_w_down), None,
                          n_ptiles)
        (p_rwkv, q3, k3, v3, k_nat, v_nat, q_s, k_s, v_s, z, xbc, dt) = _proj_call(
            x, row1(mix_norm), _pad_lanes(w_in[l], PROJ_PAD).astype(BF16), n_ptiles)

        ya_p, s_p = _rwkv_call(p_rwkv, zeros_shift, zeros_s, lp, bp, tp, tp)
        yb_p = _att_prompt_call(q3, k3, v3, row1(att_norm_w), bp, tp)
        yc_p, h_p = _ssd_call(z, xbc, dt, zeros_conv, zeros_h, lp, bp, tp, tp, SSD_CHUNK)

        def sample_rows(a, t_pad):
            first = a.shape[0] - ROW_TILE
            a = a[first:first + ms].reshape(bs, ts, -1)
            return jnp.pad(a, ((0, 0), (0, t_pad - ts), (0, 0)))

        pr_s = sample_rows(p_rwkv, ts_rwkv)
        ya_s, s_s = _rwkv_call(pr_s.reshape(bs * ts_rwkv, RWKV_PROJ), state_rwkv_shift[l], state_rwkv[l], lp,
                               bs, ts_rwkv, ts)
        yb_s = _att_sample_call(sample_rows(q_s, 8), sample_rows(k_s, 8), sample_rows(v_s, 8),
                                cache_att_k, cache_att_v, l, row1(att_norm_w), ts)
        conv8 = jnp.pad(state_ssm_conv[l], ((0, 0), (8 - (SSM_CONV - 1), 0), (0, 0)))
        flat8 = lambda a: sample_rows(a, ts_ssd).reshape(bs * ts_ssd, -1)
        yc_s, h_s = _ssd_call(flat8(z), flat8(xbc), flat8(dt), conv8, state_ssm[l], lp, bs, ts_ssd, ts, ts_ssd)

        def tile_rows(a, t_pad):
            return _pad_rows(a.reshape(bs, t_pad, -1)[:, :ts].reshape(ms, -1), ROW_TILE)

        mix = ((ya_p, tile_rows(ya_s, ts_rwkv)), (yb_p, tile_rows(yb_s, 8)), (yc_p, tile_rows(yc_s, ts_ssd)))
        wo = w_out[l].astype(BF16)

        last_rows = lambda a, n: jnp.stack([a[tp - n + i:mp:tp] for i in range(n)], axis=1)
        keep = min(2048, tp)
        outs["p_k"].append(k_nat.reshape(bp, tp, ATT_HEADS, HEAD_DIM)[:, tp - keep:])
        outs["p_v"].append(v_nat.reshape(bp, tp, ATT_HEADS, HEAD_DIM)[:, tp - keep:])
        outs["p_shift"].append(last_rows(p_rwkv, 1)[:, 0])
        outs["p_s"].append(s_p)
        outs["p_conv"].append(last_rows(xbc, SSM_CONV - 1))
        outs["p_h"].append(h_p)
        outs["s_k"].append(k_s[:ms].reshape(bs, ts, ATT_HEADS, HEAD_DIM))
        outs["s_v"].append(v_s[:ms].reshape(bs, ts, ATT_HEADS, HEAD_DIM))
        outs["s_shift"].append(pr_s[:, ts - 1])
        ext = jnp.concatenate([state_ssm_conv[l], xbc[mp:mp + ms].reshape(bs, ts, SSM_CONV_DIM)], axis=1)
        outs["s_s"].append(s_s)
        outs["s_conv"].append(ext[:, ts:])
        outs["s_h"].append(h_s)

        x = _ffn_call(x, mix, wo, row1(ffn2_norm), bf(ffn2_w_gate), bf(ffn2_w_up), bf(ffn2_w_down),
                      final_norm.reshape(1, -1) if l == depth - 1 else None, n_ptiles)
        if l == depth - 1:
            y_prompt, y_sample = x
        else:
            lnext = l + 1
            x = _ffn_call(x, None, None, ffn1_norm[lnext].reshape(1, -1), ffn1_w_gate[lnext].astype(BF16),
                          ffn1_w_up[lnext].astype(BF16), ffn1_w_down[lnext].astype(BF16), None, n_ptiles)

    st = lambda k: jnp.stack(outs[k])
    return (y_prompt.reshape(bp, tp, D_MODEL), y_sample[:ms].reshape(bs, ts, D_MODEL),
            st("p_k"), st("p_v"), st("p_shift"), st("p_s"), st("p_conv"), st("p_h"),
            st("s_k"), st("s_v"), st("s_shift"), st("s_s"), st("s_conv"), st("s_h"))
```

```python
import functools

import jax
import jax.numpy as jnp
from jax import lax
from jax.experimental import pallas as pl
from jax.experimental.pallas import tpu as pltpu

F32 = jnp.float32
BF16 = jnp.bfloat16

D_MODEL = 1024
FFN_DIM = 2816
HEAD_DIM = 64
RWKV_HEADS = 4
RWKV_DIM = RWKV_HEADS * HEAD_DIM
RWKV_PROJ = 1024
RWKV_GN_EPS = 64e-5
ATT_HEADS = 6
ATT_DIM = ATT_HEADS * HEAD_DIM
ATT_PAIRS = ATT_HEADS // 2
DILATIONS = (1, 4, 16)
ATT_STEPS = 128
SSM_HEADS = 6
SSM_DIM = SSM_HEADS * HEAD_DIM
SSM_GROUPS = 2
SSM_STATE = 128
SSM_CONV = 4
SSM_CONV_DIM = SSM_DIM + 2 * SSM_GROUPS * SSM_STATE
PROJ_DIM = RWKV_PROJ + 3 * ATT_DIM + SSM_DIM + SSM_CONV_DIM + SSM_HEADS
PROJ_PAD = 3584
NORM_EPS = 1e-6
NEG_INF = -1e30

LANES = 128
ROW_TILE = 512
FFN_CHUNK = 256
RWKV_CHUNK = 64
RWKV_GROUP = 4
SSD_CHUNK = 128
VMEM_LIMIT = 56 * 1024 * 1024

OFF_ATT = RWKV_PROJ
OFF_Z = OFF_ATT + 3 * ATT_DIM
OFF_XBC = OFF_Z + SSM_DIM
OFF_DT = OFF_XBC + SSM_CONV_DIM


def _params(sem):
    return pltpu.CompilerParams(dimension_semantics=sem, vmem_limit_bytes=VMEM_LIMIT)


def _dot(a, b):
    return jnp.dot(a.astype(BF16), b.astype(BF16), preferred_element_type=F32)


def _dot_nt(a, b):
    return lax.dot_general(a.astype(BF16), b.astype(BF16), (((1,), (1,)), ((), ())),
                           preferred_element_type=F32)


def _dot_tn(a, b):
    return lax.dot_general(a.astype(BF16), b.astype(BF16), (((0,), (0,)), ((), ())),
                           preferred_element_type=F32)


def _dot_hi(a, b):
    return jnp.dot(a, b, preferred_element_type=F32, precision=lax.Precision.HIGHEST)


def _rms(x, w):
    return x * lax.rsqrt(jnp.mean(x * x, axis=-1, keepdims=True) + NORM_EPS) * w


def _sigmoid(x):
    return 1.0 / (1.0 + jnp.exp(-x))


def _softplus(x):
    return jnp.maximum(x, 0.0) + jnp.log(1.0 + jnp.exp(-jnp.abs(x)))


def _div(x, n):
    assert n & (n - 1) == 0
    return lax.shift_right_logical(x, n.bit_length() - 1)


def _mod(x, n):
    assert n & (n - 1) == 0
    return x & (n - 1)


def _cumsum_rows(x):
    n = x.shape[0]
    row = lax.broadcasted_iota(jnp.int32, x.shape, 0)
    s = 1
    while s < n:
        x = x + jnp.where(row >= s, pltpu.roll(x, s, axis=0), 0.0)
        s *= 2
    return x


def _ffn_body(has_mix, final, n_prompt_tiles, *refs):
    it = iter(refs)
    x_ref = next(it)
    if has_mix:
        yap, yas, ybp, ybs, ycp, ycs, wo_ref = (next(it) for _ in range(7))
    nw_ref, wg_ref, wu_ref, wd_ref = (next(it) for _ in range(4))
    if final:
        fn_ref = next(it)
        op_ref, os_ref = next(it), next(it)
    else:
        o_ref = next(it)
    h_ref, acc_ref = next(it), next(it)
    i = pl.program_id(0)
    is_sample = i >= n_prompt_tiles

    x = x_ref[...]
    if has_mix:
        ya = jnp.where(is_sample, yas[...], yap[...])
        yb = jnp.where(is_sample, ybs[...], ybp[...])
        yc = jnp.where(is_sample, ycs[...], ycp[...])
        x = (x + _dot(ya, wo_ref[0:RWKV_DIM, :])
             + _dot(yb, wo_ref[RWKV_DIM:RWKV_DIM + ATT_DIM, :])
             + _dot(yc, wo_ref[RWKV_DIM + ATT_DIM:, :]))
    h_ref[...] = _rms(x, nw_ref[...]).astype(BF16)
    acc_ref[...] = x

    def chunk(c, carry):
        off = pl.multiple_of(c * FFN_CHUNK, FFN_CHUNK)
        h = h_ref[...]
        g = jnp.dot(h, wg_ref[:, pl.ds(off, FFN_CHUNK)], preferred_element_type=F32)
        u = jnp.dot(h, wu_ref[:, pl.ds(off, FFN_CHUNK)], preferred_element_type=F32)
        a = (g * _sigmoid(g) * u).astype(BF16)
        acc_ref[...] += 0.5 * jnp.dot(a, wd_ref[pl.ds(off, FFN_CHUNK), :], preferred_element_type=F32)
        return carry

    lax.fori_loop(0, FFN_DIM // FFN_CHUNK, chunk, 0)
    if final:
        y = _rms(acc_ref[...], fn_ref[...])

        @pl.when(jnp.logical_not(is_sample))
        def _():
            op_ref[...] = y

        @pl.when(is_sample)
        def _():
            os_ref[...] = y
    else:
        o_ref[...] = acc_ref[...]


def _ffn_call(x, mix, wo, nw, wg, wu, wd, final_norm, n_prompt_tiles):
    m_pad = x.shape[0]
    tm = ROW_TILE
    has_mix = mix is not None
    final = final_norm is not None
    last_p = n_prompt_tiles - 1
    row = lambda i: (i, 0)
    fixed = lambda i: (0, 0)
    prow = lambda i: (jnp.minimum(i, last_p), 0)
    resident = functools.partial(pl.BlockSpec, index_map=fixed, pipeline_mode=pl.Buffered(1))
    in_specs = [pl.BlockSpec((tm, D_MODEL), row)]
    args = [x]
    if has_mix:
        for yp, ys in mix:
            w = yp.shape[1]
            in_specs += [pl.BlockSpec((tm, w), prow), pl.BlockSpec((tm, w), fixed)]
            args += [yp, ys]
        in_specs.append(resident((D_MODEL, D_MODEL)))
        args.append(wo)
    in_specs += [pl.BlockSpec((1, D_MODEL), fixed), resident((D_MODEL, FFN_DIM)),
                 resident((D_MODEL, FFN_DIM)), resident((FFN_DIM, D_MODEL))]
    args += [nw, wg, wu, wd]
    if final:
        in_specs.append(pl.BlockSpec((1, D_MODEL), fixed))
        args.append(final_norm)
        out_shape = (jax.ShapeDtypeStruct((n_prompt_tiles * tm, D_MODEL), F32),
                     jax.ShapeDtypeStruct((tm, D_MODEL), F32))
        out_specs = (pl.BlockSpec((tm, D_MODEL), prow), pl.BlockSpec((tm, D_MODEL), fixed))
    else:
        out_shape = jax.ShapeDtypeStruct((m_pad, D_MODEL), F32)
        out_specs = pl.BlockSpec((tm, D_MODEL), row)
    return pl.pallas_call(
        functools.partial(_ffn_body, has_mix, final, n_prompt_tiles),
        grid=(m_pad // tm,), in_specs=in_specs, out_specs=out_specs, out_shape=out_shape,
        scratch_shapes=[pltpu.VMEM((tm, D_MODEL), BF16), pltpu.VMEM((tm, D_MODEL), F32)],
        compiler_params=_params(("arbitrary",)), name="ffn")(*args)


def _proj_body(n_prompt_tiles, x_ref, nw_ref, w_ref, pr_ref, q3_ref, k3_ref, v3_ref, kn_ref, vn_ref,
               qs_ref, ks_ref, vs_ref, z_ref, xbc_ref, dt_ref):
    i = pl.program_id(0)
    is_sample = i >= n_prompt_tiles
    h = _rms(x_ref[...], nw_ref[...]).astype(BF16)

    def seg(a, b):
        return jnp.dot(h, w_ref[:, a:b], preferred_element_type=F32)

    pr_ref[...] = seg(0, OFF_ATT)
    q = seg(OFF_ATT, OFF_ATT + ATT_DIM)
    k = seg(OFF_ATT + ATT_DIM, OFF_ATT + 2 * ATT_DIM)
    v = seg(OFF_ATT + 2 * ATT_DIM, OFF_Z)
    z_ref[...] = seg(OFF_Z, OFF_XBC)
    xbc_ref[...] = seg(OFF_XBC, OFF_DT)
    dt_ref[...] = seg(OFF_DT, PROJ_PAD)

    @pl.when(jnp.logical_not(is_sample))
    def _():
        kn_ref[0] = k.T
        vn_ref[0] = v.T
        for j in range(ATT_PAIRS):
            q3_ref[j] = q[:, j * LANES:(j + 1) * LANES]
            k3_ref[j] = k[:, j * LANES:(j + 1) * LANES]
            v3_ref[j] = v[:, j * LANES:(j + 1) * LANES]

    @pl.when(is_sample)
    def _():
        qs_ref[...] = q
        ks_ref[...] = k
        vs_ref[...] = v


def _proj_call(x, nw, w, n_prompt_tiles, t_prompt):
    m_pad = x.shape[0]
    tm = ROW_TILE
    mp = n_prompt_tiles * tm
    last_p = n_prompt_tiles - 1
    tiles_per_seq = t_prompt // tm
    row = lambda i: (i, 0)
    fixed = lambda i: (0, 0)
    prow3 = lambda i: (0, jnp.minimum(i, last_p), 0)
    pcol = lambda i: (jnp.minimum(i, last_p) // tiles_per_seq, 0, jnp.minimum(i, last_p) % tiles_per_seq)
    sds = jax.ShapeDtypeStruct
    out_shape = (sds((m_pad, RWKV_PROJ), F32),
                 sds((ATT_PAIRS, mp, LANES), F32), sds((ATT_PAIRS, mp, LANES), F32), sds((ATT_PAIRS, mp, LANES), F32),
                 sds((mp // t_prompt, ATT_DIM, t_prompt), F32), sds((mp // t_prompt, ATT_DIM, t_prompt), F32),
                 sds((tm, ATT_DIM), F32), sds((tm, ATT_DIM), F32), sds((tm, ATT_DIM), F32),
                 sds((m_pad, SSM_DIM), F32), sds((m_pad, SSM_CONV_DIM), F32), sds((m_pad, LANES), F32))
    p3 = pl.BlockSpec((ATT_PAIRS, tm, LANES), prow3)
    out_specs = (pl.BlockSpec((tm, RWKV_PROJ), row), p3, p3, p3,
                 pl.BlockSpec((1, ATT_DIM, tm), pcol), pl.BlockSpec((1, ATT_DIM, tm), pcol),
                 pl.BlockSpec((tm, ATT_DIM), fixed), pl.BlockSpec((tm, ATT_DIM), fixed),
                 pl.BlockSpec((tm, ATT_DIM), fixed),
                 pl.BlockSpec((tm, SSM_DIM), row), pl.BlockSpec((tm, SSM_CONV_DIM), row),
                 pl.BlockSpec((tm, LANES), row))
    return pl.pallas_call(
        functools.partial(_proj_body, n_prompt_tiles),
        grid=(m_pad // tm,),
        in_specs=[pl.BlockSpec((tm, D_MODEL), row), pl.BlockSpec((1, D_MODEL), fixed),
                  pl.BlockSpec((D_MODEL, PROJ_PAD), fixed, pipeline_mode=pl.Buffered(1))],
        out_specs=out_specs, out_shape=out_shape,
        compiler_params=_params(("arbitrary",)), name="proj")(x, nw, w)


def _rwkv_body(chunk, group, by_batch, t_valid, t_total, p_ref, sh_ref, s0_ref, mu_ref, w0_ref, w2_ref, a0_ref, a2_ref,
               g2_ref, kk_ref, ka_ref, rk_ref, lnw_ref, lnb_ref, y_ref, so_ref, sb_ref, prev_ref):
    C = chunk
    G = group
    H = RWKV_HEADS
    N = HEAD_DIM
    R = RWKV_DIM
    GC = G * C
    c = pl.program_id(1)
    nc = pl.num_programs(1)
    p = p_ref[...]
    row_p = lax.broadcasted_iota(jnp.int32, p.shape, 0)
    p_prev = pltpu.roll(p, 1, axis=0)

    if by_batch:
        for gi in range(G):
            p_prev = jnp.where(row_p == gi * C, sh_ref[gi], p_prev)
        zero = jnp.zeros((N, N), F32)
        states = [jnp.concatenate([jnp.concatenate([s0_ref[gi, h] if hh == h else zero for hh in range(H)], axis=1)
                                   for h in range(H)], axis=0) for gi in range(G)]
    else:
        @pl.when(c == 0)
        def _():
            sb_ref[...] = jnp.zeros((R, R), F32)
            for h in range(H):
                sb_ref[h * N:(h + 1) * N, h * N:(h + 1) * N] = s0_ref[0, h]
            prev_ref[...] = jnp.broadcast_to(sh_ref[0], (8, RWKV_PROJ))

        p_prev = jnp.where(row_p == 0, prev_ref[0:1, :], p_prev)
        prev_ref[...] = jnp.broadcast_to(p[GC - 1:GC, :], (8, RWKV_PROJ))
    ps = p + mu_ref[...] * (p_prev - p)
    r = ps[:, 0:R]
    k = ps[:, R:2 * R]
    v = ps[:, 2 * R:3 * R]
    w_lr = ps[:, 3 * R:3 * R + 64]
    a_lr = ps[:, 3 * R + 64:3 * R + 128]
    g_lr = ps[:, 3 * R + 128:]

    w_log = -_softplus(-(w0_ref[...] + _dot(jnp.tanh(w_lr), w2_ref[...]))) - 0.5
    ld = -jnp.exp(w_log)
    a = _sigmoid(a0_ref[...] + _dot(a_lr, a2_ref[...]))
    g = _dot(_sigmoid(g_lr), g2_ref[...])

    lane_r = lax.broadcasted_iota(jnp.int32, (R, R), 1)
    row_r = lax.broadcasted_iota(jnp.int32, (R, R), 0)
    own_head = _div(lane_r, N) == _div(row_r, N)
    head_ones = jnp.where(own_head, 1.0, 0.0).astype(BF16)

    def head_sum(x):
        hi = x.astype(BF16)
        lo = (x - hi.astype(F32)).astype(BF16)
        return (jnp.dot(hi, head_ones, preferred_element_type=F32)
                + jnp.dot(lo, head_ones, preferred_element_type=F32))

    kk = k * kk_ref[...]
    kk = kk / jnp.maximum(jnp.sqrt(head_sum(kk * kk)), 1e-12)
    kp = k * (1.0 + (a - 1.0) * ka_ref[...])
    if t_valid < t_total:
        row_r2 = lax.broadcasted_iota(jnp.int32, (GC, R), 0)
        live = (_mod(row_r2, C) if by_batch else c * GC + row_r2) < t_valid
        ld = jnp.where(live, ld, 0.0)
        kk = jnp.where(live, kk, 0.0)
        kp = jnp.where(live, kp, 0.0)
        v = jnp.where(live, v, 0.0)

    HC = H * C
    bd_row = _div(lax.broadcasted_iota(jnp.int32, (HC, R), 0), C)
    bd_lane = _div(lax.broadcasted_iota(jnp.int32, (HC, R), 1), N)

    def bd_rows(x):
        return jnp.where(bd_row == bd_lane, jnp.concatenate([x] * H, axis=0), 0.0)

    sq_row = _div(lax.broadcasted_iota(jnp.int32, (HC, HC), 0), C)
    sq_col = _div(lax.broadcasted_iota(jnp.int32, (HC, HC), 1), C)

    def mm(x, y):
        return _dot(x, jnp.where(sq_row == sq_col, jnp.concatenate([y] * H, axis=0), 0.0))

    t_idx = lax.broadcasted_iota(jnp.int32, (C, HC), 0)
    s_idx = _mod(lax.broadcasted_iota(jnp.int32, (C, HC), 1), C)
    strict = s_idx < t_idx
    incl = s_idx <= t_idx
    eye = jnp.where(s_idx == t_idx, 1.0, 0.0).astype(F32)
    same16 = _div(s_idx, 16) == _div(t_idx, 16)

    chunks = [slice(gi * C, (gi + 1) * C) for gi in range(G)]
    cum = [_cumsum_rows(ld[rows]) for rows in chunks]
    w_inc = [jnp.exp(x) for x in cum]
    w_inv = [jnp.exp(-x) for x in cum]
    w_end = [x[C - 1:C, :] for x in w_inc]
    b_t = [kk[rows] * a[rows] * wi for rows, wi in zip(chunks, w_inv)]
    k_t = [kp[rows] * wi for rows, wi in zip(chunks, w_inv)]
    lhs = [jnp.concatenate([kk[rows] * jnp.exp(cm - ld[rows]), r[rows] * wi], axis=0)
           for rows, cm, wi in zip(chunks, cum, w_inc)]
    sc = [_dot_nt(x, jnp.concatenate([bd_rows(bt), bd_rows(kt)], axis=0)) for x, bt, kt in zip(lhs, b_t, k_t)]
    a_ab = [jnp.where(strict, x[0:C, 0:HC], 0.0) for x in sc]
    a_ak = [jnp.where(strict, x[0:C, HC:], 0.0) for x in sc]
    p_rb = [jnp.where(incl, x[C:, 0:HC], 0.0) for x in sc]
    p_rk = [jnp.where(incl, x[C:, HC:], 0.0) for x in sc]
    e1 = [jnp.where(same16, -x, 0.0) for x in a_ab]
    n_low = [x + y for x, y in zip(a_ab, e1)]
    e2 = [mm(x, x) for x in e1]
    st = [mm(jnp.concatenate([eye + x, y], axis=0), y) for x, y in zip(e1, e2)]
    p4 = [eye + x + y[0:C] for x, y in zip(e1, st)]
    e4 = [y[C:] for y in st]
    st = [mm(jnp.concatenate([x, y], axis=0), y) for x, y in zip(p4, e4)]
    p8 = [x + y[0:C] for x, y in zip(p4, st)]
    e8 = [y[C:] for y in st]
    d_inv = [x + mm(x, y) for x, y in zip(p8, e8)]
    f1 = [-mm(x, y) for x, y in zip(d_inv, n_low)]
    f2 = [mm(x, x) for x in f1]
    q3 = [eye + x + mm(eye + x, y) for x, y in zip(f1, f2)]
    t_inv = [mm(x, y) for x, y in zip(q3, d_inv)]
    v_c = [v[rows] for rows in chunks]
    v_bd = [bd_rows(x) for x in v_c]
    avp = [_dot(jnp.concatenate([x, y], axis=0), z) for x, y, z in zip(a_ak, p_rk, v_bd)]
    tail = [jnp.concatenate([bt * we, kt * we], axis=0) for bt, kt, we in zip(b_t, k_t, w_end)]

    def chain(gi, sb):
        ks_rs = _dot_nt(lhs[gi], sb)
        u = _dot(t_inv[gi], bd_rows(-ks_rs[0:C] - avp[gi][0:C]))
        upd = _dot_tn(jnp.concatenate([u, v_c[gi]], axis=0), tail[gi])
        y_c = ks_rs[C:] + _dot(p_rb[gi], bd_rows(u)) + avp[gi][C:]
        return jnp.where(own_head, sb * w_end[gi] + upd, 0.0), y_c

    ys = []
    if by_batch:
        for gi in range(G):
            sb, y_c = chain(gi, states[gi])
            ys.append(y_c)
            for h in range(H):
                so_ref[gi, h] = sb[h * N:(h + 1) * N, h * N:(h + 1) * N]
    else:
        sb = sb_ref[...]
        for gi in range(G):
            sb, y_c = chain(gi, sb)
            ys.append(y_c)
        sb_ref[...] = sb

        @pl.when(c == nc - 1)
        def _():
            for h in range(H):
                so_ref[0, h] = sb_ref[h * N:(h + 1) * N, h * N:(h + 1) * N]
    y = jnp.concatenate(ys, axis=0) if G > 1 else ys[0]

    mean = head_sum(y) * (1.0 / N)
    yc = y - mean
    var = head_sum(yc * yc) * (1.0 / N)
    yn = yc * lax.rsqrt(var + RWKV_GN_EPS) * lnw_ref[...] + lnb_ref[...]
    bonus = head_sum(r * kp * rk_ref[...]) * v
    y_ref[...] = (yn + bonus) * g


def _rwkv_call(p, shift_prev, s0, lp, n_batch, t_total, t_valid):
    C = RWKV_CHUNK
    by_batch = t_total == C
    G = RWKV_GROUP
    assert (n_batch if by_batch else t_total // C) % G == 0
    nc = 1 if by_batch else t_total // (G * C)
    n_outer = n_batch // G if by_batch else n_batch
    gb = G if by_batch else 1
    vec = lambda n: pl.BlockSpec((1, n), lambda b, c: (0, 0))
    mat = lambda a, n: pl.BlockSpec((a, n), lambda b, c: (0, 0))
    in_specs = [pl.BlockSpec((G * C, RWKV_PROJ), lambda b, c: (b * nc + c, 0)),
                pl.BlockSpec((gb, 1, RWKV_PROJ), lambda b, c: (b, 0, 0)),
                pl.BlockSpec((gb, RWKV_HEADS, HEAD_DIM, HEAD_DIM), lambda b, c: (b, 0, 0, 0)),
                vec(RWKV_PROJ), vec(RWKV_DIM), mat(64, RWKV_DIM), vec(RWKV_DIM), mat(64, RWKV_DIM),
                mat(128, RWKV_DIM), vec(RWKV_DIM), vec(RWKV_DIM), vec(RWKV_DIM), vec(RWKV_DIM), vec(RWKV_DIM)]
    out_shape = (jax.ShapeDtypeStruct((n_batch * t_total, RWKV_DIM), F32),
                 jax.ShapeDtypeStruct((n_batch, RWKV_HEADS, HEAD_DIM, HEAD_DIM), F32))
    out_specs = (pl.BlockSpec((G * C, RWKV_DIM), lambda b, c: (b * nc + c, 0)),
                 pl.BlockSpec((gb, RWKV_HEADS, HEAD_DIM, HEAD_DIM), lambda b, c: (b, 0, 0, 0)))
    return pl.pallas_call(
        functools.partial(_rwkv_body, C, G, by_batch, t_valid, t_total),
        grid=(n_outer, nc), in_specs=in_specs, out_specs=out_specs, out_shape=out_shape,
        scratch_shapes=[pltpu.VMEM((RWKV_DIM, RWKV_DIM), F32), pltpu.VMEM((8, RWKV_PROJ), F32)],
        compiler_params=_params(("arbitrary", "arbitrary")), name="rwkv")(
            p, shift_prev.reshape(n_batch, 1, RWKV_PROJ), s0,
            lp["rwkv_mu"], lp["rwkv_w0"], lp["rwkv_w2"], lp["rwkv_a0"], lp["rwkv_a2"], lp["rwkv_g2"],
            lp["rwkv_k_k"], lp["rwkv_k_a"], lp["rwkv_r_k"], lp["rwkv_ln_w"], lp["rwkv_ln_b"])


def _alibi_slope(h):
    return 2.0 ** (-8.0 * (h + 1) / ATT_HEADS)


def _att_prompt_body(t_len, q_ref, k_ref, v_ref, nw_ref, o_ref, acc_ref, m_ref, l_ref):
    blk = ATT_STEPS
    for d in DILATIONS:
        n = t_len // d
        kw = min(2 * blk, n)
        nb = n // blk
        qi = _mod(lax.broadcasted_iota(jnp.int32, (2 * blk, kw), 0), blk)
        kj = lax.broadcasted_iota(jnp.int32, (2 * blk, kw), 1)
        top = lax.broadcasted_iota(jnp.int32, (2 * blk, kw), 0) < blk
        lane_top = lax.broadcasted_iota(jnp.int32, (blk, LANES), 1) < HEAD_DIM

        def step(it, carry, d=d, kw=kw, nb=nb, qi=qi, kj=kj, top=top, lane_top=lane_top):
            res = it // nb
            b = it % nb
            kb = jnp.maximum(b - 1, 0) if kw > blk else b
            q_rows = pl.ds(res + d * blk * b, blk, stride=d)
            k_rows = pl.ds(res + d * blk * kb, kw, stride=d)
            delta = (b - kb) * blk + qi - kj
            valid = (delta >= 0) & (delta <= blk)
            dist = (d * delta).astype(F32)
            pairs = range(ATT_PAIRS)
            q2 = [q_ref.at[j][q_rows, :] * (HEAD_DIM ** -0.5) for j in pairs]
            k2 = [k_ref.at[j][k_rows, :] for j in pairs]
            v2 = [v_ref.at[j][k_rows, :] for j in pairs]
            q_st = [jnp.concatenate([jnp.where(lane_top, x, 0.0), jnp.where(lane_top, 0.0, x)], axis=0) for x in q2]
            s = [_dot_nt(x, y) for x, y in zip(q_st, k2)]
            slope = [jnp.where(top, _alibi_slope(2 * j), _alibi_slope(2 * j + 1)) for j in pairs]
            s = [jnp.where(valid, x - sl * dist, NEG_INF) for x, sl in zip(s, slope)]
            mh = [jnp.max(x, axis=-1, keepdims=True) for x in s]
            pr = [jnp.exp(x - m) for x, m in zip(s, mh)]
            lh = [jnp.sum(x, axis=-1, keepdims=True) for x in pr]
            ah = [_dot(x, y) for x, y in zip(pr, v2)]

            def unstack(x):
                x = jnp.broadcast_to(x, (2 * blk, LANES))
                return jnp.where(lane_top, x[0:blk], x[blk:])

            results = []
            for j in pairs:
                acc, m, l = unstack(ah[j]), unstack(mh[j]), unstack(lh[j])
                if d != DILATIONS[0]:
                    m_old = m_ref.at[j][q_rows, :]
                    m_new = jnp.maximum(m_old, m)
                    e_old = jnp.exp(m_old - m_new)
                    e_new = jnp.exp(m - m_new)
                    acc = acc_ref.at[j][q_rows, :] * e_old + acc * e_new
                    l = l_ref.at[j][q_rows, :] * e_old + l * e_new
                    m = m_new
                results.append((acc, m, l))
            for j, (acc, m, l) in enumerate(results):
                acc_ref.at[j][q_rows, :] = acc
                m_ref.at[j][q_rows, :] = m
                l_ref.at[j][q_rows, :] = l
            return carry

        lax.fori_loop(0, d * nb, step, 0)

    def finish(it, carry):
        rows = pl.ds(pl.multiple_of(it * blk, blk), blk)
        ys = [acc_ref[j, rows, :] / l_ref[j, rows, :] for j in range(ATT_PAIRS)]
        ms = sum(jnp.sum(y * y, axis=-1, keepdims=True) for y in ys) * (1.0 / ATT_DIM)
        scale = lax.rsqrt(ms + NORM_EPS)
        for j in range(ATT_PAIRS):
            o_ref[rows, j * LANES:(j + 1) * LANES] = ys[j] * scale * nw_ref[:, j * LANES:(j + 1) * LANES]
        return carry

    lax.fori_loop(0, t_len // blk, finish, 0)


def _att_prompt_call(q3, k3, v3, nw, n_batch, t_len):
    blk3 = pl.BlockSpec((ATT_PAIRS, t_len, LANES), lambda b: (0, b, 0))
    return pl.pallas_call(
        functools.partial(_att_prompt_body, t_len),
        grid=(n_batch,),
        in_specs=[blk3, blk3, blk3, pl.BlockSpec((1, ATT_DIM), lambda b: (0, 0))],
        out_specs=pl.BlockSpec((t_len, ATT_DIM), lambda b: (b, 0)),
        out_shape=jax.ShapeDtypeStruct((n_batch * t_len, ATT_DIM), F32),
        scratch_shapes=[pltpu.VMEM((ATT_PAIRS, t_len, LANES), F32)] * 3,
        compiler_params=_params(("arbitrary",)), name="att_prompt")(q3, k3, v3, nw)


def _multiplicity(dist):
    mult = jnp.zeros(dist.shape, F32)
    for d in DILATIONS:
        hit = (_mod(dist, d) == 0) & (dist <= d * ATT_STEPS)
        mult = mult + jnp.where(hit, 1.0, 0.0)
    return mult


def _att_sample_body(t_new, w_buf, q_ref, kn_ref, vn_ref, kt_ref, vt_ref, nw_ref, o_ref):
    RQ = 8
    rows = ATT_HEADS * RQ
    q = q_ref[0] * (HEAD_DIM ** -0.5)
    head_of_row = _div(lax.broadcasted_iota(jnp.int32, (rows, ATT_DIM), 0), RQ)
    head_of_lane = _div(lax.broadcasted_iota(jnp.int32, (rows, ATT_DIM), 1), HEAD_DIM)
    own = head_of_row == head_of_lane
    q_bd = jnp.where(own, jnp.concatenate([q] * ATT_HEADS, axis=0), 0.0)

    s_buf = _dot(q_bd, kt_ref[0, 0])
    s_new = _dot_nt(q_bd, kn_ref[0])

    def weights(s, key_pos):
        t = _mod(lax.broadcasted_iota(jnp.int32, s.shape, 0), RQ)
        h = _div(lax.broadcasted_iota(jnp.int32, s.shape, 0), RQ)
        dist = w_buf + t - key_pos
        slope = sum(jnp.where(h == hh, _alibi_slope(hh), 0.0) for hh in range(ATT_HEADS))
        mult = jnp.where(dist >= 0, _multiplicity(jnp.maximum(dist, 0)), 0.0)
        return jnp.where(mult > 0, s - slope * dist.astype(F32), NEG_INF), mult

    sb, mult_b = weights(s_buf, lax.broadcasted_iota(jnp.int32, s_buf.shape, 1))
    sn, mult_n = weights(s_new, w_buf + lax.broadcasted_iota(jnp.int32, s_new.shape, 1))
    m = jnp.maximum(jnp.max(sb, axis=-1, keepdims=True), jnp.max(sn, axis=-1, keepdims=True))
    pb = mult_b * jnp.exp(sb - m)
    pn = mult_n * jnp.exp(sn - m)
    den = jnp.sum(pb, axis=-1, keepdims=True) + jnp.sum(pn, axis=-1, keepdims=True)
    num = _dot_nt(pb, vt_ref[0, 0]) + _dot(pn, vn_ref[0])
    out = jnp.where(own, num / den, 0.0).reshape(ATT_HEADS, RQ, ATT_DIM)
    o_ref[0] = _rms(jnp.sum(out, axis=0), nw_ref[...])


def _att_sample_call(q, k_new, v_new, cache_kt, cache_vt, layer, nw, t_new):
    depth, n_batch, _, w_buf = cache_kt.shape
    small = pl.BlockSpec((1, 8, ATT_DIM), lambda b: (b, 0, 0))
    big = pl.BlockSpec((1, 1, ATT_DIM, w_buf), lambda b: (layer, b, 0, 0))
    return pl.pallas_call(
        functools.partial(_att_sample_body, t_new, w_buf),
        grid=(n_batch,),
        in_specs=[small, small, small, big, big, pl.BlockSpec((1, ATT_DIM), lambda b: (0, 0))],
        out_specs=small, out_shape=jax.ShapeDtypeStruct((n_batch, 8, ATT_DIM), F32),
        compiler_params=_params(("arbitrary",)), name="att_sample")(q, k_new, v_new, cache_kt, cache_vt, nw)


def _ssd_body(chunk, t_valid, t_total, z_ref, xbc_ref, dt_ref, cp_ref, h0_ref, cw_ref, cb_ref, dtb_ref, alog_ref,
              dskip_ref, nw_ref, y_ref, ho_ref, hs_ref, xc_ref):
    L = chunk
    P = HEAD_DIM
    NS = SSM_STATE
    c = pl.program_id(1)
    nc = pl.num_programs(1)

    @pl.when(c == 0)
    def _():
        hs_ref[...] = h0_ref[0]
        xc_ref[0:8, :] = cp_ref[0]

    xc_ref[8:8 + L, :] = xbc_ref[...]
    conv = cb_ref[...]
    for i in range(SSM_CONV):
        conv = conv + xc_ref[5 + i:5 + i + L, :] * cw_ref[i:i + 1, :]
    xc_ref[0:8, :] = xc_ref[L:L + 8, :]
    act = conv * _sigmoid(conv)
    xs = act[:, 0:SSM_DIM]
    bm = act[:, SSM_DIM:SSM_DIM + SSM_GROUPS * NS]
    cm = act[:, SSM_DIM + SSM_GROUPS * NS:]

    dt = _softplus(dt_ref[...] + dtb_ref[...])
    if t_valid < t_total:
        live = (c * L + lax.broadcasted_iota(jnp.int32, dt.shape, 0)) < t_valid
        dt = jnp.where(live, dt, 0.0)
    a_neg = -jnp.exp(alog_ref[...])
    acum = _cumsum_rows(dt * a_neg)
    acum_t = acum.T
    dt_t = dt.T
    li = lax.broadcasted_iota(jnp.int32, (L, L), 0)
    si = lax.broadcasted_iota(jnp.int32, (L, L), 1)
    causal = li >= si
    cb = [_dot_nt(cm[:, g * NS:(g + 1) * NS], bm[:, g * NS:(g + 1) * NS]) for g in range(SSM_GROUPS)]
    rep = SSM_HEADS // SSM_GROUPS
    GW = rep * P
    bd_row = _div(lax.broadcasted_iota(jnp.int32, (rep * L, GW), 0), L)
    bd_lane = _div(lax.broadcasted_iota(jnp.int32, (rep * L, GW), 1), P)

    def lanes_of(cols):
        return jnp.concatenate([jnp.broadcast_to(x, (x.shape[0], P)) for x in cols], axis=1)

    ys = []
    for g in range(SSM_GROUPS):
        heads = range(g * rep, (g + 1) * rep)
        x_g = xs[:, g * GW:(g + 1) * GW]
        b_g = bm[:, g * NS:(g + 1) * NS]
        c_g = cm[:, g * NS:(g + 1) * NS]
        scores = [cb[g] * jnp.exp(jnp.where(causal, acum[:, h:h + 1] - acum_t[h:h + 1, :], NEG_INF)) * dt_t[h:h + 1, :]
                  for h in heads]
        ac = lanes_of([acum[:, h:h + 1] for h in heads])
        a_last = lanes_of([acum[L - 1:L, h:h + 1] for h in heads])
        x_bd = jnp.where(bd_row == bd_lane, jnp.concatenate([x_g] * rep, axis=0), 0.0)
        h_g = hs_ref[g * rep:(g + 1) * rep].reshape(GW, NS)
        y = (_dot(jnp.concatenate(scores, axis=1), x_bd) + _dot_nt(c_g, h_g) * jnp.exp(ac)
             + lanes_of([dskip_ref[:, h:h + 1] for h in heads]) * x_g)
        to_end = jnp.exp(a_last - ac) * lanes_of([dt[:, h:h + 1] for h in heads])
        carry = jnp.concatenate([jnp.broadcast_to(jnp.exp(acum[L - 1:L, h:h + 1]), (P, NS)) for h in heads], axis=0)
        hs_ref[g * rep:(g + 1) * rep] = (h_g * carry + _dot_tn(x_g * to_end, b_g)).reshape(rep, P, NS)
        ys.append(y)
    y = jnp.concatenate(ys, axis=1)
    z = z_ref[...]
    y = y * (z * _sigmoid(z))
    y_ref[...] = _rms(y, nw_ref[...])

    @pl.when(c == nc - 1)
    def _():
        ho_ref[0] = hs_ref[...]


def _ssd_call(z, xbc, dt, conv_prev8, h0, lp, n_batch, t_total, t_valid, chunk):
    L = chunk
    nc = t_total // L
    rows = lambda w: pl.BlockSpec((L, w), lambda b, c: (b * nc + c, 0))
    vec = lambda a, n: pl.BlockSpec((a, n), lambda b, c: (0, 0))
    in_specs = [rows(SSM_DIM), rows(SSM_CONV_DIM), rows(LANES),
                pl.BlockSpec((1, 8, SSM_CONV_DIM), lambda b, c: (b, 0, 0)),
                pl.BlockSpec((1, SSM_HEADS, HEAD_DIM, SSM_STATE), lambda b, c: (b, 0, 0, 0)),
                vec(SSM_CONV, SSM_CONV_DIM), vec(1, SSM_CONV_DIM), vec(1, LANES), vec(1, LANES), vec(1, LANES),
                vec(1, SSM_DIM)]
    out_shape = (jax.ShapeDtypeStruct((n_batch * t_total, SSM_DIM), F32),
                 jax.ShapeDtypeStruct((n_batch, SSM_HEADS, HEAD_DIM, SSM_STATE), F32))
    out_specs = (rows(SSM_DIM), pl.BlockSpec((1, SSM_HEADS, HEAD_DIM, SSM_STATE), lambda b, c: (b, 0, 0, 0)))
    return pl.pallas_call(
        functools.partial(_ssd_body, L, t_valid, t_total),
        grid=(n_batch, nc), in_specs=in_specs, out_specs=out_specs, out_shape=out_shape,
        scratch_shapes=[pltpu.VMEM((SSM_HEADS, HEAD_DIM, SSM_STATE), F32), pltpu.VMEM((L + 8, SSM_CONV_DIM), F32)],
        compiler_params=_params(("arbitrary", "arbitrary")), name="ssd")(
            z, xbc, dt, conv_prev8, h0, lp["ssm_conv_w"], lp["ssm_conv_b"], lp["ssm_dt_bias"], lp["ssm_A_log"],
            lp["ssm_D"], lp["ssm_norm_w"])


def _pad_lanes(v, n):
    return jnp.pad(v, ((0, 0), (0, n - v.shape[-1])))


def _pad_rows(x, n):
    return jnp.pad(x, ((0, n - x.shape[0]),) + ((0, 0),) * (x.ndim - 1))


def kernel(x_prompt, x_sample, cache_att_k, cache_att_v, state_rwkv_shift, state_rwkv, state_ssm_conv, state_ssm, ffn1_norm, ffn1_w_gate, ffn1_w_up, ffn1_w_down, mix_norm, w_in, rwkv_mu, rwkv_w0, rwkv_w2, rwkv_a0, rwkv_a2, rwkv_g2, rwkv_k_k, rwkv_k_a, rwkv_r_k, rwkv_ln_w, rwkv_ln_b, att_norm_w, ssm_conv_w, ssm_conv_b, ssm_dt_bias, ssm_A_log, ssm_D, ssm_norm_w, w_out, ffn2_norm, ffn2_w_gate, ffn2_w_up, ffn2_w_down, final_norm):
    depth = w_in.shape[0]
    bp, tp, _ = x_prompt.shape
    bs, ts, _ = x_sample.shape
    w_buf = cache_att_k.shape[2]
    mp = bp * tp
    ms = bs * ts
    assert mp % ROW_TILE == 0 and ms <= ROW_TILE and tp % (ATT_STEPS * DILATIONS[-1]) == 0 and ts <= 8
    n_ptiles = mp // ROW_TILE
    ts_rwkv = RWKV_CHUNK
    ts_ssd = 8

    x = jnp.concatenate([x_prompt.reshape(mp, D_MODEL), _pad_rows(x_sample.reshape(ms, D_MODEL), ROW_TILE)], axis=0)
    zeros_shift = jnp.zeros((bp, RWKV_PROJ), F32)
    zeros_s = jnp.zeros((bp, RWKV_HEADS, HEAD_DIM, HEAD_DIM), F32)
    zeros_conv = jnp.zeros((bp, 8, SSM_CONV_DIM), F32)
    zeros_h = jnp.zeros((bp, SSM_HEADS, HEAD_DIM, SSM_STATE), F32)
    to_cp = lambda c: jnp.transpose(c, (0, 1, 3, 4, 2)).reshape(depth, bs, ATT_DIM, w_buf)
    cache_kt, cache_vt = to_cp(cache_att_k), to_cp(cache_att_v)

    outs = {k: [] for k in ("p_k", "p_v", "p_shift", "p_s", "p_conv", "p_h",
                            "s_k", "s_v", "s_shift", "s_s", "s_conv", "s_h")}
    mix = None
    wo = None
    y_prompt = y_sample = None
    for l in range(depth):
        row1 = lambda v: v[l].reshape(1, -1)
        lp = dict(rwkv_mu=row1(rwkv_mu), rwkv_w0=row1(rwkv_w0), rwkv_w2=rwkv_w2[l], rwkv_a0=row1(rwkv_a0),
                  rwkv_a2=rwkv_a2[l], rwkv_g2=rwkv_g2[l], rwkv_k_k=row1(rwkv_k_k), rwkv_k_a=row1(rwkv_k_a),
                  rwkv_r_k=row1(rwkv_r_k), rwkv_ln_w=row1(rwkv_ln_w), rwkv_ln_b=row1(rwkv_ln_b),
                  ssm_conv_w=ssm_conv_w[l], ssm_conv_b=row1(ssm_conv_b),
                  ssm_dt_bias=_pad_lanes(row1(ssm_dt_bias), LANES), ssm_A_log=_pad_lanes(row1(ssm_A_log), LANES),
                  ssm_D=_pad_lanes(row1(ssm_D), LANES), ssm_norm_w=row1(ssm_norm_w))
        bf = lambda w: w[l].astype(BF16)
        if l == 0:
            x = _ffn_call(x, None, None, row1(ffn1_norm), bf(ffn1_w_gate), bf(ffn1_w_up), bf(ffn1_w_down), None,
                          n_ptiles)
        (p_rwkv, q3, k3, v3, k_cp, v_cp, q_s, k_s, v_s, z, xbc, dt) = _proj_call(
            x, row1(mix_norm), _pad_lanes(w_in[l], PROJ_PAD).astype(BF16), n_ptiles, tp)

        ya_p, s_p = _rwkv_call(p_rwkv, zeros_shift, zeros_s, lp, bp, tp, tp)
        yb_p = _att_prompt_call(q3, k3, v3, row1(att_norm_w), bp, tp)
        yc_p, h_p = _ssd_call(z, xbc, dt, zeros_conv, zeros_h, lp, bp, tp, tp, SSD_CHUNK)

        def sample_rows(a, t_pad):
            first = a.shape[0] - ROW_TILE
            a = a[first:first + ms].reshape(bs, ts, -1)
            return jnp.pad(a, ((0, 0), (0, t_pad - ts), (0, 0)))

        pr_s = sample_rows(p_rwkv, ts_rwkv)
        ya_s, s_s = _rwkv_call(pr_s.reshape(bs * ts_rwkv, RWKV_PROJ), state_rwkv_shift[l], state_rwkv[l], lp,
                               bs, ts_rwkv, ts)
        yb_s = _att_sample_call(sample_rows(q_s, 8), sample_rows(k_s, 8), sample_rows(v_s, 8),
                                cache_kt, cache_vt, l, row1(att_norm_w), ts)
        conv8 = jnp.pad(state_ssm_conv[l], ((0, 0), (8 - (SSM_CONV - 1), 0), (0, 0)))
        flat8 = lambda a: sample_rows(a, ts_ssd).reshape(bs * ts_ssd, -1)
        yc_s, h_s = _ssd_call(flat8(z), flat8(xbc), flat8(dt), conv8, state_ssm[l], lp, bs, ts_ssd, ts, ts_ssd)

        def tile_rows(a, t_pad):
            return _pad_rows(a.reshape(bs, t_pad, -1)[:, :ts].reshape(ms, -1), ROW_TILE)

        mix = ((ya_p, tile_rows(ya_s, ts_rwkv)), (yb_p, tile_rows(yb_s, 8)), (yc_p, tile_rows(yc_s, ts_ssd)))
        wo = w_out[l].astype(BF16)

        last_rows = lambda a, n: jnp.stack([a[tp - n + i:mp:tp] for i in range(n)], axis=1)
        keep = min(2048, tp)
        from_cp = lambda a: jnp.transpose(a.reshape(bp, ATT_HEADS, HEAD_DIM, tp), (0, 3, 1, 2))[:, tp - keep:]
        outs["p_k"].append(from_cp(k_cp))
        outs["p_v"].append(from_cp(v_cp))
        outs["p_shift"].append(last_rows(p_rwkv, 1)[:, 0])
        outs["p_s"].append(s_p)
        outs["p_conv"].append(last_rows(xbc, SSM_CONV - 1))
        outs["p_h"].append(h_p)
        outs["s_k"].append(k_s[:ms].reshape(bs, ts, ATT_HEADS, HEAD_DIM))
        outs["s_v"].append(v_s[:ms].reshape(bs, ts, ATT_HEADS, HEAD_DIM))
        outs["s_shift"].append(pr_s[:, ts - 1])
        ext = jnp.concatenate([state_ssm_conv[l], xbc[mp:mp + ms].reshape(bs, ts, SSM_CONV_DIM)], axis=1)
        outs["s_s"].append(s_s)
        outs["s_conv"].append(ext[:, ts:])
        outs["s_h"].append(h_s)

        x = _ffn_call(x, mix, wo, row1(ffn2_norm), bf(ffn2_w_gate), bf(ffn2_w_up), bf(ffn2_w_down),
                      final_norm.reshape(1, -1) if l == depth - 1 else None, n_ptiles)
        if l == depth - 1:
            y_prompt, y_sample = x
        else:
            lnext = l + 1
            x = _ffn_call(x, None, None, ffn1_norm[lnext].reshape(1, -1), ffn1_w_gate[lnext].astype(BF16),
                          ffn1_w_up[lnext].astype(BF16), ffn1_w_down[lnext].astype(BF16), None, n_ptiles)

    st = lambda k: jnp.stack(outs[k])
    return (y_prompt.reshape(bp, tp, D_MODEL), y_sample[:ms].reshape(bs, ts, D_MODEL),
            st("p_k"), st("p_v"), st("p_shift"), st("p_s"), st("p_conv"), st("p_h"),
            st("s_k"), st("s_v"), st("s_shift"), st("s_s"), st("s_conv"), st("s_h"))
```

```python
import functools

import jax
import jax.numpy as jnp
from jax import lax
from jax.experimental import pallas as pl
from jax.experimental.pallas import tpu as pltpu

F32 = jnp.float32
BF16 = jnp.bfloat16

D_MODEL = 1024
FFN_DIM = 2816
HEAD_DIM = 64
RWKV_HEADS = 4
RWKV_DIM = RWKV_HEADS * HEAD_DIM
RWKV_PROJ = 1024
RWKV_GN_EPS = 64e-5
ATT_HEADS = 6
ATT_DIM = ATT_HEADS * HEAD_DIM
ATT_PAIRS = ATT_HEADS // 2
DILATIONS = (1, 4, 16)
ATT_STEPS = 128
SSM_HEADS = 6
SSM_DIM = SSM_HEADS * HEAD_DIM
SSM_GROUPS = 2
SSM_STATE = 128
SSM_CONV = 4
SSM_CONV_DIM = SSM_DIM + 2 * SSM_GROUPS * SSM_STATE
PROJ_DIM = RWKV_PROJ + 3 * ATT_DIM + SSM_DIM + SSM_CONV_DIM + SSM_HEADS
PROJ_PAD = 3584
NORM_EPS = 1e-6
NEG_INF = -1e30
LOG2E = 1.4426950408889634

LANES = 128
ROW_TILE = 512
FFN_CHUNK = 256
RWKV_CHUNK = 64
RWKV_GROUP = 4
SSD_CHUNK = 128
VMEM_LIMIT = 56 * 1024 * 1024

OFF_ATT = RWKV_PROJ
OFF_Z = OFF_ATT + 3 * ATT_DIM
OFF_XBC = OFF_Z + SSM_DIM
OFF_DT = OFF_XBC + SSM_CONV_DIM


def _params(sem):
    return pltpu.CompilerParams(dimension_semantics=sem, vmem_limit_bytes=VMEM_LIMIT)


def _dot(a, b):
    return jnp.dot(a.astype(BF16), b.astype(BF16), preferred_element_type=F32)


def _dot_nt(a, b):
    return lax.dot_general(a.astype(BF16), b.astype(BF16), (((1,), (1,)), ((), ())),
                           preferred_element_type=F32)


def _dot_tn(a, b):
    return lax.dot_general(a.astype(BF16), b.astype(BF16), (((0,), (0,)), ((), ())),
                           preferred_element_type=F32)


def _dot_hi(a, b):
    return jnp.dot(a, b, preferred_element_type=F32, precision=lax.Precision.HIGHEST)


def _rms(x, w):
    return x * lax.rsqrt(jnp.mean(x * x, axis=-1, keepdims=True) + NORM_EPS) * w


def _sigmoid(x):
    return 1.0 / (1.0 + jnp.exp(-x))


def _softplus(x):
    return jnp.maximum(x, 0.0) + jnp.log(1.0 + jnp.exp(-jnp.abs(x)))


def _div(x, n):
    assert n & (n - 1) == 0
    return lax.shift_right_logical(x, n.bit_length() - 1)


def _mod(x, n):
    assert n & (n - 1) == 0
    return x & (n - 1)


def _cumsum_rows(x):
    n = x.shape[0]
    row = lax.broadcasted_iota(jnp.int32, x.shape, 0)
    s = 1
    while s < n:
        x = x + jnp.where(row >= s, pltpu.roll(x, s, axis=0), 0.0)
        s *= 2
    return x


def _ffn_body(has_mix, final, n_prompt_tiles, *refs):
    it = iter(refs)
    x_ref = next(it)
    if has_mix:
        yap, yas, ybp, ybs, ycp, ycs, wo_ref = (next(it) for _ in range(7))
    nw_ref, wg_ref, wu_ref, wd_ref = (next(it) for _ in range(4))
    if final:
        fn_ref = next(it)
        op_ref, os_ref = next(it), next(it)
    else:
        o_ref = next(it)
    h_ref, acc_ref = next(it), next(it)
    i = pl.program_id(0)
    is_sample = i >= n_prompt_tiles

    x = x_ref[...]
    if has_mix:
        ya = jnp.where(is_sample, yas[...], yap[...])
        yb = jnp.where(is_sample, ybs[...], ybp[...])
        yc = jnp.where(is_sample, ycs[...], ycp[...])
        x = (x + _dot(ya, wo_ref[0:RWKV_DIM, :])
             + _dot(yb, wo_ref[RWKV_DIM:RWKV_DIM + ATT_DIM, :])
             + _dot(yc, wo_ref[RWKV_DIM + ATT_DIM:, :]))
    h_ref[...] = _rms(x, nw_ref[...]).astype(BF16)
    acc_ref[...] = x

    def chunk(c, carry):
        off = pl.multiple_of(c * FFN_CHUNK, FFN_CHUNK)
        h = h_ref[...]
        g = jnp.dot(h, wg_ref[:, pl.ds(off, FFN_CHUNK)], preferred_element_type=F32)
        u = jnp.dot(h, wu_ref[:, pl.ds(off, FFN_CHUNK)], preferred_element_type=F32)
        a = (g * _sigmoid(g) * u).astype(BF16)
        acc_ref[...] += 0.5 * jnp.dot(a, wd_ref[pl.ds(off, FFN_CHUNK), :], preferred_element_type=F32)
        return carry

    lax.fori_loop(0, FFN_DIM // FFN_CHUNK, chunk, 0)
    if final:
        y = _rms(acc_ref[...], fn_ref[...])

        @pl.when(jnp.logical_not(is_sample))
        def _():
            op_ref[...] = y

        @pl.when(is_sample)
        def _():
            os_ref[...] = y
    else:
        o_ref[...] = acc_ref[...]


def _ffn_call(x, mix, wo, nw, wg, wu, wd, final_norm, n_prompt_tiles):
    m_pad = x.shape[0]
    tm = ROW_TILE
    has_mix = mix is not None
    final = final_norm is not None
    last_p = n_prompt_tiles - 1
    row = lambda i: (i, 0)
    fixed = lambda i: (0, 0)
    prow = lambda i: (jnp.minimum(i, last_p), 0)
    resident = functools.partial(pl.BlockSpec, index_map=fixed, pipeline_mode=pl.Buffered(1))
    in_specs = [pl.BlockSpec((tm, D_MODEL), row)]
    args = [x]
    if has_mix:
        for yp, ys in mix:
            w = yp.shape[1]
            in_specs += [pl.BlockSpec((tm, w), prow), pl.BlockSpec((tm, w), fixed)]
            args += [yp, ys]
        in_specs.append(resident((D_MODEL, D_MODEL)))
        args.append(wo)
    in_specs += [pl.BlockSpec((1, D_MODEL), fixed), resident((D_MODEL, FFN_DIM)),
                 resident((D_MODEL, FFN_DIM)), resident((FFN_DIM, D_MODEL))]
    args += [nw, wg, wu, wd]
    if final:
        in_specs.append(pl.BlockSpec((1, D_MODEL), fixed))
        args.append(final_norm)
        out_shape = (jax.ShapeDtypeStruct((n_prompt_tiles * tm, D_MODEL), F32),
                     jax.ShapeDtypeStruct((tm, D_MODEL), F32))
        out_specs = (pl.BlockSpec((tm, D_MODEL), prow), pl.BlockSpec((tm, D_MODEL), fixed))
    else:
        out_shape = jax.ShapeDtypeStruct((m_pad, D_MODEL), F32)
        out_specs = pl.BlockSpec((tm, D_MODEL), row)
    return pl.pallas_call(
        functools.partial(_ffn_body, has_mix, final, n_prompt_tiles),
        grid=(m_pad // tm,), in_specs=in_specs, out_specs=out_specs, out_shape=out_shape,
        scratch_shapes=[pltpu.VMEM((tm, D_MODEL), BF16), pltpu.VMEM((tm, D_MODEL), F32)],
        compiler_params=_params(("arbitrary",)), name="ffn")(*args)


def _proj_body(n_prompt_tiles, x_ref, nw_ref, w_ref, pr_ref, q3_ref, k3_ref, v3_ref, kn_ref, vn_ref, z_ref, xbc_ref,
               dt_ref, prs_ref, qs_ref, ks_ref, vs_ref, zs_ref, xbcs_ref, dts_ref):
    i = pl.program_id(0)
    is_sample = i >= n_prompt_tiles
    h = _rms(x_ref[...], nw_ref[...]).astype(BF16)

    def seg(a, b):
        return jnp.dot(h, w_ref[:, a:b], preferred_element_type=F32)

    pr = seg(0, OFF_ATT)
    q = seg(OFF_ATT, OFF_ATT + ATT_DIM)
    k = seg(OFF_ATT + ATT_DIM, OFF_ATT + 2 * ATT_DIM)
    v = seg(OFF_ATT + 2 * ATT_DIM, OFF_Z)
    z = seg(OFF_Z, OFF_XBC)
    xbc = seg(OFF_XBC, OFF_DT)
    dt = seg(OFF_DT, PROJ_PAD)

    @pl.when(jnp.logical_not(is_sample))
    def _():
        pr_ref[...] = pr
        z_ref[...] = z
        xbc_ref[...] = xbc
        dt_ref[...] = dt
        kn_ref[0] = k.T
        vn_ref[0] = v.T
        for j in range(ATT_PAIRS):
            q3_ref[j] = q[:, j * LANES:(j + 1) * LANES]
            k3_ref[j] = k[:, j * LANES:(j + 1) * LANES]
            v3_ref[j] = v[:, j * LANES:(j + 1) * LANES]

    @pl.when(is_sample)
    def _():
        prs_ref[...] = pr
        qs_ref[...] = q
        ks_ref[...] = k
        vs_ref[...] = v
        zs_ref[...] = z
        xbcs_ref[...] = xbc
        dts_ref[...] = dt


def _proj_call(x, nw, w, n_prompt_tiles, t_prompt):
    m_pad = x.shape[0]
    tm = ROW_TILE
    mp = n_prompt_tiles * tm
    last_p = n_prompt_tiles - 1
    tiles_per_seq = t_prompt // tm
    row = lambda i: (i, 0)
    fixed = lambda i: (0, 0)
    prow = lambda i: (jnp.minimum(i, last_p), 0)
    prow3 = lambda i: (0, jnp.minimum(i, last_p), 0)
    pcol = lambda i: (jnp.minimum(i, last_p) // tiles_per_seq, 0, jnp.minimum(i, last_p) % tiles_per_seq)
    sds = jax.ShapeDtypeStruct
    widths = (RWKV_PROJ, ATT_DIM, ATT_DIM, ATT_DIM, SSM_DIM, SSM_CONV_DIM, LANES)
    out_shape = ((sds((mp, RWKV_PROJ), F32),
                  sds((ATT_PAIRS, mp, LANES), F32), sds((ATT_PAIRS, mp, LANES), F32), sds((ATT_PAIRS, mp, LANES), F32),
                  sds((mp // t_prompt, ATT_DIM, t_prompt), F32), sds((mp // t_prompt, ATT_DIM, t_prompt), F32),
                  sds((mp, SSM_DIM), F32), sds((mp, SSM_CONV_DIM), F32), sds((mp, LANES), F32))
                 + tuple(sds((tm, w_), F32) for w_ in widths))
    p3 = pl.BlockSpec((ATT_PAIRS, tm, LANES), prow3)
    out_specs = ((pl.BlockSpec((tm, RWKV_PROJ), prow), p3, p3, p3,
                  pl.BlockSpec((1, ATT_DIM, tm), pcol), pl.BlockSpec((1, ATT_DIM, tm), pcol),
                  pl.BlockSpec((tm, SSM_DIM), prow), pl.BlockSpec((tm, SSM_CONV_DIM), prow),
                  pl.BlockSpec((tm, LANES), prow))
                 + tuple(pl.BlockSpec((tm, w_), fixed) for w_ in widths))
    return pl.pallas_call(
        functools.partial(_proj_body, n_prompt_tiles),
        grid=(m_pad // tm,),
        in_specs=[pl.BlockSpec((tm, D_MODEL), row), pl.BlockSpec((1, D_MODEL), fixed),
                  pl.BlockSpec((D_MODEL, PROJ_PAD), fixed, pipeline_mode=pl.Buffered(1))],
        out_specs=out_specs, out_shape=out_shape,
        compiler_params=_params(("arbitrary",)), name="proj")(x, nw, w)


def _rwkv_body(chunk, group, by_batch, t_valid, t_total, p_ref, sh_ref, s0_ref, mu_ref, w0_ref, w2_ref, a0_ref, a2_ref,
               g2_ref, kk_ref, ka_ref, rk_ref, lnw_ref, lnb_ref, y_ref, so_ref, last_ref, sb_ref, prev_ref):
    C = chunk
    G = group
    H = RWKV_HEADS
    N = HEAD_DIM
    R = RWKV_DIM
    GC = G * C
    c = pl.program_id(1)
    nc = pl.num_programs(1)
    p = p_ref[...]
    row_p = lax.broadcasted_iota(jnp.int32, p.shape, 0)
    p_prev = pltpu.roll(p, 1, axis=0)

    if by_batch:
        for gi in range(G):
            p_prev = jnp.where(row_p == gi * C, sh_ref[gi], p_prev)
        zero = jnp.zeros((N, N), F32)
        states = [jnp.concatenate([jnp.concatenate([s0_ref[gi, h] if hh == h else zero for hh in range(H)], axis=1)
                                   for h in range(H)], axis=0) for gi in range(G)]
    else:
        @pl.when(c == 0)
        def _():
            sb_ref[...] = jnp.zeros((R, R), F32)
            for h in range(H):
                sb_ref[h * N:(h + 1) * N, h * N:(h + 1) * N] = s0_ref[0, h]
            prev_ref[...] = jnp.broadcast_to(sh_ref[0], (8, RWKV_PROJ))

        p_prev = jnp.where(row_p == 0, prev_ref[0:1, :], p_prev)
        prev_ref[...] = jnp.broadcast_to(p[GC - 1:GC, :], (8, RWKV_PROJ))
    ps = p + mu_ref[...] * (p_prev - p)
    r = ps[:, 0:R]
    k = ps[:, R:2 * R]
    v = ps[:, 2 * R:3 * R]
    w_lr = ps[:, 3 * R:3 * R + 64]
    a_lr = ps[:, 3 * R + 64:3 * R + 128]
    g_lr = ps[:, 3 * R + 128:]

    w_log = -_softplus(-(w0_ref[...] + _dot(jnp.tanh(w_lr), w2_ref[...]))) - 0.5
    ld = -jnp.exp(w_log)
    a = _sigmoid(a0_ref[...] + _dot(a_lr, a2_ref[...]))
    g = _dot(_sigmoid(g_lr), g2_ref[...])

    lane_r = lax.broadcasted_iota(jnp.int32, (R, R), 1)
    row_r = lax.broadcasted_iota(jnp.int32, (R, R), 0)
    own_head = _div(lane_r, N) == _div(row_r, N)
    head_ones = jnp.where(own_head, 1.0, 0.0).astype(BF16)

    def head_sum(x):
        hi = x.astype(BF16)
        lo = (x - hi.astype(F32)).astype(BF16)
        return (jnp.dot(hi, head_ones, preferred_element_type=F32)
                + jnp.dot(lo, head_ones, preferred_element_type=F32))

    kk = k * kk_ref[...]
    kk = kk / jnp.maximum(jnp.sqrt(head_sum(kk * kk)), 1e-12)
    kp = k * (1.0 + (a - 1.0) * ka_ref[...])
    if t_valid < t_total:
        row_r2 = lax.broadcasted_iota(jnp.int32, (GC, R), 0)
        live = (_mod(row_r2, C) if by_batch else c * GC + row_r2) < t_valid
        ld = jnp.where(live, ld, 0.0)
        kk = jnp.where(live, kk, 0.0)
        kp = jnp.where(live, kp, 0.0)
        v = jnp.where(live, v, 0.0)

    HC = H * C
    bd_row = _div(lax.broadcasted_iota(jnp.int32, (HC, R), 0), C)
    bd_lane = _div(lax.broadcasted_iota(jnp.int32, (HC, R), 1), N)

    def bd_rows(x):
        return jnp.where(bd_row == bd_lane, jnp.concatenate([x] * H, axis=0), 0.0)

    sq_row = _div(lax.broadcasted_iota(jnp.int32, (HC, HC), 0), C)
    sq_col = _div(lax.broadcasted_iota(jnp.int32, (HC, HC), 1), C)

    def mm(x, y):
        return _dot(x, jnp.where(sq_row == sq_col, jnp.concatenate([y] * H, axis=0), 0.0))

    t_idx = lax.broadcasted_iota(jnp.int32, (C, HC), 0)
    s_idx = _mod(lax.broadcasted_iota(jnp.int32, (C, HC), 1), C)
    strict = s_idx < t_idx
    incl = s_idx <= t_idx
    eye = jnp.where(s_idx == t_idx, 1.0, 0.0).astype(F32)
    same16 = _div(s_idx, 16) == _div(t_idx, 16)

    chunks = [slice(gi * C, (gi + 1) * C) for gi in range(G)]
    cum = [_cumsum_rows(ld[rows]) for rows in chunks]
    w_inc = [jnp.exp(x) for x in cum]
    w_inv = [jnp.exp(-x) for x in cum]
    w_end = [x[C - 1:C, :] for x in w_inc]
    b_t = [kk[rows] * a[rows] * wi for rows, wi in zip(chunks, w_inv)]
    k_t = [kp[rows] * wi for rows, wi in zip(chunks, w_inv)]
    lhs = [jnp.concatenate([kk[rows] * jnp.exp(cm - ld[rows]), r[rows] * wi], axis=0)
           for rows, cm, wi in zip(chunks, cum, w_inc)]
    sc = [_dot_nt(x, jnp.concatenate([bd_rows(bt), bd_rows(kt)], axis=0)) for x, bt, kt in zip(lhs, b_t, k_t)]
    a_ab = [jnp.where(strict, x[0:C, 0:HC], 0.0) for x in sc]
    a_ak = [jnp.where(strict, x[0:C, HC:], 0.0) for x in sc]
    p_rb = [jnp.where(incl, x[C:, 0:HC], 0.0) for x in sc]
    p_rk = [jnp.where(incl, x[C:, HC:], 0.0) for x in sc]
    e1 = [jnp.where(same16, -x, 0.0) for x in a_ab]
    n_low = [x + y for x, y in zip(a_ab, e1)]
    e2 = [mm(x, x) for x in e1]
    st = [mm(jnp.concatenate([eye + x, y], axis=0), y) for x, y in zip(e1, e2)]
    p4 = [eye + x + y[0:C] for x, y in zip(e1, st)]
    e4 = [y[C:] for y in st]
    st = [mm(jnp.concatenate([x, y], axis=0), y) for x, y in zip(p4, e4)]
    p8 = [x + y[0:C] for x, y in zip(p4, st)]
    e8 = [y[C:] for y in st]
    d_inv = [x + mm(x, y) for x, y in zip(p8, e8)]
    f1 = [-mm(x, y) for x, y in zip(d_inv, n_low)]
    f2 = [mm(x, x) for x in f1]
    q3 = [eye + x + mm(eye + x, y) for x, y in zip(f1, f2)]
    t_inv = [mm(x, y) for x, y in zip(q3, d_inv)]
    v_c = [v[rows] for rows in chunks]
    v_bd = [bd_rows(x) for x in v_c]
    avp = [_dot(jnp.concatenate([x, y], axis=0), z) for x, y, z in zip(a_ak, p_rk, v_bd)]
    tail = [jnp.concatenate([bt * we, kt * we], axis=0) for bt, kt, we in zip(b_t, k_t, w_end)]

    def chain(gi, sb):
        ks_rs = _dot_nt(lhs[gi], sb)
        u = _dot(t_inv[gi], bd_rows(-ks_rs[0:C] - avp[gi][0:C]))
        upd = _dot_tn(jnp.concatenate([u, v_c[gi]], axis=0), tail[gi])
        y_c = ks_rs[C:] + _dot(p_rb[gi], bd_rows(u)) + avp[gi][C:]
        return jnp.where(own_head, sb * w_end[gi] + upd, 0.0), y_c

    ys = []
    if by_batch:
        for gi in range(G):
            sb, y_c = chain(gi, states[gi])
            ys.append(y_c)
            for h in range(H):
                so_ref[gi, h] = sb[h * N:(h + 1) * N, h * N:(h + 1) * N]
    else:
        sb = sb_ref[...]
        for gi in range(G):
            sb, y_c = chain(gi, sb)
            ys.append(y_c)
        sb_ref[...] = sb

        @pl.when(c == nc - 1)
        def _():
            for h in range(H):
                so_ref[0, h] = sb_ref[h * N:(h + 1) * N, h * N:(h + 1) * N]
    y = jnp.concatenate(ys, axis=0) if G > 1 else ys[0]

    yc = y - head_sum(y) * (1.0 / N)
    var = head_sum(yc * yc) * (1.0 / N)
    yn = yc * lax.rsqrt(var + RWKV_GN_EPS) * lnw_ref[...] + lnb_ref[...]
    bonus = head_sum(r * kp * rk_ref[...]) * v
    y_ref[...] = (yn + bonus) * g
    if by_batch:
        for gi in range(G):
            last_ref[gi] = jnp.broadcast_to(p[gi * C + t_valid - 1:gi * C + t_valid, :], (8, RWKV_PROJ))
    else:
        @pl.when(c == nc - 1)
        def _():
            last_ref[0] = prev_ref[...]


def _rwkv_call(p, shift_prev, s0, lp, n_batch, t_total, t_valid):
    C = RWKV_CHUNK
    by_batch = t_total == C
    G = RWKV_GROUP
    assert (n_batch if by_batch else t_total // C) % G == 0
    nc = 1 if by_batch else t_total // (G * C)
    n_outer = n_batch // G if by_batch else n_batch
    gb = G if by_batch else 1
    vec = lambda n: pl.BlockSpec((1, n), lambda b, c: (0, 0))
    mat = lambda a, n: pl.BlockSpec((a, n), lambda b, c: (0, 0))
    in_specs = [pl.BlockSpec((G * C, RWKV_PROJ), lambda b, c: (b * nc + c, 0)),
                pl.BlockSpec((gb, 1, RWKV_PROJ), lambda b, c: (b, 0, 0)),
                pl.BlockSpec((gb, RWKV_HEADS, HEAD_DIM, HEAD_DIM), lambda b, c: (b, 0, 0, 0)),
                vec(RWKV_PROJ), vec(RWKV_DIM), mat(64, RWKV_DIM), vec(RWKV_DIM), mat(64, RWKV_DIM),
                mat(128, RWKV_DIM), vec(RWKV_DIM), vec(RWKV_DIM), vec(RWKV_DIM), vec(RWKV_DIM), vec(RWKV_DIM)]
    out_shape = (jax.ShapeDtypeStruct((n_batch * t_total, RWKV_DIM), F32),
                 jax.ShapeDtypeStruct((n_batch, RWKV_HEADS, HEAD_DIM, HEAD_DIM), F32),
                 jax.ShapeDtypeStruct((n_batch, 8, RWKV_PROJ), F32))
    out_specs = (pl.BlockSpec((G * C, RWKV_DIM), lambda b, c: (b * nc + c, 0)),
                 pl.BlockSpec((gb, RWKV_HEADS, HEAD_DIM, HEAD_DIM), lambda b, c: (b, 0, 0, 0)),
                 pl.BlockSpec((gb, 8, RWKV_PROJ), lambda b, c: (b, 0, 0)))
    return pl.pallas_call(
        functools.partial(_rwkv_body, C, G, by_batch, t_valid, t_total),
        grid=(n_outer, nc), in_specs=in_specs, out_specs=out_specs, out_shape=out_shape,
        scratch_shapes=[pltpu.VMEM((RWKV_DIM, RWKV_DIM), F32), pltpu.VMEM((8, RWKV_PROJ), F32)],
        compiler_params=_params(("arbitrary", "arbitrary")), name="rwkv")(
            p, shift_prev.reshape(n_batch, 1, RWKV_PROJ), s0,
            lp["rwkv_mu"], lp["rwkv_w0"], lp["rwkv_w2"], lp["rwkv_a0"], lp["rwkv_a2"], lp["rwkv_g2"],
            lp["rwkv_k_k"], lp["rwkv_k_a"], lp["rwkv_r_k"], lp["rwkv_ln_w"], lp["rwkv_ln_b"])


def _alibi_slope(h):
    return 2.0 ** (-8.0 * (h + 1) / ATT_HEADS)


def _att_prompt_body(t_len, q_ref, k_ref, v_ref, nw_ref, o_ref, acc_ref, m_ref, l_ref, bias_ref):
    blk = ATT_STEPS
    pairs = range(ATT_PAIRS)
    lane_top = lax.broadcasted_iota(jnp.int32, (blk, LANES), 1) < HEAD_DIM

    @pl.when(pl.program_id(0) == 0)
    def _():
        qi = _mod(lax.broadcasted_iota(jnp.int32, (2 * blk, 2 * blk), 0), blk)
        kj = lax.broadcasted_iota(jnp.int32, (2 * blk, 2 * blk), 1)
        top = lax.broadcasted_iota(jnp.int32, (2 * blk, 2 * blk), 0) < blk
        for di, d in enumerate(DILATIONS):
            for off in range(2):
                delta = off * blk + qi - kj
                valid = (delta >= 0) & (delta <= blk)
                dist = (d * delta).astype(F32) * LOG2E
                for j in pairs:
                    slope = jnp.where(top, _alibi_slope(2 * j), _alibi_slope(2 * j + 1))
                    bias_ref[di, off, j] = jnp.where(valid, -slope * dist, NEG_INF)

    for di, d in enumerate(DILATIONS):
        n = t_len // d
        kw = min(2 * blk, n)
        nb = n // blk

        def step(it, carry, di=di, d=d, kw=kw, nb=nb):
            res = it // nb
            b = it % nb
            kb = jnp.maximum(b - 1, 0) if kw > blk else b
            q_rows = pl.ds(res + d * blk * b, blk, stride=d)
            k_rows = pl.ds(res + d * blk * kb, kw, stride=d)
            q2 = [q_ref.at[j][q_rows, :] * (HEAD_DIM ** -0.5 * LOG2E) for j in pairs]
            k2 = [k_ref.at[j][k_rows, :] for j in pairs]
            v2 = [v_ref.at[j][k_rows, :] for j in pairs]
            q_st = [jnp.concatenate([jnp.where(lane_top, x, 0.0), jnp.where(lane_top, 0.0, x)], axis=0) for x in q2]
            s = [_dot_nt(x, y) + bias_ref[di, b - kb, j, :, 0:kw] for j, x, y in zip(pairs, q_st, k2)]
            mh = [jnp.max(x, axis=-1, keepdims=True) for x in s]
            pr = [jnp.exp2(x - m) for x, m in zip(s, mh)]
            lh = [jnp.sum(x, axis=-1, keepdims=True) for x in pr]
            ah = [_dot(x, y) for x, y in zip(pr, v2)]

            def unstack(x):
                x = jnp.broadcast_to(x, (2 * blk, LANES))
                return jnp.where(lane_top, x[0:blk], x[blk:])

            results = []
            for j in pairs:
                acc, m, l = unstack(ah[j]), unstack(mh[j]), unstack(lh[j])
                if d != DILATIONS[0]:
                    m_old = m_ref.at[j][q_rows, :]
                    m_new = jnp.maximum(m_old, m)
                    e_old = jnp.exp2(m_old - m_new)
                    e_new = jnp.exp2(m - m_new)
                    acc = acc_ref.at[j][q_rows, :] * e_old + acc * e_new
                    l = l_ref.at[j][q_rows, :] * e_old + l * e_new
                    m = m_new
                results.append((acc, m, l))
            for j, (acc, m, l) in enumerate(results):
                acc_ref.at[j][q_rows, :] = acc
                m_ref.at[j][q_rows, :] = m
                l_ref.at[j][q_rows, :] = l
            return carry

        lax.fori_loop(0, d * nb, step, 0)

    def finish(it, carry):
        rows = pl.ds(pl.multiple_of(it * blk, blk), blk)
        ys = [acc_ref[j, rows, :] / l_ref[j, rows, :] for j in range(ATT_PAIRS)]
        ms = sum(jnp.sum(y * y, axis=-1, keepdims=True) for y in ys) * (1.0 / ATT_DIM)
        scale = lax.rsqrt(ms + NORM_EPS)
        for j in range(ATT_PAIRS):
            o_ref[rows, j * LANES:(j + 1) * LANES] = ys[j] * scale * nw_ref[:, j * LANES:(j + 1) * LANES]
        return carry

    lax.fori_loop(0, t_len // blk, finish, 0)


def _att_prompt_call(q3, k3, v3, nw, n_batch, t_len):
    blk3 = pl.BlockSpec((ATT_PAIRS, t_len, LANES), lambda b: (0, b, 0))
    return pl.pallas_call(
        functools.partial(_att_prompt_body, t_len),
        grid=(n_batch,),
        in_specs=[blk3, blk3, blk3, pl.BlockSpec((1, ATT_DIM), lambda b: (0, 0))],
        out_specs=pl.BlockSpec((t_len, ATT_DIM), lambda b: (b, 0)),
        out_shape=jax.ShapeDtypeStruct((n_batch * t_len, ATT_DIM), F32),
        scratch_shapes=[pltpu.VMEM((ATT_PAIRS, t_len, LANES), F32)] * 3
        + [pltpu.VMEM((len(DILATIONS), 2, ATT_PAIRS, 2 * ATT_STEPS, 2 * ATT_STEPS), F32)],
        compiler_params=_params(("arbitrary",)), name="att_prompt")(q3, k3, v3, nw)


def _multiplicity(dist):
    mult = jnp.zeros(dist.shape, F32)
    for d in DILATIONS:
        hit = (_mod(dist, d) == 0) & (dist <= d * ATT_STEPS)
        mult = mult + jnp.where(hit, 1.0, 0.0)
    return mult


def _att_sample_body(t_new, w_buf, q_ref, kn_ref, vn_ref, kt_ref, vt_ref, nw_ref, o_ref):
    RQ = 8
    rows = ATT_HEADS * RQ
    q = q_ref[0] * (HEAD_DIM ** -0.5)
    head_of_row = _div(lax.broadcasted_iota(jnp.int32, (rows, ATT_DIM), 0), RQ)
    head_of_lane = _div(lax.broadcasted_iota(jnp.int32, (rows, ATT_DIM), 1), HEAD_DIM)
    own = head_of_row == head_of_lane
    q_bd = jnp.where(own, jnp.concatenate([q] * ATT_HEADS, axis=0), 0.0)

    s_buf = _dot(q_bd, kt_ref[0, 0])
    s_new = _dot_nt(q_bd, kn_ref[0])

    def weights(s, key_pos):
        t = _mod(lax.broadcasted_iota(jnp.int32, s.shape, 0), RQ)
        h = _div(lax.broadcasted_iota(jnp.int32, s.shape, 0), RQ)
        dist = w_buf + t - key_pos
        slope = sum(jnp.where(h == hh, _alibi_slope(hh), 0.0) for hh in range(ATT_HEADS))
        mult = jnp.where(dist >= 0, _multiplicity(jnp.maximum(dist, 0)), 0.0)
        return jnp.where(mult > 0, s - slope * dist.astype(F32), NEG_INF), mult

    sb, mult_b = weights(s_buf, lax.broadcasted_iota(jnp.int32, s_buf.shape, 1))
    sn, mult_n = weights(s_new, w_buf + lax.broadcasted_iota(jnp.int32, s_new.shape, 1))
    m = jnp.maximum(jnp.max(sb, axis=-1, keepdims=True), jnp.max(sn, axis=-1, keepdims=True))
    pb = mult_b * jnp.exp(sb - m)
    pn = mult_n * jnp.exp(sn - m)
    den = jnp.sum(pb, axis=-1, keepdims=True) + jnp.sum(pn, axis=-1, keepdims=True)
    num = _dot_nt(pb, vt_ref[0, 0]) + _dot(pn, vn_ref[0])
    out = jnp.where(own, num / den, 0.0).reshape(ATT_HEADS, RQ, ATT_DIM)
    o_ref[0] = _rms(jnp.sum(out, axis=0), nw_ref[...])


def _att_sample_call(q, k_new, v_new, cache_kt, cache_vt, layer, nw, t_new):
    depth, n_batch, _, w_buf = cache_kt.shape
    small = pl.BlockSpec((1, 8, ATT_DIM), lambda b: (b, 0, 0))
    big = pl.BlockSpec((1, 1, ATT_DIM, w_buf), lambda b: (layer, b, 0, 0))
    return pl.pallas_call(
        functools.partial(_att_sample_body, t_new, w_buf),
        grid=(n_batch,),
        in_specs=[small, small, small, big, big, pl.BlockSpec((1, ATT_DIM), lambda b: (0, 0))],
        out_specs=small, out_shape=jax.ShapeDtypeStruct((n_batch, 8, ATT_DIM), F32),
        compiler_params=_params(("arbitrary",)), name="att_sample")(q, k_new, v_new, cache_kt, cache_vt, nw)


def _ssd_body(chunk, t_valid, t_total, z_ref, xbc_ref, dt_ref, cp_ref, h0_ref, cw_ref, cb_ref, dtb_ref, alog_ref,
              dskip_ref, nw_ref, y_ref, ho_ref, cv_ref, hs_ref, xc_ref):
    L = chunk
    P = HEAD_DIM
    NS = SSM_STATE
    c = pl.program_id(1)
    nc = pl.num_programs(1)

    @pl.when(c == 0)
    def _():
        hs_ref[...] = h0_ref[0]
        xc_ref[0:8, :] = cp_ref[0]

    xc_ref[8:8 + L, :] = xbc_ref[...]
    conv = cb_ref[...]
    for i in range(SSM_CONV):
        conv = conv + xc_ref[5 + i:5 + i + L, :] * cw_ref[i:i + 1, :]
    last_valid = t_valid - (t_total // L - 1) * L

    @pl.when(c == nc - 1)
    def _():
        cv_ref[0] = xc_ref[last_valid:last_valid + 8, :]

    xc_ref[0:8, :] = xc_ref[L:L + 8, :]
    act = conv * _sigmoid(conv)
    xs = act[:, 0:SSM_DIM]
    bm = act[:, SSM_DIM:SSM_DIM + SSM_GROUPS * NS]
    cm = act[:, SSM_DIM + SSM_GROUPS * NS:]

    dt = _softplus(dt_ref[...] + dtb_ref[...])
    if t_valid < t_total:
        live = (c * L + lax.broadcasted_iota(jnp.int32, dt.shape, 0)) < t_valid
        dt = jnp.where(live, dt, 0.0)
    a_neg = -jnp.exp(alog_ref[...])
    acum = _cumsum_rows(dt * a_neg)
    acum_t = acum.T
    dt_t = dt.T
    li = lax.broadcasted_iota(jnp.int32, (L, L), 0)
    si = lax.broadcasted_iota(jnp.int32, (L, L), 1)
    causal = li >= si
    cb = [_dot_nt(cm[:, g * NS:(g + 1) * NS], bm[:, g * NS:(g + 1) * NS]) for g in range(SSM_GROUPS)]
    rep = SSM_HEADS // SSM_GROUPS
    GW = rep * P
    bd_row = _div(lax.broadcasted_iota(jnp.int32, (rep * L, GW), 0), L)
    bd_lane = _div(lax.broadcasted_iota(jnp.int32, (rep * L, GW), 1), P)

    def lanes_of(cols):
        return jnp.concatenate([jnp.broadcast_to(x, (x.shape[0], P)) for x in cols], axis=1)

    ys = []
    for g in range(SSM_GROUPS):
        heads = range(g * rep, (g + 1) * rep)
        x_g = xs[:, g * GW:(g + 1) * GW]
        b_g = bm[:, g * NS:(g + 1) * NS]
        c_g = cm[:, g * NS:(g + 1) * NS]
        scores = [cb[g] * jnp.exp(jnp.where(causal, acum[:, h:h + 1] - acum_t[h:h + 1, :], NEG_INF)) * dt_t[h:h + 1, :]
                  for h in heads]
        ac = lanes_of([acum[:, h:h + 1] for h in heads])
        a_last = lanes_of([acum[L - 1:L, h:h + 1] for h in heads])
        x_bd = jnp.where(bd_row == bd_lane, jnp.concatenate([x_g] * rep, axis=0), 0.0)
        h_g = hs_ref[g * rep:(g + 1) * rep].reshape(GW, NS)
        y = (_dot(jnp.concatenate(scores, axis=1), x_bd) + _dot_nt(c_g, h_g) * jnp.exp(ac)
             + lanes_of([dskip_ref[:, h:h + 1] for h in heads]) * x_g)
        to_end = jnp.exp(a_last - ac) * lanes_of([dt[:, h:h + 1] for h in heads])
        carry = jnp.concatenate([jnp.broadcast_to(jnp.exp(acum[L - 1:L, h:h + 1]), (P, NS)) for h in heads], axis=0)
        hs_ref[g * rep:(g + 1) * rep] = (h_g * carry + _dot_tn(x_g * to_end, b_g)).reshape(rep, P, NS)
        ys.append(y)
    y = jnp.concatenate(ys, axis=1)
    z = z_ref[...]
    y = y * (z * _sigmoid(z))
    y_ref[...] = _rms(y, nw_ref[...])

    @pl.when(c == nc - 1)
    def _():
        ho_ref[0] = hs_ref[...]


def _ssd_call(z, xbc, dt, conv_prev8, h0, lp, n_batch, t_total, t_valid, chunk):
    L = chunk
    nc = t_total // L
    rows = lambda w: pl.BlockSpec((L, w), lambda b, c: (b * nc + c, 0))
    vec = lambda a, n: pl.BlockSpec((a, n), lambda b, c: (0, 0))
    in_specs = [rows(SSM_DIM), rows(SSM_CONV_DIM), rows(LANES),
                pl.BlockSpec((1, 8, SSM_CONV_DIM), lambda b, c: (b, 0, 0)),
                pl.BlockSpec((1, SSM_HEADS, HEAD_DIM, SSM_STATE), lambda b, c: (b, 0, 0, 0)),
                vec(SSM_CONV, SSM_CONV_DIM), vec(1, SSM_CONV_DIM), vec(1, LANES), vec(1, LANES), vec(1, LANES),
                vec(1, SSM_DIM)]
    out_shape = (jax.ShapeDtypeStruct((n_batch * t_total, SSM_DIM), F32),
                 jax.ShapeDtypeStruct((n_batch, SSM_HEADS, HEAD_DIM, SSM_STATE), F32),
                 jax.ShapeDtypeStruct((n_batch, 8, SSM_CONV_DIM), F32))
    out_specs = (rows(SSM_DIM), pl.BlockSpec((1, SSM_HEADS, HEAD_DIM, SSM_STATE), lambda b, c: (b, 0, 0, 0)),
                 pl.BlockSpec((1, 8, SSM_CONV_DIM), lambda b, c: (b, 0, 0)))
    return pl.pallas_call(
        functools.partial(_ssd_body, L, t_valid, t_total),
        grid=(n_batch, nc), in_specs=in_specs, out_specs=out_specs, out_shape=out_shape,
        scratch_shapes=[pltpu.VMEM((SSM_HEADS, HEAD_DIM, SSM_STATE), F32), pltpu.VMEM((L + 8, SSM_CONV_DIM), F32)],
        compiler_params=_params(("arbitrary", "arbitrary")), name="ssd")(
            z, xbc, dt, conv_prev8, h0, lp["ssm_conv_w"], lp["ssm_conv_b"], lp["ssm_dt_bias"], lp["ssm_A_log"],
            lp["ssm_D"], lp["ssm_norm_w"])


def _pad_lanes(v, n):
    return jnp.pad(v, ((0, 0), (0, n - v.shape[-1])))


def _pad_rows(x, n):
    return jnp.pad(x, ((0, n - x.shape[0]),) + ((0, 0),) * (x.ndim - 1))


def kernel(x_prompt, x_sample, cache_att_k, cache_att_v, state_rwkv_shift, state_rwkv, state_ssm_conv, state_ssm, ffn1_norm, ffn1_w_gate, ffn1_w_up, ffn1_w_down, mix_norm, w_in, rwkv_mu, rwkv_w0, rwkv_w2, rwkv_a0, rwkv_a2, rwkv_g2, rwkv_k_k, rwkv_k_a, rwkv_r_k, rwkv_ln_w, rwkv_ln_b, att_norm_w, ssm_conv_w, ssm_conv_b, ssm_dt_bias, ssm_A_log, ssm_D, ssm_norm_w, w_out, ffn2_norm, ffn2_w_gate, ffn2_w_up, ffn2_w_down, final_norm):
    depth = w_in.shape[0]
    bp, tp, _ = x_prompt.shape
    bs, ts, _ = x_sample.shape
    w_buf = cache_att_k.shape[2]
    mp = bp * tp
    ms = bs * ts
    assert mp % ROW_TILE == 0 and ms <= ROW_TILE and tp % (ATT_STEPS * DILATIONS[-1]) == 0 and ts <= 8
    n_ptiles = mp // ROW_TILE
    ts_rwkv = RWKV_CHUNK
    ts_ssd = 8

    x = jnp.concatenate([x_prompt.reshape(mp, D_MODEL), _pad_rows(x_sample.reshape(ms, D_MODEL), ROW_TILE)], axis=0)
    zeros_shift = jnp.zeros((bp, RWKV_PROJ), F32)
    zeros_s = jnp.zeros((bp, RWKV_HEADS, HEAD_DIM, HEAD_DIM), F32)
    zeros_conv = jnp.zeros((bp, 8, SSM_CONV_DIM), F32)
    zeros_h = jnp.zeros((bp, SSM_HEADS, HEAD_DIM, SSM_STATE), F32)
    to_cp = lambda c: jnp.transpose(c, (0, 1, 3, 4, 2)).reshape(depth, bs, ATT_DIM, w_buf)
    cache_kt, cache_vt = to_cp(cache_att_k), to_cp(cache_att_v)

    outs = {k: [] for k in ("p_k", "p_v", "p_shift", "p_s", "p_conv", "p_h",
                            "s_k", "s_v", "s_shift", "s_s", "s_conv", "s_h")}
    mix = None
    wo = None
    y_prompt = y_sample = None
    for l in range(depth):
        row1 = lambda v: v[l].reshape(1, -1)
        lp = dict(rwkv_mu=row1(rwkv_mu), rwkv_w0=row1(rwkv_w0), rwkv_w2=rwkv_w2[l], rwkv_a0=row1(rwkv_a0),
                  rwkv_a2=rwkv_a2[l], rwkv_g2=rwkv_g2[l], rwkv_k_k=row1(rwkv_k_k), rwkv_k_a=row1(rwkv_k_a),
                  rwkv_r_k=row1(rwkv_r_k), rwkv_ln_w=row1(rwkv_ln_w), rwkv_ln_b=row1(rwkv_ln_b),
                  ssm_conv_w=ssm_conv_w[l], ssm_conv_b=row1(ssm_conv_b),
                  ssm_dt_bias=_pad_lanes(row1(ssm_dt_bias), LANES), ssm_A_log=_pad_lanes(row1(ssm_A_log), LANES),
                  ssm_D=_pad_lanes(row1(ssm_D), LANES), ssm_norm_w=row1(ssm_norm_w))
        bf = lambda w: w[l].astype(BF16)
        if l == 0:
            x = _ffn_call(x, None, None, row1(ffn1_norm), bf(ffn1_w_gate), bf(ffn1_w_up), bf(ffn1_w_down), None,
                          n_ptiles)
        (p_rwkv, q3, k3, v3, k_cp, v_cp, z, xbc, dt, pr_t, q_s, k_s, v_s, z_t, xbc_t, dt_t) = _proj_call(
            x, row1(mix_norm), _pad_lanes(w_in[l], PROJ_PAD).astype(BF16), n_ptiles, tp)

        ya_p, s_p, last_p = _rwkv_call(p_rwkv, zeros_shift, zeros_s, lp, bp, tp, tp)
        yb_p = _att_prompt_call(q3, k3, v3, row1(att_norm_w), bp, tp)
        yc_p, h_p, conv_p = _ssd_call(z, xbc, dt, zeros_conv, zeros_h, lp, bp, tp, tp, SSD_CHUNK)

        def sample_rows(a, t_pad):
            return jnp.pad(a[:ms].reshape(bs, ts, -1), ((0, 0), (0, t_pad - ts), (0, 0)))

        pr_s = sample_rows(pr_t, ts_rwkv)
        ya_s, s_s, last_s = _rwkv_call(pr_s.reshape(bs * ts_rwkv, RWKV_PROJ), state_rwkv_shift[l], state_rwkv[l], lp,
                               bs, ts_rwkv, ts)
        yb_s = _att_sample_call(sample_rows(q_s, 8), sample_rows(k_s, 8), sample_rows(v_s, 8),
                                cache_kt, cache_vt, l, row1(att_norm_w), ts)
        conv8 = jnp.pad(state_ssm_conv[l], ((0, 0), (8 - (SSM_CONV - 1), 0), (0, 0)))
        flat8 = lambda a: sample_rows(a, ts_ssd).reshape(bs * ts_ssd, -1)
        yc_s, h_s, conv_s = _ssd_call(flat8(z_t), flat8(xbc_t), flat8(dt_t), conv8, state_ssm[l], lp, bs, ts_ssd, ts,
                                      ts_ssd)

        def tile_rows(a, t_pad):
            return _pad_rows(a.reshape(bs, t_pad, -1)[:, :ts].reshape(ms, -1), ROW_TILE)

        mix = ((ya_p, tile_rows(ya_s, ts_rwkv)), (yb_p, tile_rows(yb_s, 8)), (yc_p, tile_rows(yc_s, ts_ssd)))
        wo = w_out[l].astype(BF16)

        keep = min(2048, tp)
        from_cp = lambda a: jnp.transpose(a.reshape(bp, ATT_HEADS, HEAD_DIM, tp), (0, 3, 1, 2))[:, tp - keep:]
        outs["p_k"].append(from_cp(k_cp))
        outs["p_v"].append(from_cp(v_cp))
        outs["p_shift"].append(last_p[:, 0])
        outs["p_s"].append(s_p)
        outs["p_conv"].append(conv_p[:, 8 - (SSM_CONV - 1):])
        outs["p_h"].append(h_p)
        outs["s_k"].append(k_s[:ms].reshape(bs, ts, ATT_HEADS, HEAD_DIM))
        outs["s_v"].append(v_s[:ms].reshape(bs, ts, ATT_HEADS, HEAD_DIM))
        outs["s_shift"].append(last_s[:, 0])
        outs["s_s"].append(s_s)
        outs["s_conv"].append(conv_s[:, 8 - (SSM_CONV - 1):])
        outs["s_h"].append(h_s)

        x = _ffn_call(x, mix, wo, row1(ffn2_norm), bf(ffn2_w_gate), bf(ffn2_w_up), bf(ffn2_w_down),
                      final_norm.reshape(1, -1) if l == depth - 1 else None, n_ptiles)
        if l == depth - 1:
            y_prompt, y_sample = x
        else:
            lnext = l + 1
            x = _ffn_call(x, None, None, ffn1_norm[lnext].reshape(1, -1), ffn1_w_gate[lnext].astype(BF16),
                          ffn1_w_up[lnext].astype(BF16), ffn1_w_down[lnext].astype(BF16), None, n_ptiles)

    st = lambda k: jnp.stack(outs[k])
    return (y_prompt.reshape(bp, tp, D_MODEL), y_sample[:ms].reshape(bs, ts, D_MODEL),
            st("p_k"), st("p_v"), st("p_shift"), st("p_s"), st("p_conv"), st("p_h"),
            st("s_k"), st("s_v"), st("s_shift"), st("s_s"), st("s_conv"), st("s_h"))
```

```python
import functools

import jax
import jax.numpy as jnp
from jax import lax
from jax.experimental import pallas as pl
from jax.experimental.pallas import tpu as pltpu

F32 = jnp.float32
BF16 = jnp.bfloat16

D_MODEL = 1024
FFN_DIM = 2816
HEAD_DIM = 64
RWKV_HEADS = 4
RWKV_DIM = RWKV_HEADS * HEAD_DIM
RWKV_PROJ = 1024
RWKV_GN_EPS = 64e-5
ATT_HEADS = 6
ATT_DIM = ATT_HEADS * HEAD_DIM
ATT_PAIRS = ATT_HEADS // 2
DILATIONS = (1, 4, 16)
ATT_STEPS = 128
SSM_HEADS = 6
SSM_DIM = SSM_HEADS * HEAD_DIM
SSM_GROUPS = 2
SSM_STATE = 128
SSM_CONV = 4
SSM_CONV_DIM = SSM_DIM + 2 * SSM_GROUPS * SSM_STATE
PROJ_DIM = RWKV_PROJ + 3 * ATT_DIM + SSM_DIM + SSM_CONV_DIM + SSM_HEADS
PROJ_PAD = 3584
NORM_EPS = 1e-6
NEG_INF = -1e30
LOG2E = 1.4426950408889634

LANES = 128
ROW_TILE = 512
FFN_CHUNK = 256
RWKV_CHUNK = 64
RWKV_GROUP = 8
SSD_CHUNK = 128
VMEM_LIMIT = 56 * 1024 * 1024

OFF_ATT = RWKV_PROJ
OFF_Z = OFF_ATT + 3 * ATT_DIM
OFF_XBC = OFF_Z + SSM_DIM
OFF_DT = OFF_XBC + SSM_CONV_DIM


def _params(sem):
    return pltpu.CompilerParams(dimension_semantics=sem, vmem_limit_bytes=VMEM_LIMIT)


def _dot(a, b):
    return jnp.dot(a.astype(BF16), b.astype(BF16), preferred_element_type=F32)


def _dot_nt(a, b):
    return lax.dot_general(a.astype(BF16), b.astype(BF16), (((1,), (1,)), ((), ())),
                           preferred_element_type=F32)


def _dot_tn(a, b):
    return lax.dot_general(a.astype(BF16), b.astype(BF16), (((0,), (0,)), ((), ())),
                           preferred_element_type=F32)


def _dot_hi(a, b):
    return jnp.dot(a, b, preferred_element_type=F32, precision=lax.Precision.HIGHEST)


def _rms(x, w):
    return x * lax.rsqrt(jnp.mean(x * x, axis=-1, keepdims=True) + NORM_EPS) * w


def _sigmoid(x):
    return 1.0 / (1.0 + jnp.exp(-x))


def _softplus(x):
    return jnp.maximum(x, 0.0) + jnp.log(1.0 + jnp.exp(-jnp.abs(x)))


def _div(x, n):
    assert n & (n - 1) == 0
    return lax.shift_right_logical(x, n.bit_length() - 1)


def _mod(x, n):
    assert n & (n - 1) == 0
    return x & (n - 1)


def _cumsum_rows(x):
    n = x.shape[0]
    row = lax.broadcasted_iota(jnp.int32, x.shape, 0)
    s = 1
    while s < n:
        x = x + jnp.where(row >= s, pltpu.roll(x, s, axis=0), 0.0)
        s *= 2
    return x


def _ffn_body(has_mix, final, n_prompt_tiles, *refs):
    it = iter(refs)
    x_ref = next(it)
    if has_mix:
        yap, yas, ybp, ybs, ycp, ycs, wo_ref = (next(it) for _ in range(7))
    nw_ref, wg_ref, wu_ref, wd_ref = (next(it) for _ in range(4))
    if final:
        fn_ref = next(it)
        op_ref, os_ref = next(it), next(it)
    else:
        o_ref = next(it)
    h_ref, acc_ref = next(it), next(it)
    i = pl.program_id(0)
    is_sample = i >= n_prompt_tiles

    x = x_ref[...]
    if has_mix:
        ya = jnp.where(is_sample, yas[...], yap[...])
        yb = jnp.where(is_sample, ybs[...], ybp[...])
        yc = jnp.where(is_sample, ycs[...], ycp[...])
        x = (x + _dot(ya, wo_ref[0:RWKV_DIM, :])
             + _dot(yb, wo_ref[RWKV_DIM:RWKV_DIM + ATT_DIM, :])
             + _dot(yc, wo_ref[RWKV_DIM + ATT_DIM:, :]))
    h_ref[...] = _rms(x, nw_ref[...]).astype(BF16)
    acc_ref[...] = x

    def chunk(c, carry):
        off = pl.multiple_of(c * FFN_CHUNK, FFN_CHUNK)
        h = h_ref[...]
        g = jnp.dot(h, wg_ref[:, pl.ds(off, FFN_CHUNK)], preferred_element_type=F32)
        u = jnp.dot(h, wu_ref[:, pl.ds(off, FFN_CHUNK)], preferred_element_type=F32)
        a = (g * _sigmoid(g) * u).astype(BF16)
        acc_ref[...] += 0.5 * jnp.dot(a, wd_ref[pl.ds(off, FFN_CHUNK), :], preferred_element_type=F32)
        return carry

    lax.fori_loop(0, FFN_DIM // FFN_CHUNK, chunk, 0)
    if final:
        y = _rms(acc_ref[...], fn_ref[...])

        @pl.when(jnp.logical_not(is_sample))
        def _():
            op_ref[...] = y

        @pl.when(is_sample)
        def _():
            os_ref[...] = y
    else:
        o_ref[...] = acc_ref[...]


def _ffn_call(x, mix, wo, nw, wg, wu, wd, final_norm, n_prompt_tiles):
    m_pad = x.shape[0]
    tm = ROW_TILE
    has_mix = mix is not None
    final = final_norm is not None
    last_p = n_prompt_tiles - 1
    row = lambda i: (i, 0)
    fixed = lambda i: (0, 0)
    prow = lambda i: (jnp.minimum(i, last_p), 0)
    resident = functools.partial(pl.BlockSpec, index_map=fixed, pipeline_mode=pl.Buffered(1))
    in_specs = [pl.BlockSpec((tm, D_MODEL), row)]
    args = [x]
    if has_mix:
        for yp, ys in mix:
            w = yp.shape[1]
            in_specs += [pl.BlockSpec((tm, w), prow), pl.BlockSpec((tm, w), fixed)]
            args += [yp, ys]
        in_specs.append(resident((D_MODEL, D_MODEL)))
        args.append(wo)
    in_specs += [pl.BlockSpec((1, D_MODEL), fixed), resident((D_MODEL, FFN_DIM)),
                 resident((D_MODEL, FFN_DIM)), resident((FFN_DIM, D_MODEL))]
    args += [nw, wg, wu, wd]
    if final:
        in_specs.append(pl.BlockSpec((1, D_MODEL), fixed))
        args.append(final_norm)
        out_shape = (jax.ShapeDtypeStruct((n_prompt_tiles * tm, D_MODEL), F32),
                     jax.ShapeDtypeStruct((tm, D_MODEL), F32))
        out_specs = (pl.BlockSpec((tm, D_MODEL), prow), pl.BlockSpec((tm, D_MODEL), fixed))
    else:
        out_shape = jax.ShapeDtypeStruct((m_pad, D_MODEL), F32)
        out_specs = pl.BlockSpec((tm, D_MODEL), row)
    return pl.pallas_call(
        functools.partial(_ffn_body, has_mix, final, n_prompt_tiles),
        grid=(m_pad // tm,), in_specs=in_specs, out_specs=out_specs, out_shape=out_shape,
        scratch_shapes=[pltpu.VMEM((tm, D_MODEL), BF16), pltpu.VMEM((tm, D_MODEL), F32)],
        compiler_params=_params(("arbitrary",)), name="ffn")(*args)


def _proj_body(n_prompt_tiles, x_ref, nw_ref, w_ref, pr_ref, q3_ref, k3_ref, v3_ref, kn_ref, vn_ref, z_ref, xbc_ref,
               dt_ref, prs_ref, qs_ref, ks_ref, vs_ref, zs_ref, xbcs_ref, dts_ref):
    i = pl.program_id(0)
    is_sample = i >= n_prompt_tiles
    h = _rms(x_ref[...], nw_ref[...]).astype(BF16)

    def seg(a, b):
        return jnp.dot(h, w_ref[:, a:b], preferred_element_type=F32)

    pr = seg(0, OFF_ATT)
    q = seg(OFF_ATT, OFF_ATT + ATT_DIM)
    k = seg(OFF_ATT + ATT_DIM, OFF_ATT + 2 * ATT_DIM)
    v = seg(OFF_ATT + 2 * ATT_DIM, OFF_Z)
    z = seg(OFF_Z, OFF_XBC)
    xbc = seg(OFF_XBC, OFF_DT)
    dt = seg(OFF_DT, PROJ_PAD)

    @pl.when(jnp.logical_not(is_sample))
    def _():
        pr_ref[...] = pr
        z_ref[...] = z
        xbc_ref[...] = xbc
        dt_ref[...] = dt
        kn_ref[0] = k.T
        vn_ref[0] = v.T
        for j in range(ATT_PAIRS):
            q3_ref[j] = q[:, j * LANES:(j + 1) * LANES]
            k3_ref[j] = k[:, j * LANES:(j + 1) * LANES]
            v3_ref[j] = v[:, j * LANES:(j + 1) * LANES]

    @pl.when(is_sample)
    def _():
        prs_ref[...] = pr
        qs_ref[...] = q
        ks_ref[...] = k
        vs_ref[...] = v
        zs_ref[...] = z
        xbcs_ref[...] = xbc
        dts_ref[...] = dt


def _proj_call(x, nw, w, n_prompt_tiles, t_prompt):
    m_pad = x.shape[0]
    tm = ROW_TILE
    mp = n_prompt_tiles * tm
    last_p = n_prompt_tiles - 1
    tiles_per_seq = t_prompt // tm
    row = lambda i: (i, 0)
    fixed = lambda i: (0, 0)
    prow = lambda i: (jnp.minimum(i, last_p), 0)
    prow3 = lambda i: (0, jnp.minimum(i, last_p), 0)
    pcol = lambda i: (jnp.minimum(i, last_p) // tiles_per_seq, 0, jnp.minimum(i, last_p) % tiles_per_seq)
    sds = jax.ShapeDtypeStruct
    widths = (RWKV_PROJ, ATT_DIM, ATT_DIM, ATT_DIM, SSM_DIM, SSM_CONV_DIM, LANES)
    out_shape = ((sds((mp, RWKV_PROJ), F32),
                  sds((ATT_PAIRS, mp, LANES), F32), sds((ATT_PAIRS, mp, LANES), F32), sds((ATT_PAIRS, mp, LANES), F32),
                  sds((mp // t_prompt, ATT_DIM, t_prompt), F32), sds((mp // t_prompt, ATT_DIM, t_prompt), F32),
                  sds((mp, SSM_DIM), F32), sds((mp, SSM_CONV_DIM), F32), sds((mp, LANES), F32))
                 + tuple(sds((tm, w_), F32) for w_ in widths))
    p3 = pl.BlockSpec((ATT_PAIRS, tm, LANES), prow3)
    out_specs = ((pl.BlockSpec((tm, RWKV_PROJ), prow), p3, p3, p3,
                  pl.BlockSpec((1, ATT_DIM, tm), pcol), pl.BlockSpec((1, ATT_DIM, tm), pcol),
                  pl.BlockSpec((tm, SSM_DIM), prow), pl.BlockSpec((tm, SSM_CONV_DIM), prow),
                  pl.BlockSpec((tm, LANES), prow))
                 + tuple(pl.BlockSpec((tm, w_), fixed) for w_ in widths))
    return pl.pallas_call(
        functools.partial(_proj_body, n_prompt_tiles),
        grid=(m_pad // tm,),
        in_specs=[pl.BlockSpec((tm, D_MODEL), row), pl.BlockSpec((1, D_MODEL), fixed),
                  pl.BlockSpec((D_MODEL, PROJ_PAD), fixed, pipeline_mode=pl.Buffered(1))],
        out_specs=out_specs, out_shape=out_shape,
        compiler_params=_params(("arbitrary",)), name="proj")(x, nw, w)


def _rwkv_body(chunk, group, by_batch, t_valid, t_total, p_ref, sh_ref, s0_ref, mu_ref, w0_ref, w2_ref, a0_ref, a2_ref,
               g2_ref, kk_ref, ka_ref, rk_ref, lnw_ref, lnb_ref, y_ref, so_ref, last_ref, sb_ref, prev_ref):
    C = chunk
    G = group
    H = RWKV_HEADS
    N = HEAD_DIM
    R = RWKV_DIM
    GC = G * C
    c = pl.program_id(1)
    nc = pl.num_programs(1)
    p = p_ref[...]
    row_p = lax.broadcasted_iota(jnp.int32, p.shape, 0)
    p_prev = pltpu.roll(p, 1, axis=0)

    if by_batch:
        for gi in range(G):
            p_prev = jnp.where(row_p == gi * C, sh_ref[gi], p_prev)
        zero = jnp.zeros((N, N), F32)
        states = [jnp.concatenate([jnp.concatenate([s0_ref[gi, h] if hh == h else zero for hh in range(H)], axis=1)
                                   for h in range(H)], axis=0) for gi in range(G)]
    else:
        @pl.when(c == 0)
        def _():
            sb_ref[...] = jnp.zeros((R, R), F32)
            for h in range(H):
                sb_ref[h * N:(h + 1) * N, h * N:(h + 1) * N] = s0_ref[0, h]
            prev_ref[...] = jnp.broadcast_to(sh_ref[0], (8, RWKV_PROJ))

        p_prev = jnp.where(row_p == 0, prev_ref[0:1, :], p_prev)
        prev_ref[...] = jnp.broadcast_to(p[GC - 1:GC, :], (8, RWKV_PROJ))
    ps = p + mu_ref[...] * (p_prev - p)
    r = ps[:, 0:R]
    k = ps[:, R:2 * R]
    v = ps[:, 2 * R:3 * R]
    w_lr = ps[:, 3 * R:3 * R + 64]
    a_lr = ps[:, 3 * R + 64:3 * R + 128]
    g_lr = ps[:, 3 * R + 128:]

    w_log = -_softplus(-(w0_ref[...] + _dot(jnp.tanh(w_lr), w2_ref[...]))) - 0.5
    ld = -jnp.exp(w_log)
    a = _sigmoid(a0_ref[...] + _dot(a_lr, a2_ref[...]))
    g = _dot(_sigmoid(g_lr), g2_ref[...])

    lane_r = lax.broadcasted_iota(jnp.int32, (R, R), 1)
    row_r = lax.broadcasted_iota(jnp.int32, (R, R), 0)
    own_head = _div(lane_r, N) == _div(row_r, N)
    head_ones = jnp.where(own_head, 1.0, 0.0).astype(BF16)

    def head_sum(x):
        hi = x.astype(BF16)
        lo = (x - hi.astype(F32)).astype(BF16)
        return (jnp.dot(hi, head_ones, preferred_element_type=F32)
                + jnp.dot(lo, head_ones, preferred_element_type=F32))

    kk = k * kk_ref[...]
    kk = kk / jnp.maximum(jnp.sqrt(head_sum(kk * kk)), 1e-12)
    kp = k * (1.0 + (a - 1.0) * ka_ref[...])
    if t_valid < t_total:
        row_r2 = lax.broadcasted_iota(jnp.int32, (GC, R), 0)
        live = (_mod(row_r2, C) if by_batch else c * GC + row_r2) < t_valid
        ld = jnp.where(live, ld, 0.0)
        kk = jnp.where(live, kk, 0.0)
        kp = jnp.where(live, kp, 0.0)
        v = jnp.where(live, v, 0.0)

    HC = H * C
    bd_row = _div(lax.broadcasted_iota(jnp.int32, (HC, R), 0), C)
    bd_lane = _div(lax.broadcasted_iota(jnp.int32, (HC, R), 1), N)

    def bd_rows(x):
        return jnp.where(bd_row == bd_lane, jnp.concatenate([x] * H, axis=0), 0.0)

    sq_row = _div(lax.broadcasted_iota(jnp.int32, (HC, HC), 0), C)
    sq_col = _div(lax.broadcasted_iota(jnp.int32, (HC, HC), 1), C)

    def mm(x, y):
        return _dot(x, jnp.where(sq_row == sq_col, jnp.concatenate([y] * H, axis=0), 0.0))

    t_idx = lax.broadcasted_iota(jnp.int32, (C, HC), 0)
    s_idx = _mod(lax.broadcasted_iota(jnp.int32, (C, HC), 1), C)
    strict = s_idx < t_idx
    incl = s_idx <= t_idx
    eye = jnp.where(s_idx == t_idx, 1.0, 0.0).astype(F32)
    same16 = _div(s_idx, 16) == _div(t_idx, 16)

    chunks = [slice(gi * C, (gi + 1) * C) for gi in range(G)]
    cum = [_cumsum_rows(ld[rows]) for rows in chunks]
    w_inc = [jnp.exp(x) for x in cum]
    w_inv = [jnp.exp(-x) for x in cum]
    w_end = [x[C - 1:C, :] for x in w_inc]
    b_t = [kk[rows] * a[rows] * wi for rows, wi in zip(chunks, w_inv)]
    k_t = [kp[rows] * wi for rows, wi in zip(chunks, w_inv)]
    lhs = [jnp.concatenate([kk[rows] * jnp.exp(cm - ld[rows]), r[rows] * wi], axis=0)
           for rows, cm, wi in zip(chunks, cum, w_inc)]
    sc = [_dot_nt(x, jnp.concatenate([bd_rows(bt), bd_rows(kt)], axis=0)) for x, bt, kt in zip(lhs, b_t, k_t)]
    a_ab = [jnp.where(strict, x[0:C, 0:HC], 0.0) for x in sc]
    a_ak = [jnp.where(strict, x[0:C, HC:], 0.0) for x in sc]
    p_rb = [jnp.where(incl, x[C:, 0:HC], 0.0) for x in sc]
    p_rk = [jnp.where(incl, x[C:, HC:], 0.0) for x in sc]
    e1 = [jnp.where(same16, -x, 0.0) for x in a_ab]
    n_low = [x + y for x, y in zip(a_ab, e1)]
    e2 = [mm(x, x) for x in e1]
    st = [mm(jnp.concatenate([eye + x, y], axis=0), y) for x, y in zip(e1, e2)]
    p4 = [eye + x + y[0:C] for x, y in zip(e1, st)]
    e4 = [y[C:] for y in st]
    st = [mm(jnp.concatenate([x, y], axis=0), y) for x, y in zip(p4, e4)]
    p8 = [x + y[0:C] for x, y in zip(p4, st)]
    e8 = [y[C:] for y in st]
    d_inv = [x + mm(x, y) for x, y in zip(p8, e8)]
    f1 = [-mm(x, y) for x, y in zip(d_inv, n_low)]
    f2 = [mm(x, x) for x in f1]
    q3 = [eye + x + mm(eye + x, y) for x, y in zip(f1, f2)]
    t_inv = [mm(x, y) for x, y in zip(q3, d_inv)]
    v_c = [v[rows] for rows in chunks]
    v_bd = [bd_rows(x) for x in v_c]
    avp = [_dot(jnp.concatenate([x, y], axis=0), z) for x, y, z in zip(a_ak, p_rk, v_bd)]
    tail = [jnp.concatenate([bt * we, kt * we], axis=0) for bt, kt, we in zip(b_t, k_t, w_end)]

    def chain(gi, sb):
        ks_rs = _dot_nt(lhs[gi], sb)
        u = _dot(t_inv[gi], bd_rows(-ks_rs[0:C] - avp[gi][0:C]))
        upd = _dot_tn(jnp.concatenate([u, v_c[gi]], axis=0), tail[gi])
        y_c = ks_rs[C:] + _dot(p_rb[gi], bd_rows(u)) + avp[gi][C:]
        return jnp.where(own_head, sb * w_end[gi] + upd, 0.0), y_c

    ys = []
    if by_batch:
        for gi in range(G):
            sb, y_c = chain(gi, states[gi])
            ys.append(y_c)
            for h in range(H):
                so_ref[gi, h] = sb[h * N:(h + 1) * N, h * N:(h + 1) * N]
    else:
        sb = sb_ref[...]
        for gi in range(G):
            sb, y_c = chain(gi, sb)
            ys.append(y_c)
        sb_ref[...] = sb

        @pl.when(c == nc - 1)
        def _():
            for h in range(H):
                so_ref[0, h] = sb_ref[h * N:(h + 1) * N, h * N:(h + 1) * N]
    y = jnp.concatenate(ys, axis=0) if G > 1 else ys[0]

    yc = y - head_sum(y) * (1.0 / N)
    var = head_sum(yc * yc) * (1.0 / N)
    yn = yc * lax.rsqrt(var + RWKV_GN_EPS) * lnw_ref[...] + lnb_ref[...]
    bonus = head_sum(r * kp * rk_ref[...]) * v
    y_ref[...] = (yn + bonus) * g
    if by_batch:
        for gi in range(G):
            last_ref[gi] = jnp.broadcast_to(p[gi * C + t_valid - 1:gi * C + t_valid, :], (8, RWKV_PROJ))
    else:
        @pl.when(c == nc - 1)
        def _():
            last_ref[0] = prev_ref[...]


def _rwkv_call(p, shift_prev, s0, lp, n_batch, t_total, t_valid):
    C = RWKV_CHUNK
    by_batch = t_total == C
    G = RWKV_GROUP
    assert (n_batch if by_batch else t_total // C) % G == 0
    nc = 1 if by_batch else t_total // (G * C)
    n_outer = n_batch // G if by_batch else n_batch
    gb = G if by_batch else 1
    vec = lambda n: pl.BlockSpec((1, n), lambda b, c: (0, 0))
    mat = lambda a, n: pl.BlockSpec((a, n), lambda b, c: (0, 0))
    in_specs = [pl.BlockSpec((G * C, RWKV_PROJ), lambda b, c: (b * nc + c, 0)),
                pl.BlockSpec((gb, 1, RWKV_PROJ), lambda b, c: (b, 0, 0)),
                pl.BlockSpec((gb, RWKV_HEADS, HEAD_DIM, HEAD_DIM), lambda b, c: (b, 0, 0, 0)),
                vec(RWKV_PROJ), vec(RWKV_DIM), mat(64, RWKV_DIM), vec(RWKV_DIM), mat(64, RWKV_DIM),
                mat(128, RWKV_DIM), vec(RWKV_DIM), vec(RWKV_DIM), vec(RWKV_DIM), vec(RWKV_DIM), vec(RWKV_DIM)]
    out_shape = (jax.ShapeDtypeStruct((n_batch * t_total, RWKV_DIM), F32),
                 jax.ShapeDtypeStruct((n_batch, RWKV_HEADS, HEAD_DIM, HEAD_DIM), F32),
                 jax.ShapeDtypeStruct((n_batch, 8, RWKV_PROJ), F32))
    out_specs = (pl.BlockSpec((G * C, RWKV_DIM), lambda b, c: (b * nc + c, 0)),
                 pl.BlockSpec((gb, RWKV_HEADS, HEAD_DIM, HEAD_DIM), lambda b, c: (b, 0, 0, 0)),
                 pl.BlockSpec((gb, 8, RWKV_PROJ), lambda b, c: (b, 0, 0)))
    return pl.pallas_call(
        functools.partial(_rwkv_body, C, G, by_batch, t_valid, t_total),
        grid=(n_outer, nc), in_specs=in_specs, out_specs=out_specs, out_shape=out_shape,
        scratch_shapes=[pltpu.VMEM((RWKV_DIM, RWKV_DIM), F32), pltpu.VMEM((8, RWKV_PROJ), F32)],
        compiler_params=_params(("arbitrary", "arbitrary")), name="rwkv")(
            p, shift_prev.reshape(n_batch, 1, RWKV_PROJ), s0,
            lp["rwkv_mu"], lp["rwkv_w0"], lp["rwkv_w2"], lp["rwkv_a0"], lp["rwkv_a2"], lp["rwkv_g2"],
            lp["rwkv_k_k"], lp["rwkv_k_a"], lp["rwkv_r_k"], lp["rwkv_ln_w"], lp["rwkv_ln_b"])


def _alibi_slope(h):
    return 2.0 ** (-8.0 * (h + 1) / ATT_HEADS)


def _att_prompt_body(t_len, q_ref, k_ref, v_ref, nw_ref, o_ref, acc_ref, m_ref, l_ref, bias_ref):
    blk = ATT_STEPS
    pairs = range(ATT_PAIRS)
    lane_top = lax.broadcasted_iota(jnp.int32, (blk, LANES), 1) < HEAD_DIM

    @pl.when(pl.program_id(0) == 0)
    def _():
        qi = _mod(lax.broadcasted_iota(jnp.int32, (2 * blk, 2 * blk), 0), blk)
        kj = lax.broadcasted_iota(jnp.int32, (2 * blk, 2 * blk), 1)
        top = lax.broadcasted_iota(jnp.int32, (2 * blk, 2 * blk), 0) < blk
        for di, d in enumerate(DILATIONS):
            for off in range(2):
                delta = off * blk + qi - kj
                valid = (delta >= 0) & (delta <= blk)
                dist = (d * delta).astype(F32) * LOG2E
                for j in pairs:
                    slope = jnp.where(top, _alibi_slope(2 * j), _alibi_slope(2 * j + 1))
                    bias_ref[di, off, j] = jnp.where(valid, -slope * dist, NEG_INF)

    order = tuple(reversed(tuple(enumerate(DILATIONS))))
    for di, d in order:
        n = t_len // d
        kw = min(2 * blk, n)
        nb = n // blk

        def step(it, carry, di=di, d=d, kw=kw, nb=nb):
            res = it // nb
            b = it % nb
            kb = jnp.maximum(b - 1, 0) if kw > blk else b
            q_rows = pl.ds(res + d * blk * b, blk, stride=d)
            k_rows = pl.ds(res + d * blk * kb, kw, stride=d)
            q2 = [q_ref.at[j][q_rows, :] * (HEAD_DIM ** -0.5 * LOG2E) for j in pairs]
            k2 = [k_ref.at[j][k_rows, :] for j in pairs]
            v2 = [v_ref.at[j][k_rows, :] for j in pairs]
            q_st = [jnp.concatenate([jnp.where(lane_top, x, 0.0), jnp.where(lane_top, 0.0, x)], axis=0) for x in q2]
            s = [_dot_nt(x, y) + bias_ref[di, b - kb, j, :, 0:kw] for j, x, y in zip(pairs, q_st, k2)]
            mh = [jnp.max(x, axis=-1, keepdims=True) for x in s]
            pr = [jnp.exp2(x - m) for x, m in zip(s, mh)]
            lh = [jnp.sum(x, axis=-1, keepdims=True) for x in pr]
            ah = [_dot(x, y) for x, y in zip(pr, v2)]

            def unstack(x):
                x = jnp.broadcast_to(x, (2 * blk, LANES))
                return jnp.where(lane_top, x[0:blk], x[blk:])

            results = []
            for j in pairs:
                acc, m, l = unstack(ah[j]), unstack(mh[j]), unstack(lh[j])
                if d != order[0][1]:
                    m_old = m_ref.at[j][q_rows, :]
                    m_new = jnp.maximum(m_old, m)
                    e_old = jnp.exp2(m_old - m_new)
                    e_new = jnp.exp2(m - m_new)
                    acc = acc_ref.at[j][q_rows, :] * e_old + acc * e_new
                    l = l_ref.at[j][q_rows, :] * e_old + l * e_new
                    m = m_new
                results.append((acc, m, l))
            if d != order[-1][1]:
                for j, (acc, m, l) in enumerate(results):
                    acc_ref.at[j][q_rows, :] = acc
                    m_ref.at[j][q_rows, :] = m
                    l_ref.at[j][q_rows, :] = l
            else:
                assert d == 1
                ys = [acc / l for acc, _, l in results]
                ms = sum(jnp.sum(y * y, axis=-1, keepdims=True) for y in ys) * (1.0 / ATT_DIM)
                scale = lax.rsqrt(ms + NORM_EPS)
                rows = pl.ds(pl.multiple_of(blk * b, blk), blk)
                for j in pairs:
                    o_ref[rows, j * LANES:(j + 1) * LANES] = ys[j] * scale * nw_ref[:, j * LANES:(j + 1) * LANES]
            return carry

        lax.fori_loop(0, d * nb, step, 0)


def _att_prompt_call(q3, k3, v3, nw, n_batch, t_len):
    blk3 = pl.BlockSpec((ATT_PAIRS, t_len, LANES), lambda b: (0, b, 0))
    return pl.pallas_call(
        functools.partial(_att_prompt_body, t_len),
        grid=(n_batch,),
        in_specs=[blk3, blk3, blk3, pl.BlockSpec((1, ATT_DIM), lambda b: (0, 0))],
        out_specs=pl.BlockSpec((t_len, ATT_DIM), lambda b: (b, 0)),
        out_shape=jax.ShapeDtypeStruct((n_batch * t_len, ATT_DIM), F32),
        scratch_shapes=[pltpu.VMEM((ATT_PAIRS, t_len, LANES), F32)] * 3
        + [pltpu.VMEM((len(DILATIONS), 2, ATT_PAIRS, 2 * ATT_STEPS, 2 * ATT_STEPS), F32)],
        compiler_params=_params(("arbitrary",)), name="att_prompt")(q3, k3, v3, nw)


def _multiplicity(dist):
    mult = jnp.zeros(dist.shape, F32)
    for d in DILATIONS:
        hit = (_mod(dist, d) == 0) & (dist <= d * ATT_STEPS)
        mult = mult + jnp.where(hit, 1.0, 0.0)
    return mult


def _att_sample_body(t_new, w_buf, q_ref, kn_ref, vn_ref, kt_ref, vt_ref, nw_ref, o_ref):
    RQ = 8
    rows = ATT_HEADS * RQ
    q = q_ref[0] * (HEAD_DIM ** -0.5)
    head_of_row = _div(lax.broadcasted_iota(jnp.int32, (rows, ATT_DIM), 0), RQ)
    head_of_lane = _div(lax.broadcasted_iota(jnp.int32, (rows, ATT_DIM), 1), HEAD_DIM)
    own = head_of_row == head_of_lane
    q_bd = jnp.where(own, jnp.concatenate([q] * ATT_HEADS, axis=0), 0.0)

    s_buf = _dot(q_bd, kt_ref[0, 0])
    s_new = _dot_nt(q_bd, kn_ref[0])

    def weights(s, key_pos):
        t = _mod(lax.broadcasted_iota(jnp.int32, s.shape, 0), RQ)
        h = _div(lax.broadcasted_iota(jnp.int32, s.shape, 0), RQ)
        dist = w_buf + t - key_pos
        slope = sum(jnp.where(h == hh, _alibi_slope(hh), 0.0) for hh in range(ATT_HEADS))
        mult = jnp.where(dist >= 0, _multiplicity(jnp.maximum(dist, 0)), 0.0)
        return jnp.where(mult > 0, s - slope * dist.astype(F32), NEG_INF), mult

    sb, mult_b = weights(s_buf, lax.broadcasted_iota(jnp.int32, s_buf.shape, 1))
    sn, mult_n = weights(s_new, w_buf + lax.broadcasted_iota(jnp.int32, s_new.shape, 1))
    m = jnp.maximum(jnp.max(sb, axis=-1, keepdims=True), jnp.max(sn, axis=-1, keepdims=True))
    pb = mult_b * jnp.exp(sb - m)
    pn = mult_n * jnp.exp(sn - m)
    den = jnp.sum(pb, axis=-1, keepdims=True) + jnp.sum(pn, axis=-1, keepdims=True)
    num = _dot_nt(pb, vt_ref[0, 0]) + _dot(pn, vn_ref[0])
    out = jnp.where(own, num / den, 0.0).reshape(ATT_HEADS, RQ, ATT_DIM)
    o_ref[0] = _rms(jnp.sum(out, axis=0), nw_ref[...])


def _att_sample_call(q, k_new, v_new, cache_kt, cache_vt, layer, nw, t_new):
    depth, n_batch, _, w_buf = cache_kt.shape
    small = pl.BlockSpec((1, 8, ATT_DIM), lambda b: (b, 0, 0))
    big = pl.BlockSpec((1, 1, ATT_DIM, w_buf), lambda b: (layer, b, 0, 0))
    return pl.pallas_call(
        functools.partial(_att_sample_body, t_new, w_buf),
        grid=(n_batch,),
        in_specs=[small, small, small, big, big, pl.BlockSpec((1, ATT_DIM), lambda b: (0, 0))],
        out_specs=small, out_shape=jax.ShapeDtypeStruct((n_batch, 8, ATT_DIM), F32),
        compiler_params=_params(("arbitrary",)), name="att_sample")(q, k_new, v_new, cache_kt, cache_vt, nw)


def _ssd_body(chunk, t_valid, t_total, z_ref, xbc_ref, dt_ref, cp_ref, h0_ref, cw_ref, cb_ref, dtb_ref, alog_ref,
              dskip_ref, nw_ref, y_ref, ho_ref, cv_ref, hs_ref, xc_ref):
    L = chunk
    P = HEAD_DIM
    NS = SSM_STATE
    c = pl.program_id(1)
    nc = pl.num_programs(1)

    @pl.when(c == 0)
    def _():
        hs_ref[...] = h0_ref[0]
        xc_ref[0:8, :] = cp_ref[0]

    xc_ref[8:8 + L, :] = xbc_ref[...]
    conv = cb_ref[...]
    for i in range(SSM_CONV):
        conv = conv + xc_ref[5 + i:5 + i + L, :] * cw_ref[i:i + 1, :]
    last_valid = t_valid - (t_total // L - 1) * L
    conv_tail = xc_ref[last_valid:last_valid + 8, :]
    xc_ref[0:8, :] = xc_ref[L:L + 8, :]
    act = conv * _sigmoid(conv)
    xs = act[:, 0:SSM_DIM]
    bm = act[:, SSM_DIM:SSM_DIM + SSM_GROUPS * NS]
    cm = act[:, SSM_DIM + SSM_GROUPS * NS:]

    dt = _softplus(dt_ref[...] + dtb_ref[...])
    if t_valid < t_total:
        live = (c * L + lax.broadcasted_iota(jnp.int32, dt.shape, 0)) < t_valid
        dt = jnp.where(live, dt, 0.0)
    a_neg = -jnp.exp(alog_ref[...])
    acum = _cumsum_rows(dt * a_neg)
    acum_t = acum.T
    dt_t = dt.T
    li = lax.broadcasted_iota(jnp.int32, (L, L), 0)
    si = lax.broadcasted_iota(jnp.int32, (L, L), 1)
    causal = li >= si
    cb = [_dot_nt(cm[:, g * NS:(g + 1) * NS], bm[:, g * NS:(g + 1) * NS]) for g in range(SSM_GROUPS)]
    rep = SSM_HEADS // SSM_GROUPS
    GW = rep * P
    bd_row = _div(lax.broadcasted_iota(jnp.int32, (rep * L, GW), 0), L)
    bd_lane = _div(lax.broadcasted_iota(jnp.int32, (rep * L, GW), 1), P)

    def lanes_of(cols):
        return jnp.concatenate([jnp.broadcast_to(x, (x.shape[0], P)) for x in cols], axis=1)

    ys = []
    for g in range(SSM_GROUPS):
        heads = range(g * rep, (g + 1) * rep)
        x_g = xs[:, g * GW:(g + 1) * GW]
        b_g = bm[:, g * NS:(g + 1) * NS]
        c_g = cm[:, g * NS:(g + 1) * NS]
        scores = [cb[g] * jnp.exp(jnp.where(causal, acum[:, h:h + 1] - acum_t[h:h + 1, :], NEG_INF)) * dt_t[h:h + 1, :]
                  for h in heads]
        ac = lanes_of([acum[:, h:h + 1] for h in heads])
        a_last = lanes_of([acum[L - 1:L, h:h + 1] for h in heads])
        x_bd = jnp.where(bd_row == bd_lane, jnp.concatenate([x_g] * rep, axis=0), 0.0)
        h_g = hs_ref[g * rep:(g + 1) * rep].reshape(GW, NS)
        y = (_dot(jnp.concatenate(scores, axis=1), x_bd) + _dot_nt(c_g, h_g) * jnp.exp(ac)
             + lanes_of([dskip_ref[:, h:h + 1] for h in heads]) * x_g)
        to_end = jnp.exp(a_last - ac) * lanes_of([dt[:, h:h + 1] for h in heads])
        carry = jnp.concatenate([jnp.broadcast_to(jnp.exp(acum[L - 1:L, h:h + 1]), (P, NS)) for h in heads], axis=0)
        hs_ref[g * rep:(g + 1) * rep] = (h_g * carry + _dot_tn(x_g * to_end, b_g)).reshape(rep, P, NS)
        ys.append(y)
    y = jnp.concatenate(ys, axis=1)
    z = z_ref[...]
    y = y * (z * _sigmoid(z))
    y_ref[...] = _rms(y, nw_ref[...])

    @pl.when(c == nc - 1)
    def _():
        ho_ref[0] = hs_ref[...]
        cv_ref[0] = conv_tail


def _ssd_call(z, xbc, dt, conv_prev8, h0, lp, n_batch, t_total, t_valid, chunk):
    L = chunk
    nc = t_total // L
    rows = lambda w: pl.BlockSpec((L, w), lambda b, c: (b * nc + c, 0))
    vec = lambda a, n: pl.BlockSpec((a, n), lambda b, c: (0, 0))
    in_specs = [rows(SSM_DIM), rows(SSM_CONV_DIM), rows(LANES),
                pl.BlockSpec((1, 8, SSM_CONV_DIM), lambda b, c: (b, 0, 0)),
                pl.BlockSpec((1, SSM_HEADS, HEAD_DIM, SSM_STATE), lambda b, c: (b, 0, 0, 0)),
                vec(SSM_CONV, SSM_CONV_DIM), vec(1, SSM_CONV_DIM), vec(1, LANES), vec(1, LANES), vec(1, LANES),
                vec(1, SSM_DIM)]
    out_shape = (jax.ShapeDtypeStruct((n_batch * t_total, SSM_DIM), F32),
                 jax.ShapeDtypeStruct((n_batch, SSM_HEADS, HEAD_DIM, SSM_STATE), F32),
                 jax.ShapeDtypeStruct((n_batch, 8, SSM_CONV_DIM), F32))
    out_specs = (rows(SSM_DIM), pl.BlockSpec((1, SSM_HEADS, HEAD_DIM, SSM_STATE), lambda b, c: (b, 0, 0, 0)),
                 pl.BlockSpec((1, 8, SSM_CONV_DIM), lambda b, c: (b, 0, 0)))
    return pl.pallas_call(
        functools.partial(_ssd_body, L, t_valid, t_total),
        grid=(n_batch, nc), in_specs=in_specs, out_specs=out_specs, out_shape=out_shape,
        scratch_shapes=[pltpu.VMEM((SSM_HEADS, HEAD_DIM, SSM_STATE), F32), pltpu.VMEM((L + 8, SSM_CONV_DIM), F32)],
        compiler_params=_params(("arbitrary", "arbitrary")), name="ssd")(
            z, xbc, dt, conv_prev8, h0, lp["ssm_conv_w"], lp["ssm_conv_b"], lp["ssm_dt_bias"], lp["ssm_A_log"],
            lp["ssm_D"], lp["ssm_norm_w"])


def _pad_lanes(v, n):
    return jnp.pad(v, ((0, 0), (0, n - v.shape[-1])))


def _pad_rows(x, n):
    return jnp.pad(x, ((0, n - x.shape[0]),) + ((0, 0),) * (x.ndim - 1))


def kernel(x_prompt, x_sample, cache_att_k, cache_att_v, state_rwkv_shift, state_rwkv, state_ssm_conv, state_ssm, ffn1_norm, ffn1_w_gate, ffn1_w_up, ffn1_w_down, mix_norm, w_in, rwkv_mu, rwkv_w0, rwkv_w2, rwkv_a0, rwkv_a2, rwkv_g2, rwkv_k_k, rwkv_k_a, rwkv_r_k, rwkv_ln_w, rwkv_ln_b, att_norm_w, ssm_conv_w, ssm_conv_b, ssm_dt_bias, ssm_A_log, ssm_D, ssm_norm_w, w_out, ffn2_norm, ffn2_w_gate, ffn2_w_up, ffn2_w_down, final_norm):
    depth = w_in.shape[0]
    bp, tp, _ = x_prompt.shape
    bs, ts, _ = x_sample.shape
    w_buf = cache_att_k.shape[2]
    mp = bp * tp
    ms = bs * ts
    assert mp % ROW_TILE == 0 and ms <= ROW_TILE and tp % (ATT_STEPS * DILATIONS[-1]) == 0 and ts <= 8
    n_ptiles = mp // ROW_TILE
    ts_rwkv = RWKV_CHUNK
    ts_ssd = 8

    x = jnp.concatenate([x_prompt.reshape(mp, D_MODEL), _pad_rows(x_sample.reshape(ms, D_MODEL), ROW_TILE)], axis=0)
    zeros_shift = jnp.zeros((bp, RWKV_PROJ), F32)
    zeros_s = jnp.zeros((bp, RWKV_HEADS, HEAD_DIM, HEAD_DIM), F32)
    zeros_conv = jnp.zeros((bp, 8, SSM_CONV_DIM), F32)
    zeros_h = jnp.zeros((bp, SSM_HEADS, HEAD_DIM, SSM_STATE), F32)
    to_cp = lambda c: jnp.transpose(c, (0, 1, 3, 4, 2)).reshape(depth, bs, ATT_DIM, w_buf)
    cache_kt, cache_vt = to_cp(cache_att_k), to_cp(cache_att_v)

    outs = {k: [] for k in ("p_k", "p_v", "p_shift", "p_s", "p_conv", "p_h",
                            "s_k", "s_v", "s_shift", "s_s", "s_conv", "s_h")}
    mix = None
    wo = None
    y_prompt = y_sample = None
    for l in range(depth):
        row1 = lambda v: v[l].reshape(1, -1)
        lp = dict(rwkv_mu=row1(rwkv_mu), rwkv_w0=row1(rwkv_w0), rwkv_w2=rwkv_w2[l], rwkv_a0=row1(rwkv_a0),
                  rwkv_a2=rwkv_a2[l], rwkv_g2=rwkv_g2[l], rwkv_k_k=row1(rwkv_k_k), rwkv_k_a=row1(rwkv_k_a),
                  rwkv_r_k=row1(rwkv_r_k), rwkv_ln_w=row1(rwkv_ln_w), rwkv_ln_b=row1(rwkv_ln_b),
                  ssm_conv_w=ssm_conv_w[l], ssm_conv_b=row1(ssm_conv_b),
                  ssm_dt_bias=_pad_lanes(row1(ssm_dt_bias), LANES), ssm_A_log=_pad_lanes(row1(ssm_A_log), LANES),
                  ssm_D=_pad_lanes(row1(ssm_D), LANES), ssm_norm_w=row1(ssm_norm_w))
        bf = lambda w: w[l].astype(BF16)
        if l == 0:
            x = _ffn_call(x, None, None, row1(ffn1_norm), bf(ffn1_w_gate), bf(ffn1_w_up), bf(ffn1_w_down), None,
                          n_ptiles)
        (p_rwkv, q3, k3, v3, k_cp, v_cp, z, xbc, dt, pr_t, q_s, k_s, v_s, z_t, xbc_t, dt_t) = _proj_call(
            x, row1(mix_norm), _pad_lanes(w_in[l], PROJ_PAD).astype(BF16), n_ptiles, tp)

        ya_p, s_p, last_p = _rwkv_call(p_rwkv, zeros_shift, zeros_s, lp, bp, tp, tp)
        yb_p = _att_prompt_call(q3, k3, v3, row1(att_norm_w), bp, tp)
        yc_p, h_p, conv_p = _ssd_call(z, xbc, dt, zeros_conv, zeros_h, lp, bp, tp, tp, SSD_CHUNK)

        def sample_rows(a, t_pad):
            return jnp.pad(a[:ms].reshape(bs, ts, -1), ((0, 0), (0, t_pad - ts), (0, 0)))

        pr_s = sample_rows(pr_t, ts_rwkv)
        ya_s, s_s, last_s = _rwkv_call(pr_s.reshape(bs * ts_rwkv, RWKV_PROJ), state_rwkv_shift[l], state_rwkv[l], lp,
                               bs, ts_rwkv, ts)
        yb_s = _att_sample_call(sample_rows(q_s, 8), sample_rows(k_s, 8), sample_rows(v_s, 8),
                                cache_kt, cache_vt, l, row1(att_norm_w), ts)
        conv8 = jnp.pad(state_ssm_conv[l], ((0, 0), (8 - (SSM_CONV - 1), 0), (0, 0)))
        flat8 = lambda a: sample_rows(a, ts_ssd).reshape(bs * ts_ssd, -1)
        yc_s, h_s, conv_s = _ssd_call(flat8(z_t), flat8(xbc_t), flat8(dt_t), conv8, state_ssm[l], lp, bs, ts_ssd, ts,
                                      ts_ssd)

        def tile_rows(a, t_pad):
            return _pad_rows(a.reshape(bs, t_pad, -1)[:, :ts].reshape(ms, -1), ROW_TILE)

        mix = ((ya_p, tile_rows(ya_s, ts_rwkv)), (yb_p, tile_rows(yb_s, 8)), (yc_p, tile_rows(yc_s, ts_ssd)))
        wo = w_out[l].astype(BF16)

        keep = min(2048, tp)
        from_cp = lambda a: jnp.transpose(a.reshape(bp, ATT_HEADS, HEAD_DIM, tp), (0, 3, 1, 2))[:, tp - keep:]
        outs["p_k"].append(from_cp(k_cp))
        outs["p_v"].append(from_cp(v_cp))
        outs["p_shift"].append(last_p[:, 0])
        outs["p_s"].append(s_p)
        outs["p_conv"].append(conv_p[:, 8 - (SSM_CONV - 1):])
        outs["p_h"].append(h_p)
        outs["s_k"].append(k_s[:ms].reshape(bs, ts, ATT_HEADS, HEAD_DIM))
        outs["s_v"].append(v_s[:ms].reshape(bs, ts, ATT_HEADS, HEAD_DIM))
        outs["s_shift"].append(last_s[:, 0])
        outs["s_s"].append(s_s)
        outs["s_conv"].append(conv_s[:, 8 - (SSM_CONV - 1):])
        outs["s_h"].append(h_s)

        x = _ffn_call(x, mix, wo, row1(ffn2_norm), bf(ffn2_w_gate), bf(ffn2_w_up), bf(ffn2_w_down),
                      final_norm.reshape(1, -1) if l == depth - 1 else None, n_ptiles)
        if l == depth - 1:
            y_prompt, y_sample = x
        else:
            lnext = l + 1
            x = _ffn_call(x, None, None, ffn1_norm[lnext].reshape(1, -1), ffn1_w_gate[lnext].astype(BF16),
                          ffn1_w_up[lnext].astype(BF16), ffn1_w_down[lnext].astype(BF16), None, n_ptiles)

    st = lambda k: jnp.stack(outs[k])
    return (y_prompt.reshape(bp, tp, D_MODEL), y_sample[:ms].reshape(bs, ts, D_MODEL),
            st("p_k"), st("p_v"), st("p_shift"), st("p_s"), st("p_conv"), st("p_h"),
            st("s_k"), st("s_v"), st("s_shift"), st("s_s"), st("s_conv"), st("s_h"))
```

```python
import functools

import jax
import jax.numpy as jnp
from jax import lax
from jax.experimental import pallas as pl
from jax.experimental.pallas import tpu as pltpu

F32 = jnp.float32
BF16 = jnp.bfloat16

D_MODEL = 1024
FFN_DIM = 2816
HEAD_DIM = 64
RWKV_HEADS = 4
RWKV_DIM = RWKV_HEADS * HEAD_DIM
RWKV_PROJ = 1024
RWKV_GN_EPS = 64e-5
ATT_HEADS = 6
ATT_DIM = ATT_HEADS * HEAD_DIM
ATT_PAIRS = ATT_HEADS // 2
DILATIONS = (1, 4, 16)
ATT_STEPS = 128
SSM_HEADS = 6
SSM_DIM = SSM_HEADS * HEAD_DIM
SSM_GROUPS = 2
SSM_STATE = 128
SSM_CONV = 4
SSM_CONV_DIM = SSM_DIM + 2 * SSM_GROUPS * SSM_STATE
PROJ_DIM = RWKV_PROJ + 3 * ATT_DIM + SSM_DIM + SSM_CONV_DIM + SSM_HEADS
PROJ_PAD = 3584
NORM_EPS = 1e-6
NEG_INF = -1e30
LOG2E = 1.4426950408889634

LANES = 128
ROW_TILE = 512
FFN_CHUNK = 256
RWKV_CHUNK = 64
RWKV_GROUP = 8
SSD_CHUNK = 128
SSD_GROUP = 4
VMEM_LIMIT = 56 * 1024 * 1024

OFF_ATT = RWKV_PROJ
OFF_Z = OFF_ATT + 3 * ATT_DIM
OFF_XBC = OFF_Z + SSM_DIM
OFF_DT = OFF_XBC + SSM_CONV_DIM


def _params(sem):
    return pltpu.CompilerParams(dimension_semantics=sem, vmem_limit_bytes=VMEM_LIMIT)


def _dot(a, b):
    return jnp.dot(a.astype(BF16), b.astype(BF16), preferred_element_type=F32)


def _dot_nt(a, b):
    return lax.dot_general(a.astype(BF16), b.astype(BF16), (((1,), (1,)), ((), ())),
                           preferred_element_type=F32)


def _dot_tn(a, b):
    return lax.dot_general(a.astype(BF16), b.astype(BF16), (((0,), (0,)), ((), ())),
                           preferred_element_type=F32)


def _dot_hi(a, b):
    return jnp.dot(a, b, preferred_element_type=F32, precision=lax.Precision.HIGHEST)


def _rms(x, w):
    return x * lax.rsqrt(jnp.mean(x * x, axis=-1, keepdims=True) + NORM_EPS) * w


def _sigmoid(x):
    return 1.0 / (1.0 + jnp.exp(-x))


def _softplus(x):
    return jnp.maximum(x, 0.0) + jnp.log(1.0 + jnp.exp(-jnp.abs(x)))


def _div(x, n):
    assert n & (n - 1) == 0
    return lax.shift_right_logical(x, n.bit_length() - 1)


def _mod(x, n):
    assert n & (n - 1) == 0
    return x & (n - 1)


def _cumsum_rows(x):
    n = x.shape[0]
    row = lax.broadcasted_iota(jnp.int32, x.shape, 0)
    s = 1
    while s < n:
        x = x + jnp.where(row >= s, pltpu.roll(x, s, axis=0), 0.0)
        s *= 2
    return x


def _ffn_body(has_mix, final, n_prompt_tiles, *refs):
    it = iter(refs)
    x_ref = next(it)
    if has_mix:
        yap, yas, ybp, ybs, ycp, ycs, wo_ref = (next(it) for _ in range(7))
    nw_ref, wg_ref, wu_ref, wd_ref = (next(it) for _ in range(4))
    if final:
        fn_ref = next(it)
        op_ref, os_ref = next(it), next(it)
    else:
        o_ref = next(it)
    h_ref, acc_ref = next(it), next(it)
    i = pl.program_id(0)
    is_sample = i >= n_prompt_tiles

    x = x_ref[...]
    if has_mix:
        ya = jnp.where(is_sample, yas[...], yap[...])
        yb = jnp.where(is_sample, ybs[...], ybp[...])
        yc = jnp.where(is_sample, ycs[...], ycp[...])
        x = (x + _dot(ya, wo_ref[0:RWKV_DIM, :])
             + _dot(yb, wo_ref[RWKV_DIM:RWKV_DIM + ATT_DIM, :])
             + _dot(yc, wo_ref[RWKV_DIM + ATT_DIM:, :]))
    h_ref[...] = _rms(x, nw_ref[...]).astype(BF16)
    acc_ref[...] = x

    def chunk(c, carry):
        off = pl.multiple_of(c * FFN_CHUNK, FFN_CHUNK)
        h = h_ref[...]
        g = jnp.dot(h, wg_ref[:, pl.ds(off, FFN_CHUNK)], preferred_element_type=F32)
        u = jnp.dot(h, wu_ref[:, pl.ds(off, FFN_CHUNK)], preferred_element_type=F32)
        a = (g * _sigmoid(g) * u).astype(BF16)
        acc_ref[...] += 0.5 * jnp.dot(a, wd_ref[pl.ds(off, FFN_CHUNK), :], preferred_element_type=F32)
        return carry

    lax.fori_loop(0, FFN_DIM // FFN_CHUNK, chunk, 0)
    if final:
        y = _rms(acc_ref[...], fn_ref[...])

        @pl.when(jnp.logical_not(is_sample))
        def _():
            op_ref[...] = y

        @pl.when(is_sample)
        def _():
            os_ref[...] = y
    else:
        o_ref[...] = acc_ref[...]


def _ffn_call(x, mix, wo, nw, wg, wu, wd, final_norm, n_prompt_tiles):
    m_pad = x.shape[0]
    tm = ROW_TILE
    has_mix = mix is not None
    final = final_norm is not None
    last_p = n_prompt_tiles - 1
    row = lambda i: (i, 0)
    fixed = lambda i: (0, 0)
    prow = lambda i: (jnp.minimum(i, last_p), 0)
    resident = functools.partial(pl.BlockSpec, index_map=fixed, pipeline_mode=pl.Buffered(1))
    in_specs = [pl.BlockSpec((tm, D_MODEL), row)]
    args = [x]
    if has_mix:
        for yp, ys in mix:
            w = yp.shape[1]
            in_specs += [pl.BlockSpec((tm, w), prow), pl.BlockSpec((tm, w), fixed)]
            args += [yp, ys]
        in_specs.append(resident((D_MODEL, D_MODEL)))
        args.append(wo)
    in_specs += [pl.BlockSpec((1, D_MODEL), fixed), resident((D_MODEL, FFN_DIM)),
                 resident((D_MODEL, FFN_DIM)), resident((FFN_DIM, D_MODEL))]
    args += [nw, wg, wu, wd]
    if final:
        in_specs.append(pl.BlockSpec((1, D_MODEL), fixed))
        args.append(final_norm)
        out_shape = (jax.ShapeDtypeStruct((n_prompt_tiles * tm, D_MODEL), F32),
                     jax.ShapeDtypeStruct((tm, D_MODEL), F32))
        out_specs = (pl.BlockSpec((tm, D_MODEL), prow), pl.BlockSpec((tm, D_MODEL), fixed))
    else:
        out_shape = jax.ShapeDtypeStruct((m_pad, D_MODEL), F32)
        out_specs = pl.BlockSpec((tm, D_MODEL), row)
    return pl.pallas_call(
        functools.partial(_ffn_body, has_mix, final, n_prompt_tiles),
        grid=(m_pad // tm,), in_specs=in_specs, out_specs=out_specs, out_shape=out_shape,
        scratch_shapes=[pltpu.VMEM((tm, D_MODEL), BF16), pltpu.VMEM((tm, D_MODEL), F32)],
        compiler_params=_params(("arbitrary",)), name="ffn")(*args)


def _proj_body(x_ref, nw_ref, w_ref, pr_ref, q3_ref, k3_ref, v3_ref, kn_ref, vn_ref, z_ref, xbc_ref,
               dt_ref, prs_ref, qs_ref, ks_ref, vs_ref, zs_ref, xbcs_ref, dts_ref):
    h = _rms(x_ref[...], nw_ref[...]).astype(BF16)

    def seg(a, b):
        return jnp.dot(h, w_ref[:, a:b], preferred_element_type=F32)

    pr = seg(0, OFF_ATT)
    q = seg(OFF_ATT, OFF_ATT + ATT_DIM)
    k = seg(OFF_ATT + ATT_DIM, OFF_ATT + 2 * ATT_DIM)
    v = seg(OFF_ATT + 2 * ATT_DIM, OFF_Z)
    z = seg(OFF_Z, OFF_XBC)
    xbc = seg(OFF_XBC, OFF_DT)
    dt = seg(OFF_DT, PROJ_PAD)
    pr_ref[...] = pr
    z_ref[...] = z
    xbc_ref[...] = xbc
    dt_ref[...] = dt
    kn_ref[0] = k.T
    vn_ref[0] = v.T
    for j in range(ATT_PAIRS):
        q3_ref[j] = q[:, j * LANES:(j + 1) * LANES]
        k3_ref[j] = k[:, j * LANES:(j + 1) * LANES]
        v3_ref[j] = v[:, j * LANES:(j + 1) * LANES]
    prs_ref[...] = pr
    qs_ref[...] = q
    ks_ref[...] = k
    vs_ref[...] = v
    zs_ref[...] = z
    xbcs_ref[...] = xbc
    dts_ref[...] = dt


def _proj_call(x, nw, w, n_prompt_tiles, t_prompt):
    m_pad = x.shape[0]
    tm = ROW_TILE
    mp = n_prompt_tiles * tm
    tiles_per_seq = t_prompt // tm
    assert m_pad == mp + tm
    row = lambda i: (i, 0)
    fixed = lambda i: (0, 0)
    row3 = lambda i: (0, i, 0)
    col = lambda i: (i // tiles_per_seq, 0, i % tiles_per_seq)
    sds = jax.ShapeDtypeStruct
    widths = (RWKV_PROJ, ATT_DIM, ATT_DIM, ATT_DIM, SSM_DIM, SSM_CONV_DIM, LANES)
    n_seq = mp // t_prompt + 1
    out_shape = ((sds((m_pad, RWKV_PROJ), F32),
                  sds((ATT_PAIRS, m_pad, LANES), F32), sds((ATT_PAIRS, m_pad, LANES), F32),
                  sds((ATT_PAIRS, m_pad, LANES), F32),
                  sds((n_seq, ATT_DIM, t_prompt), F32), sds((n_seq, ATT_DIM, t_prompt), F32),
                  sds((m_pad, SSM_DIM), F32), sds((m_pad, SSM_CONV_DIM), F32), sds((m_pad, LANES), F32))
                 + tuple(sds((tm, w_), F32) for w_ in widths))
    p3 = pl.BlockSpec((ATT_PAIRS, tm, LANES), row3)
    out_specs = ((pl.BlockSpec((tm, RWKV_PROJ), row), p3, p3, p3,
                  pl.BlockSpec((1, ATT_DIM, tm), col), pl.BlockSpec((1, ATT_DIM, tm), col),
                  pl.BlockSpec((tm, SSM_DIM), row), pl.BlockSpec((tm, SSM_CONV_DIM), row),
                  pl.BlockSpec((tm, LANES), row))
                 + tuple(pl.BlockSpec((tm, w_), fixed) for w_ in widths))
    return pl.pallas_call(
        _proj_body,
        grid=(m_pad // tm,),
        in_specs=[pl.BlockSpec((tm, D_MODEL), row), pl.BlockSpec((1, D_MODEL), fixed),
                  pl.BlockSpec((D_MODEL, PROJ_PAD), fixed, pipeline_mode=pl.Buffered(1))],
        out_specs=out_specs, out_shape=out_shape,
        compiler_params=_params(("arbitrary",)), name="proj")(x, nw, w)


def _rwkv_body(chunk, group, by_batch, t_valid, t_total, p_ref, sh_ref, s0_ref, mu_ref, w0_ref, w2_ref, a0_ref, a2_ref,
               g2_ref, kk_ref, ka_ref, rk_ref, lnw_ref, lnb_ref, y_ref, so_ref, last_ref, sb_ref, prev_ref):
    C = chunk
    G = group
    H = RWKV_HEADS
    N = HEAD_DIM
    R = RWKV_DIM
    GC = G * C
    c = pl.program_id(1)
    nc = pl.num_programs(1)
    p = p_ref[...]
    row_p = lax.broadcasted_iota(jnp.int32, p.shape, 0)
    p_prev = pltpu.roll(p, 1, axis=0)

    if by_batch:
        for gi in range(G):
            p_prev = jnp.where(row_p == gi * C, sh_ref[gi], p_prev)
        zero = jnp.zeros((N, N), F32)
        states = [jnp.concatenate([jnp.concatenate([s0_ref[gi, h] if hh == h else zero for hh in range(H)], axis=1)
                                   for h in range(H)], axis=0) for gi in range(G)]
    else:
        @pl.when(c == 0)
        def _():
            sb_ref[...] = jnp.zeros((R, R), F32)
            for h in range(H):
                sb_ref[h * N:(h + 1) * N, h * N:(h + 1) * N] = s0_ref[0, h]
            prev_ref[...] = jnp.broadcast_to(sh_ref[0], (8, RWKV_PROJ))

        p_prev = jnp.where(row_p == 0, prev_ref[0:1, :], p_prev)
        prev_ref[...] = jnp.broadcast_to(p[GC - 1:GC, :], (8, RWKV_PROJ))
    ps = p + mu_ref[...] * (p_prev - p)
    r = ps[:, 0:R]
    k = ps[:, R:2 * R]
    v = ps[:, 2 * R:3 * R]
    w_lr = ps[:, 3 * R:3 * R + 64]
    a_lr = ps[:, 3 * R + 64:3 * R + 128]
    g_lr = ps[:, 3 * R + 128:]

    w_log = -_softplus(-(w0_ref[...] + _dot(jnp.tanh(w_lr), w2_ref[...]))) - 0.5
    ld = -jnp.exp(w_log)
    a = _sigmoid(a0_ref[...] + _dot(a_lr, a2_ref[...]))
    g = _dot(_sigmoid(g_lr), g2_ref[...])

    lane_r = lax.broadcasted_iota(jnp.int32, (R, R), 1)
    row_r = lax.broadcasted_iota(jnp.int32, (R, R), 0)
    own_head = _div(lane_r, N) == _div(row_r, N)
    head_ones = jnp.where(own_head, 1.0, 0.0).astype(BF16)

    def head_sum(x):
        hi = x.astype(BF16)
        lo = (x - hi.astype(F32)).astype(BF16)
        return (jnp.dot(hi, head_ones, preferred_element_type=F32)
                + jnp.dot(lo, head_ones, preferred_element_type=F32))

    kk = k * kk_ref[...]
    kk = kk / jnp.maximum(jnp.sqrt(head_sum(kk * kk)), 1e-12)
    kp = k * (1.0 + (a - 1.0) * ka_ref[...])
    if t_valid < t_total:
        row_r2 = lax.broadcasted_iota(jnp.int32, (GC, R), 0)
        live = (_mod(row_r2, C) if by_batch else c * GC + row_r2) < t_valid
        ld = jnp.where(live, ld, 0.0)
        kk = jnp.where(live, kk, 0.0)
        kp = jnp.where(live, kp, 0.0)
        v = jnp.where(live, v, 0.0)

    HC = H * C
    bd_row = _div(lax.broadcasted_iota(jnp.int32, (HC, R), 0), C)
    bd_lane = _div(lax.broadcasted_iota(jnp.int32, (HC, R), 1), N)

    def bd_rows(x):
        return jnp.where(bd_row == bd_lane, jnp.concatenate([x] * H, axis=0), 0.0)

    sq_row = _div(lax.broadcasted_iota(jnp.int32, (HC, HC), 0), C)
    sq_col = _div(lax.broadcasted_iota(jnp.int32, (HC, HC), 1), C)

    def mm(x, y):
        return _dot(x, jnp.where(sq_row == sq_col, jnp.concatenate([y] * H, axis=0), 0.0))

    t_idx = lax.broadcasted_iota(jnp.int32, (C, HC), 0)
    s_idx = _mod(lax.broadcasted_iota(jnp.int32, (C, HC), 1), C)
    strict = s_idx < t_idx
    incl = s_idx <= t_idx
    eye = jnp.where(s_idx == t_idx, 1.0, 0.0).astype(F32)
    same16 = _div(s_idx, 16) == _div(t_idx, 16)

    chunks = [slice(gi * C, (gi + 1) * C) for gi in range(G)]
    cum = [_cumsum_rows(ld[rows]) for rows in chunks]
    w_inc = [jnp.exp(x) for x in cum]
    w_inv = [jnp.exp(-x) for x in cum]
    w_end = [x[C - 1:C, :] for x in w_inc]
    b_t = [kk[rows] * a[rows] * wi for rows, wi in zip(chunks, w_inv)]
    k_t = [kp[rows] * wi for rows, wi in zip(chunks, w_inv)]
    lhs = [jnp.concatenate([kk[rows] * jnp.exp(cm - ld[rows]), r[rows] * wi], axis=0)
           for rows, cm, wi in zip(chunks, cum, w_inc)]
    sc = [_dot_nt(x, jnp.concatenate([bd_rows(bt), bd_rows(kt)], axis=0)) for x, bt, kt in zip(lhs, b_t, k_t)]
    a_ab = [jnp.where(strict, x[0:C, 0:HC], 0.0) for x in sc]
    a_ak = [jnp.where(strict, x[0:C, HC:], 0.0) for x in sc]
    p_rb = [jnp.where(incl, x[C:, 0:HC], 0.0) for x in sc]
    p_rk = [jnp.where(incl, x[C:, HC:], 0.0) for x in sc]
    e1 = [jnp.where(same16, -x, 0.0) for x in a_ab]
    n_low = [x + y for x, y in zip(a_ab, e1)]
    e2 = [mm(x, x) for x in e1]
    st = [mm(jnp.concatenate([eye + x, y], axis=0), y) for x, y in zip(e1, e2)]
    p4 = [eye + x + y[0:C] for x, y in zip(e1, st)]
    e4 = [y[C:] for y in st]
    st = [mm(jnp.concatenate([x, y], axis=0), y) for x, y in zip(p4, e4)]
    p8 = [x + y[0:C] for x, y in zip(p4, st)]
    e8 = [y[C:] for y in st]
    d_inv = [x + mm(x, y) for x, y in zip(p8, e8)]
    f1 = [-mm(x, y) for x, y in zip(d_inv, n_low)]
    f2 = [mm(x, x) for x in f1]
    q3 = [eye + x + mm(eye + x, y) for x, y in zip(f1, f2)]
    t_inv = [mm(x, y) for x, y in zip(q3, d_inv)]
    v_c = [v[rows] for rows in chunks]
    v_bd = [bd_rows(x) for x in v_c]
    avp = [_dot(jnp.concatenate([x, y], axis=0), z) for x, y, z in zip(a_ak, p_rk, v_bd)]
    tail = [jnp.concatenate([bt * we, kt * we], axis=0) for bt, kt, we in zip(b_t, k_t, w_end)]

    def chain(gi, sb):
        ks_rs = _dot_nt(lhs[gi], sb)
        u = _dot(t_inv[gi], bd_rows(-ks_rs[0:C] - avp[gi][0:C]))
        upd = _dot_tn(jnp.concatenate([u, v_c[gi]], axis=0), tail[gi])
        y_c = ks_rs[C:] + _dot(p_rb[gi], bd_rows(u)) + avp[gi][C:]
        return jnp.where(own_head, sb * w_end[gi] + upd, 0.0), y_c

    ys = []
    if by_batch:
        for gi in range(G):
            sb, y_c = chain(gi, states[gi])
            ys.append(y_c)
            for h in range(H):
                so_ref[gi, h] = sb[h * N:(h + 1) * N, h * N:(h + 1) * N]
    else:
        sb = sb_ref[...]
        for gi in range(G):
            sb, y_c = chain(gi, sb)
            ys.append(y_c)
        sb_ref[...] = sb
    y = jnp.concatenate(ys, axis=0) if G > 1 else ys[0]

    yc = y - head_sum(y) * (1.0 / N)
    var = head_sum(yc * yc) * (1.0 / N)
    yn = yc * lax.rsqrt(var + RWKV_GN_EPS) * lnw_ref[...] + lnb_ref[...]
    bonus = head_sum(r * kp * rk_ref[...]) * v
    y_ref[...] = (yn + bonus) * g
    if by_batch:
        for gi in range(G):
            last_ref[gi] = jnp.broadcast_to(p[gi * C + t_valid - 1:gi * C + t_valid, :], (8, RWKV_PROJ))
    else:
        @pl.when(c == nc - 1)
        def _():
            last_ref[0] = prev_ref[...]
            for h in range(H):
                so_ref[0, h] = sb_ref[h * N:(h + 1) * N, h * N:(h + 1) * N]


def _rwkv_call(p, shift_prev, s0, lp, n_batch, t_total, t_valid):
    C = RWKV_CHUNK
    by_batch = t_total == C
    G = RWKV_GROUP
    assert (n_batch if by_batch else t_total // C) % G == 0
    nc = 1 if by_batch else t_total // (G * C)
    n_outer = n_batch // G if by_batch else n_batch
    gb = G if by_batch else 1
    vec = lambda n: pl.BlockSpec((1, n), lambda b, c: (0, 0))
    mat = lambda a, n: pl.BlockSpec((a, n), lambda b, c: (0, 0))
    in_specs = [pl.BlockSpec((G * C, RWKV_PROJ), lambda b, c: (b * nc + c, 0)),
                pl.BlockSpec((gb, 1, RWKV_PROJ), lambda b, c: (b, 0, 0)),
                pl.BlockSpec((gb, RWKV_HEADS, HEAD_DIM, HEAD_DIM), lambda b, c: (b, 0, 0, 0)),
                vec(RWKV_PROJ), vec(RWKV_DIM), mat(64, RWKV_DIM), vec(RWKV_DIM), mat(64, RWKV_DIM),
                mat(128, RWKV_DIM), vec(RWKV_DIM), vec(RWKV_DIM), vec(RWKV_DIM), vec(RWKV_DIM), vec(RWKV_DIM)]
    out_shape = (jax.ShapeDtypeStruct((n_batch * t_total, RWKV_DIM), F32),
                 jax.ShapeDtypeStruct((n_batch, RWKV_HEADS, HEAD_DIM, HEAD_DIM), F32),
                 jax.ShapeDtypeStruct((n_batch, 8, RWKV_PROJ), F32))
    out_specs = (pl.BlockSpec((G * C, RWKV_DIM), lambda b, c: (b * nc + c, 0)),
                 pl.BlockSpec((gb, RWKV_HEADS, HEAD_DIM, HEAD_DIM), lambda b, c: (b, 0, 0, 0)),
                 pl.BlockSpec((gb, 8, RWKV_PROJ), lambda b, c: (b, 0, 0)))
    return pl.pallas_call(
        functools.partial(_rwkv_body, C, G, by_batch, t_valid, t_total),
        grid=(n_outer, nc), in_specs=in_specs, out_specs=out_specs, out_shape=out_shape,
        scratch_shapes=[pltpu.VMEM((RWKV_DIM, RWKV_DIM), F32), pltpu.VMEM((8, RWKV_PROJ), F32)],
        compiler_params=_params(("arbitrary", "arbitrary")), name="rwkv")(
            p, shift_prev.reshape(n_batch, 1, RWKV_PROJ), s0,
            lp["rwkv_mu"], lp["rwkv_w0"], lp["rwkv_w2"], lp["rwkv_a0"], lp["rwkv_a2"], lp["rwkv_g2"],
            lp["rwkv_k_k"], lp["rwkv_k_a"], lp["rwkv_r_k"], lp["rwkv_ln_w"], lp["rwkv_ln_b"])


def _alibi_slope(h):
    return 2.0 ** (-8.0 * (h + 1) / ATT_HEADS)


def _att_prompt_body(t_len, q_ref, k_ref, v_ref, nw_ref, o_ref, acc_ref, m_ref, l_ref, bias_ref):
    blk = ATT_STEPS
    pairs = range(ATT_PAIRS)
    lane_top = lax.broadcasted_iota(jnp.int32, (blk, LANES), 1) < HEAD_DIM

    @pl.when(pl.program_id(0) == 0)
    def _():
        qi = _mod(lax.broadcasted_iota(jnp.int32, (2 * blk, 2 * blk), 0), blk)
        kj = lax.broadcasted_iota(jnp.int32, (2 * blk, 2 * blk), 1)
        top = lax.broadcasted_iota(jnp.int32, (2 * blk, 2 * blk), 0) < blk
        for di, d in enumerate(DILATIONS):
            for off in range(2):
                delta = off * blk + qi - kj
                valid = (delta >= 0) & (delta <= blk)
                dist = (d * delta).astype(F32) * LOG2E
                for j in pairs:
                    slope = jnp.where(top, _alibi_slope(2 * j), _alibi_slope(2 * j + 1))
                    bias_ref[di, off, j] = jnp.where(valid, -slope * dist, NEG_INF)

    order = tuple(reversed(tuple(enumerate(DILATIONS))))
    for di, d in order:
        n = t_len // d
        kw = min(2 * blk, n)
        nb = n // blk

        def step(it, carry, di=di, d=d, kw=kw, nb=nb):
            res = it // nb
            b = it % nb
            kb = jnp.maximum(b - 1, 0) if kw > blk else b
            q_rows = pl.ds(res + d * blk * b, blk, stride=d)
            k_rows = pl.ds(res + d * blk * kb, kw, stride=d)
            q2 = [q_ref.at[j][q_rows, :] * (HEAD_DIM ** -0.5 * LOG2E) for j in pairs]
            k2 = [k_ref.at[j][k_rows, :] for j in pairs]
            v2 = [v_ref.at[j][k_rows, :] for j in pairs]
            q_st = [jnp.concatenate([jnp.where(lane_top, x, 0.0), jnp.where(lane_top, 0.0, x)], axis=0) for x in q2]
            s = [_dot_nt(x, y) + bias_ref[di, b - kb, j, :, 0:kw] for j, x, y in zip(pairs, q_st, k2)]
            mh = [jnp.max(x, axis=-1, keepdims=True) for x in s]
            pr = [jnp.exp2(x - m) for x, m in zip(s, mh)]
            lh = [jnp.sum(x, axis=-1, keepdims=True) for x in pr]
            ah = [_dot(x, y) for x, y in zip(pr, v2)]

            def unstack(x):
                x = jnp.broadcast_to(x, (2 * blk, LANES))
                return jnp.where(lane_top, x[0:blk], x[blk:])

            results = []
            for j in pairs:
                acc, m, l = unstack(ah[j]), unstack(mh[j]), unstack(lh[j])
                if d != order[0][1]:
                    m_old = m_ref.at[j][q_rows, :]
                    m_new = jnp.maximum(m_old, m)
                    e_old = jnp.exp2(m_old - m_new)
                    e_new = jnp.exp2(m - m_new)
                    acc = acc_ref.at[j][q_rows, :] * e_old + acc * e_new
                    l = l_ref.at[j][q_rows, :] * e_old + l * e_new
                    m = m_new
                results.append((acc, m, l))
            if d != order[-1][1]:
                for j, (acc, m, l) in enumerate(results):
                    acc_ref.at[j][q_rows, :] = acc
                    m_ref.at[j][q_rows, :] = m
                    l_ref.at[j][q_rows, :] = l
            else:
                assert d == 1
                ys = [acc / l for acc, _, l in results]
                ms = sum(jnp.sum(y * y, axis=-1, keepdims=True) for y in ys) * (1.0 / ATT_DIM)
                scale = lax.rsqrt(ms + NORM_EPS)
                rows = pl.ds(pl.multiple_of(blk * b, blk), blk)
                for j in pairs:
                    o_ref[rows, j * LANES:(j + 1) * LANES] = ys[j] * scale * nw_ref[:, j * LANES:(j + 1) * LANES]
            return carry

        lax.fori_loop(0, d * nb, step, 0)


def _att_prompt_call(q3, k3, v3, nw, n_batch, t_len):
    blk3 = pl.BlockSpec((ATT_PAIRS, t_len, LANES), lambda b: (0, b, 0))
    return pl.pallas_call(
        functools.partial(_att_prompt_body, t_len),
        grid=(n_batch,),
        in_specs=[blk3, blk3, blk3, pl.BlockSpec((1, ATT_DIM), lambda b: (0, 0))],
        out_specs=pl.BlockSpec((t_len, ATT_DIM), lambda b: (b, 0)),
        out_shape=jax.ShapeDtypeStruct((n_batch * t_len, ATT_DIM), F32),
        scratch_shapes=[pltpu.VMEM((ATT_PAIRS, t_len, LANES), F32)] * 3
        + [pltpu.VMEM((len(DILATIONS), 2, ATT_PAIRS, 2 * ATT_STEPS, 2 * ATT_STEPS), F32)],
        compiler_params=_params(("arbitrary",)), name="att_prompt")(q3, k3, v3, nw)


def _multiplicity(dist):
    mult = jnp.zeros(dist.shape, F32)
    for d in DILATIONS:
        hit = (_mod(dist, d) == 0) & (dist <= d * ATT_STEPS)
        mult = mult + jnp.where(hit, 1.0, 0.0)
    return mult


def _att_sample_body(t_new, w_buf, q_ref, kn_ref, vn_ref, kt_ref, vt_ref, nw_ref, o_ref):
    RQ = 8
    rows = ATT_HEADS * RQ
    q = q_ref[0] * (HEAD_DIM ** -0.5)
    head_of_row = _div(lax.broadcasted_iota(jnp.int32, (rows, ATT_DIM), 0), RQ)
    head_of_lane = _div(lax.broadcasted_iota(jnp.int32, (rows, ATT_DIM), 1), HEAD_DIM)
    own = head_of_row == head_of_lane
    q_bd = jnp.where(own, jnp.concatenate([q] * ATT_HEADS, axis=0), 0.0)

    s_buf = _dot(q_bd, kt_ref[0, 0])
    s_new = _dot_nt(q_bd, kn_ref[0])

    def weights(s, key_pos):
        t = _mod(lax.broadcasted_iota(jnp.int32, s.shape, 0), RQ)
        h = _div(lax.broadcasted_iota(jnp.int32, s.shape, 0), RQ)
        dist = w_buf + t - key_pos
        slope = sum(jnp.where(h == hh, _alibi_slope(hh), 0.0) for hh in range(ATT_HEADS))
        mult = jnp.where(dist >= 0, _multiplicity(jnp.maximum(dist, 0)), 0.0)
        return jnp.where(mult > 0, s - slope * dist.astype(F32), NEG_INF), mult

    sb, mult_b = weights(s_buf, lax.broadcasted_iota(jnp.int32, s_buf.shape, 1))
    sn, mult_n = weights(s_new, w_buf + lax.broadcasted_iota(jnp.int32, s_new.shape, 1))
    m = jnp.maximum(jnp.max(sb, axis=-1, keepdims=True), jnp.max(sn, axis=-1, keepdims=True))
    pb = mult_b * jnp.exp(sb - m)
    pn = mult_n * jnp.exp(sn - m)
    den = jnp.sum(pb, axis=-1, keepdims=True) + jnp.sum(pn, axis=-1, keepdims=True)
    num = _dot_nt(pb, vt_ref[0, 0]) + _dot(pn, vn_ref[0])
    out = jnp.where(own, num / den, 0.0).reshape(ATT_HEADS, RQ, ATT_DIM)
    o_ref[0] = _rms(jnp.sum(out, axis=0), nw_ref[...])


def _att_sample_call(q, k_new, v_new, cache_kt, cache_vt, layer, nw, t_new):
    depth, n_batch, _, w_buf = cache_kt.shape
    small = pl.BlockSpec((1, 8, ATT_DIM), lambda b: (b, 0, 0))
    big = pl.BlockSpec((1, 1, ATT_DIM, w_buf), lambda b: (layer, b, 0, 0))
    return pl.pallas_call(
        functools.partial(_att_sample_body, t_new, w_buf),
        grid=(n_batch,),
        in_specs=[small, small, small, big, big, pl.BlockSpec((1, ATT_DIM), lambda b: (0, 0))],
        out_specs=small, out_shape=jax.ShapeDtypeStruct((n_batch, 8, ATT_DIM), F32),
        compiler_params=_params(("arbitrary",)), name="att_sample")(q, k_new, v_new, cache_kt, cache_vt, nw)


def _ssd_body(chunk, group, t_valid, t_total, z_ref, xbc_ref, dt_ref, cp_ref, h0_ref, cw_ref, cb_ref, dtb_ref,
              alog_ref, dskip_ref, nw_ref, y_ref, ho_ref, cv_ref, hs_ref, xc_ref):
    L = chunk
    G = group
    GL = G * L
    P = HEAD_DIM
    NS = SSM_STATE
    c = pl.program_id(1)
    nc = pl.num_programs(1)

    @pl.when(c == 0)
    def _():
        hs_ref[...] = h0_ref[0]
        xc_ref[0:8, :] = cp_ref[0]

    xc_ref[8:8 + GL, :] = xbc_ref[...]
    conv = cb_ref[...]
    for i in range(SSM_CONV):
        conv = conv + xc_ref[5 + i:5 + i + GL, :] * cw_ref[i:i + 1, :]
    last_valid = t_valid - (t_total // GL - 1) * GL
    conv_tail = xc_ref[last_valid:last_valid + 8, :]
    xc_ref[0:8, :] = xc_ref[GL:GL + 8, :]
    act = conv * _sigmoid(conv)
    xs = act[:, 0:SSM_DIM]
    bm = act[:, SSM_DIM:SSM_DIM + SSM_GROUPS * NS]
    cm = act[:, SSM_DIM + SSM_GROUPS * NS:]

    dt_all = _softplus(dt_ref[...] + dtb_ref[...])
    if t_valid < t_total:
        live = (c * GL + lax.broadcasted_iota(jnp.int32, dt_all.shape, 0)) < t_valid
        dt_all = jnp.where(live, dt_all, 0.0)
    a_neg = -jnp.exp(alog_ref[...])
    li = lax.broadcasted_iota(jnp.int32, (L, L), 0)
    si = lax.broadcasted_iota(jnp.int32, (L, L), 1)
    causal = li >= si
    rep = SSM_HEADS // SSM_GROUPS
    GW = rep * P
    bd_row = _div(lax.broadcasted_iota(jnp.int32, (rep * L, GW), 0), L)
    bd_lane = _div(lax.broadcasted_iota(jnp.int32, (rep * L, GW), 1), P)

    def lanes_of(cols):
        return jnp.concatenate([jnp.broadcast_to(x, (x.shape[0], P)) for x in cols], axis=1)

    parts = []
    for ci in range(G):
        rows = slice(ci * L, (ci + 1) * L)
        dt = dt_all[rows]
        acum = _cumsum_rows(dt * a_neg)
        acum_t = acum.T
        dt_t = dt.T
        per_group = []
        for g in range(SSM_GROUPS):
            heads = range(g * rep, (g + 1) * rep)
            x_g = xs[rows, g * GW:(g + 1) * GW]
            b_g = bm[rows, g * NS:(g + 1) * NS]
            c_g = cm[rows, g * NS:(g + 1) * NS]
            cb = _dot_nt(c_g, b_g)
            scores = [cb * jnp.exp(jnp.where(causal, acum[:, h:h + 1] - acum_t[h:h + 1, :], NEG_INF)) * dt_t[h:h + 1, :]
                      for h in heads]
            ac = lanes_of([acum[:, h:h + 1] for h in heads])
            a_last = lanes_of([acum[L - 1:L, h:h + 1] for h in heads])
            x_bd = jnp.where(bd_row == bd_lane, jnp.concatenate([x_g] * rep, axis=0), 0.0)
            y_own = _dot(jnp.concatenate(scores, axis=1), x_bd) + lanes_of([dskip_ref[:, h:h + 1] for h in heads]) * x_g
            to_end = jnp.exp(a_last - ac) * lanes_of([dt[:, h:h + 1] for h in heads])
            keep = jnp.concatenate([jnp.broadcast_to(jnp.exp(acum[L - 1:L, h:h + 1]), (P, NS)) for h in heads], axis=0)
            per_group.append((y_own, c_g, jnp.exp(ac), keep, _dot_tn(x_g * to_end, b_g)))
        parts.append(per_group)

    ys = [[None] * SSM_GROUPS for _ in range(G)]
    for g in range(SSM_GROUPS):
        h_g = hs_ref[g * rep:(g + 1) * rep].reshape(GW, NS)
        for ci in range(G):
            y_own, c_g, grow, keep, add = parts[ci][g]
            ys[ci][g] = y_own + _dot_nt(c_g, h_g) * grow
            h_g = h_g * keep + add
        hs_ref[g * rep:(g + 1) * rep] = h_g.reshape(rep, P, NS)
    y = jnp.concatenate([jnp.concatenate(row, axis=1) for row in ys], axis=0)
    z = z_ref[...]
    y = y * (z * _sigmoid(z))
    y_ref[...] = _rms(y, nw_ref[...])

    @pl.when(c == nc - 1)
    def _():
        ho_ref[0] = hs_ref[...]
        cv_ref[0] = conv_tail


def _ssd_call(z, xbc, dt, conv_prev8, h0, lp, n_batch, t_total, t_valid, chunk):
    L = chunk
    G = SSD_GROUP if (t_total // L) % SSD_GROUP == 0 else 1
    nc = t_total // (G * L)
    rows = lambda w: pl.BlockSpec((G * L, w), lambda b, c: (b * nc + c, 0))
    vec = lambda a, n: pl.BlockSpec((a, n), lambda b, c: (0, 0))
    in_specs = [rows(SSM_DIM), rows(SSM_CONV_DIM), rows(LANES),
                pl.BlockSpec((1, 8, SSM_CONV_DIM), lambda b, c: (b, 0, 0)),
                pl.BlockSpec((1, SSM_HEADS, HEAD_DIM, SSM_STATE), lambda b, c: (b, 0, 0, 0)),
                vec(SSM_CONV, SSM_CONV_DIM), vec(1, SSM_CONV_DIM), vec(1, LANES), vec(1, LANES), vec(1, LANES),
                vec(1, SSM_DIM)]
    out_shape = (jax.ShapeDtypeStruct((n_batch * t_total, SSM_DIM), F32),
                 jax.ShapeDtypeStruct((n_batch, SSM_HEADS, HEAD_DIM, SSM_STATE), F32),
                 jax.ShapeDtypeStruct((n_batch, 8, SSM_CONV_DIM), F32))
    out_specs = (rows(SSM_DIM), pl.BlockSpec((1, SSM_HEADS, HEAD_DIM, SSM_STATE), lambda b, c: (b, 0, 0, 0)),
                 pl.BlockSpec((1, 8, SSM_CONV_DIM), lambda b, c: (b, 0, 0)))
    return pl.pallas_call(
        functools.partial(_ssd_body, L, G, t_valid, t_total),
        grid=(n_batch, nc), in_specs=in_specs, out_specs=out_specs, out_shape=out_shape,
        scratch_shapes=[pltpu.VMEM((SSM_HEADS, HEAD_DIM, SSM_STATE), F32),
                        pltpu.VMEM((G * L + 8, SSM_CONV_DIM), F32)],
        compiler_params=_params(("arbitrary", "arbitrary")), name="ssd")(
            z, xbc, dt, conv_prev8, h0, lp["ssm_conv_w"], lp["ssm_conv_b"], lp["ssm_dt_bias"], lp["ssm_A_log"],
            lp["ssm_D"], lp["ssm_norm_w"])


def _pad_lanes(v, n):
    return jnp.pad(v, ((0, 0), (0, n - v.shape[-1])))


def _pad_rows(x, n):
    return jnp.pad(x, ((0, n - x.shape[0]),) + ((0, 0),) * (x.ndim - 1))


def kernel(x_prompt, x_sample, cache_att_k, cache_att_v, state_rwkv_shift, state_rwkv, state_ssm_conv, state_ssm, ffn1_norm, ffn1_w_gate, ffn1_w_up, ffn1_w_down, mix_norm, w_in, rwkv_mu, rwkv_w0, rwkv_w2, rwkv_a0, rwkv_a2, rwkv_g2, rwkv_k_k, rwkv_k_a, rwkv_r_k, rwkv_ln_w, rwkv_ln_b, att_norm_w, ssm_conv_w, ssm_conv_b, ssm_dt_bias, ssm_A_log, ssm_D, ssm_norm_w, w_out, ffn2_norm, ffn2_w_gate, ffn2_w_up, ffn2_w_down, final_norm):
    depth = w_in.shape[0]
    bp, tp, _ = x_prompt.shape
    bs, ts, _ = x_sample.shape
    w_buf = cache_att_k.shape[2]
    mp = bp * tp
    ms = bs * ts
    assert mp % ROW_TILE == 0 and ms <= ROW_TILE and tp % (ATT_STEPS * DILATIONS[-1]) == 0 and ts <= 8
    n_ptiles = mp // ROW_TILE
    ts_rwkv = RWKV_CHUNK
    ts_ssd = 8

    x = jnp.concatenate([x_prompt.reshape(mp, D_MODEL), _pad_rows(x_sample.reshape(ms, D_MODEL), ROW_TILE)], axis=0)
    zeros_shift = jnp.zeros((bp, RWKV_PROJ), F32)
    zeros_s = jnp.zeros((bp, RWKV_HEADS, HEAD_DIM, HEAD_DIM), F32)
    zeros_conv = jnp.zeros((bp, 8, SSM_CONV_DIM), F32)
    zeros_h = jnp.zeros((bp, SSM_HEADS, HEAD_DIM, SSM_STATE), F32)
    to_cp = lambda c: jnp.transpose(c, (0, 1, 3, 4, 2)).reshape(depth, bs, ATT_DIM, w_buf)
    cache_kt, cache_vt = to_cp(cache_att_k), to_cp(cache_att_v)

    outs = {k: [] for k in ("p_k", "p_v", "p_shift", "p_s", "p_conv", "p_h",
                            "s_k", "s_v", "s_shift", "s_s", "s_conv", "s_h")}
    mix = None
    wo = None
    y_prompt = y_sample = None
    for l in range(depth):
        row1 = lambda v: v[l].reshape(1, -1)
        lp = dict(rwkv_mu=row1(rwkv_mu), rwkv_w0=row1(rwkv_w0), rwkv_w2=rwkv_w2[l], rwkv_a0=row1(rwkv_a0),
                  rwkv_a2=rwkv_a2[l], rwkv_g2=rwkv_g2[l], rwkv_k_k=row1(rwkv_k_k), rwkv_k_a=row1(rwkv_k_a),
                  rwkv_r_k=row1(rwkv_r_k), rwkv_ln_w=row1(rwkv_ln_w), rwkv_ln_b=row1(rwkv_ln_b),
                  ssm_conv_w=ssm_conv_w[l], ssm_conv_b=row1(ssm_conv_b),
                  ssm_dt_bias=_pad_lanes(row1(ssm_dt_bias), LANES), ssm_A_log=_pad_lanes(row1(ssm_A_log), LANES),
                  ssm_D=_pad_lanes(row1(ssm_D), LANES), ssm_norm_w=row1(ssm_norm_w))
        bf = lambda w: w[l].astype(BF16)
        if l == 0:
            x = _ffn_call(x, None, None, row1(ffn1_norm), bf(ffn1_w_gate), bf(ffn1_w_up), bf(ffn1_w_down), None,
                          n_ptiles)
        (p_rwkv, q3, k3, v3, k_cp, v_cp, z, xbc, dt, pr_t, q_s, k_s, v_s, z_t, xbc_t, dt_t) = _proj_call(
            x, row1(mix_norm), _pad_lanes(w_in[l], PROJ_PAD).astype(BF16), n_ptiles, tp)

        ya_p, s_p, last_p = _rwkv_call(p_rwkv, zeros_shift, zeros_s, lp, bp, tp, tp)
        yb_p = _att_prompt_call(q3, k3, v3, row1(att_norm_w), bp, tp)
        yc_p, h_p, conv_p = _ssd_call(z, xbc, dt, zeros_conv, zeros_h, lp, bp, tp, tp, SSD_CHUNK)

        def sample_rows(a, t_pad):
            return jnp.pad(a[:ms].reshape(bs, ts, -1), ((0, 0), (0, t_pad - ts), (0, 0)))

        pr_s = sample_rows(pr_t, ts_rwkv)
        ya_s, s_s, last_s = _rwkv_call(pr_s.reshape(bs * ts_rwkv, RWKV_PROJ), state_rwkv_shift[l], state_rwkv[l], lp,
                               bs, ts_rwkv, ts)
        yb_s = _att_sample_call(sample_rows(q_s, 8), sample_rows(k_s, 8), sample_rows(v_s, 8),
                                cache_kt, cache_vt, l, row1(att_norm_w), ts)
        conv8 = jnp.pad(state_ssm_conv[l], ((0, 0), (8 - (SSM_CONV - 1), 0), (0, 0)))
        flat8 = lambda a: sample_rows(a, ts_ssd).reshape(bs * ts_ssd, -1)
        yc_s, h_s, conv_s = _ssd_call(flat8(z_t), flat8(xbc_t), flat8(dt_t), conv8, state_ssm[l], lp, bs, ts_ssd, ts,
                                      ts_ssd)

        def tile_rows(a, t_pad):
            return _pad_rows(a.reshape(bs, t_pad, -1)[:, :ts].reshape(ms, -1), ROW_TILE)

        mix = ((ya_p, tile_rows(ya_s, ts_rwkv)), (yb_p, tile_rows(yb_s, 8)), (yc_p, tile_rows(yc_s, ts_ssd)))
        wo = w_out[l].astype(BF16)

        keep = min(2048, tp)
        from_cp = lambda a: jnp.transpose(a[:bp].reshape(bp, ATT_HEADS, HEAD_DIM, tp), (0, 3, 1, 2))[:, tp - keep:]
        outs["p_k"].append(from_cp(k_cp))
        outs["p_v"].append(from_cp(v_cp))
        outs["p_shift"].append(last_p[:, 0])
        outs["p_s"].append(s_p)
        outs["p_conv"].append(conv_p[:, 8 - (SSM_CONV - 1):])
        outs["p_h"].append(h_p)
        outs["s_k"].append(k_s[:ms].reshape(bs, ts, ATT_HEADS, HEAD_DIM))
        outs["s_v"].append(v_s[:ms].reshape(bs, ts, ATT_HEADS, HEAD_DIM))
        outs["s_shift"].append(last_s[:, 0])
        outs["s_s"].append(s_s)
        outs["s_conv"].append(conv_s[:, 8 - (SSM_CONV - 1):])
        outs["s_h"].append(h_s)

        x = _ffn_call(x, mix, wo, row1(ffn2_norm), bf(ffn2_w_gate), bf(ffn2_w_up), bf(ffn2_w_down),
                      final_norm.reshape(1, -1) if l == depth - 1 else None, n_ptiles)
        if l == depth - 1:
            y_prompt, y_sample = x
        else:
            lnext = l + 1
            x = _ffn_call(x, None, None, ffn1_norm[lnext].reshape(1, -1), ffn1_w_gate[lnext].astype(BF16),
                          ffn1_w_up[lnext].astype(BF16), ffn1_w_down[lnext].astype(BF16), None, n_ptiles)

    st = lambda k: jnp.stack(outs[k])
    return (y_prompt.reshape(bp, tp, D_MODEL), y_sample[:ms].reshape(bs, ts, D_MODEL),
            st("p_k"), st("p_v"), st("p_shift"), st("p_s"), st("p_conv"), st("p_h"),
            st("s_k"), st("s_v"), st("s_shift"), st("s_s"), st("s_conv"), st("s_h"))
```

```python
import functools

import jax
import jax.numpy as jnp
from jax import lax
from jax.experimental import pallas as pl
from jax.experimental.pallas import tpu as pltpu

F32 = jnp.float32
BF16 = jnp.bfloat16

D_MODEL = 1024
FFN_DIM = 2816
HEAD_DIM = 64
RWKV_HEADS = 4
RWKV_DIM = RWKV_HEADS * HEAD_DIM
RWKV_PROJ = 1024
RWKV_GN_EPS = 64e-5
ATT_HEADS = 6
ATT_DIM = ATT_HEADS * HEAD_DIM
ATT_PAIRS = ATT_HEADS // 2
DILATIONS = (1, 4, 16)
ATT_STEPS = 128
SSM_HEADS = 6
SSM_DIM = SSM_HEADS * HEAD_DIM
SSM_GROUPS = 2
SSM_STATE = 128
SSM_CONV = 4
SSM_CONV_DIM = SSM_DIM + 2 * SSM_GROUPS * SSM_STATE
PROJ_DIM = RWKV_PROJ + 3 * ATT_DIM + SSM_DIM + SSM_CONV_DIM + SSM_HEADS
PROJ_PAD = 3584
NORM_EPS = 1e-6
NEG_INF = -1e30
LOG2E = 1.4426950408889634

LANES = 128
ROW_TILE = 512
FFN_CHUNK = 256
RWKV_CHUNK = 64
RWKV_GROUP = 8
SSD_CHUNK = 128
SSD_GROUP = 4
VMEM_LIMIT = 56 * 1024 * 1024

OFF_ATT = RWKV_PROJ
OFF_Z = OFF_ATT + 3 * ATT_DIM
OFF_XBC = OFF_Z + SSM_DIM
OFF_DT = OFF_XBC + SSM_CONV_DIM


def _params(sem):
    return pltpu.CompilerParams(dimension_semantics=sem, vmem_limit_bytes=VMEM_LIMIT)


def _dot(a, b):
    return jnp.dot(a.astype(BF16), b.astype(BF16), preferred_element_type=F32)


def _dot_nt(a, b):
    return lax.dot_general(a.astype(BF16), b.astype(BF16), (((1,), (1,)), ((), ())),
                           preferred_element_type=F32)


def _dot_tn(a, b):
    return lax.dot_general(a.astype(BF16), b.astype(BF16), (((0,), (0,)), ((), ())),
                           preferred_element_type=F32)


def _dot_hi(a, b):
    return jnp.dot(a, b, preferred_element_type=F32, precision=lax.Precision.HIGHEST)


def _rms(x, w):
    return x * lax.rsqrt(jnp.mean(x * x, axis=-1, keepdims=True) + NORM_EPS) * w


def _sigmoid(x):
    return 1.0 / (1.0 + jnp.exp(-x))


def _softplus(x):
    return jnp.maximum(x, 0.0) + jnp.log(1.0 + jnp.exp(-jnp.abs(x)))


def _div(x, n):
    assert n & (n - 1) == 0
    return lax.shift_right_logical(x, n.bit_length() - 1)


def _mod(x, n):
    assert n & (n - 1) == 0
    return x & (n - 1)


def _cumsum_rows(x):
    n = x.shape[0]
    row = lax.broadcasted_iota(jnp.int32, x.shape, 0)
    s = 1
    while s < n:
        x = x + jnp.where(row >= s, pltpu.roll(x, s, axis=0), 0.0)
        s *= 2
    return x


def _ffn_body(has_mix, final, n_prompt_tiles, *refs):
    it = iter(refs)
    x_ref = next(it)
    if has_mix:
        yap, yas, ybp, ybs, ycp, ycs, wo_ref = (next(it) for _ in range(7))
    nw_ref, wg_ref, wu_ref, wd_ref = (next(it) for _ in range(4))
    if final:
        fn_ref = next(it)
        op_ref, os_ref = next(it), next(it)
    else:
        o_ref = next(it)
    h_ref, acc_ref = next(it), next(it)
    i = pl.program_id(0)
    is_sample = i >= n_prompt_tiles

    x = x_ref[...]
    if has_mix:
        ya = jnp.where(is_sample, yas[...], yap[...])
        yb = jnp.where(is_sample, ybs[...], ybp[...])
        yc = jnp.where(is_sample, ycs[...], ycp[...])
        x = (x + _dot(ya, wo_ref[0:RWKV_DIM, :])
             + _dot(yb, wo_ref[RWKV_DIM:RWKV_DIM + ATT_DIM, :])
             + _dot(yc, wo_ref[RWKV_DIM + ATT_DIM:, :]))
    h_ref[...] = _rms(x, nw_ref[...]).astype(BF16)
    acc_ref[...] = x

    def chunk(c, carry):
        off = pl.multiple_of(c * FFN_CHUNK, FFN_CHUNK)
        h = h_ref[...]
        g = jnp.dot(h, wg_ref[:, pl.ds(off, FFN_CHUNK)], preferred_element_type=F32)
        u = jnp.dot(h, wu_ref[:, pl.ds(off, FFN_CHUNK)], preferred_element_type=F32)
        a = (g * _sigmoid(g) * u).astype(BF16)
        acc_ref[...] += 0.5 * jnp.dot(a, wd_ref[pl.ds(off, FFN_CHUNK), :], preferred_element_type=F32)
        return carry

    lax.fori_loop(0, FFN_DIM // FFN_CHUNK, chunk, 0)
    if final:
        y = _rms(acc_ref[...], fn_ref[...])

        @pl.when(jnp.logical_not(is_sample))
        def _():
            op_ref[...] = y

        @pl.when(is_sample)
        def _():
            os_ref[...] = y
    else:
        o_ref[...] = acc_ref[...]


def _ffn_call(x, mix, wo, nw, wg, wu, wd, final_norm, n_prompt_tiles):
    m_pad = x.shape[0]
    tm = ROW_TILE
    has_mix = mix is not None
    final = final_norm is not None
    last_p = n_prompt_tiles - 1
    row = lambda i: (i, 0)
    fixed = lambda i: (0, 0)
    prow = lambda i: (jnp.minimum(i, last_p), 0)
    resident = functools.partial(pl.BlockSpec, index_map=fixed, pipeline_mode=pl.Buffered(1))
    in_specs = [pl.BlockSpec((tm, D_MODEL), row)]
    args = [x]
    if has_mix:
        for yp, ys in mix:
            w = yp.shape[1]
            in_specs += [pl.BlockSpec((tm, w), prow), pl.BlockSpec((tm, w), fixed)]
            args += [yp, ys]
        in_specs.append(resident((D_MODEL, D_MODEL)))
        args.append(wo)
    in_specs += [pl.BlockSpec((1, D_MODEL), fixed), resident((D_MODEL, FFN_DIM)),
                 resident((D_MODEL, FFN_DIM)), resident((FFN_DIM, D_MODEL))]
    args += [nw, wg, wu, wd]
    if final:
        in_specs.append(pl.BlockSpec((1, D_MODEL), fixed))
        args.append(final_norm)
        out_shape = (jax.ShapeDtypeStruct((n_prompt_tiles * tm, D_MODEL), F32),
                     jax.ShapeDtypeStruct((tm, D_MODEL), F32))
        out_specs = (pl.BlockSpec((tm, D_MODEL), prow), pl.BlockSpec((tm, D_MODEL), fixed))
    else:
        out_shape = jax.ShapeDtypeStruct((m_pad, D_MODEL), F32)
        out_specs = pl.BlockSpec((tm, D_MODEL), row)
    return pl.pallas_call(
        functools.partial(_ffn_body, has_mix, final, n_prompt_tiles),
        grid=(m_pad // tm,), in_specs=in_specs, out_specs=out_specs, out_shape=out_shape,
        scratch_shapes=[pltpu.VMEM((tm, D_MODEL), BF16), pltpu.VMEM((tm, D_MODEL), F32)],
        compiler_params=_params(("arbitrary",)), name="ffn")(*args)


def _proj_body(n_prompt_tiles, x_ref, nw_ref, w_ref, pr_ref, q3_ref, k3_ref, v3_ref, kn_ref, vn_ref, z_ref, xbc_ref,
               dt_ref, prs_ref, qs_ref, ks_ref, vs_ref, zs_ref, xbcs_ref, dts_ref):
    h = _rms(x_ref[...], nw_ref[...]).astype(BF16)

    def seg(a, b):
        return jnp.dot(h, w_ref[:, a:b], preferred_element_type=F32)

    pr = seg(0, OFF_ATT)
    q = seg(OFF_ATT, OFF_ATT + ATT_DIM)
    k = seg(OFF_ATT + ATT_DIM, OFF_ATT + 2 * ATT_DIM)
    v = seg(OFF_ATT + 2 * ATT_DIM, OFF_Z)
    z = seg(OFF_Z, OFF_XBC)
    xbc = seg(OFF_XBC, OFF_DT)
    dt = seg(OFF_DT, PROJ_PAD)
    pr_ref[...] = pr
    z_ref[...] = z
    xbc_ref[...] = xbc
    dt_ref[...] = dt
    for j in range(ATT_PAIRS):
        q3_ref[j] = q[:, j * LANES:(j + 1) * LANES]
        k3_ref[j] = k[:, j * LANES:(j + 1) * LANES]
        v3_ref[j] = v[:, j * LANES:(j + 1) * LANES]
    prs_ref[...] = pr
    qs_ref[...] = q
    ks_ref[...] = k
    vs_ref[...] = v
    zs_ref[...] = z
    xbcs_ref[...] = xbc
    dts_ref[...] = dt

    @pl.when(pl.program_id(0) < n_prompt_tiles)
    def _():
        kn_ref[0] = k.T
        vn_ref[0] = v.T


def _proj_call(x, nw, w, n_prompt_tiles, t_prompt):
    m_pad = x.shape[0]
    tm = ROW_TILE
    mp = n_prompt_tiles * tm
    tiles_per_seq = t_prompt // tm
    assert m_pad == mp + tm
    row = lambda i: (i, 0)
    fixed = lambda i: (0, 0)
    row3 = lambda i: (0, i, 0)
    last_p = n_prompt_tiles - 1
    col = lambda i: (jnp.minimum(i, last_p) // tiles_per_seq, 0, jnp.minimum(i, last_p) % tiles_per_seq)
    sds = jax.ShapeDtypeStruct
    widths = (RWKV_PROJ, ATT_DIM, ATT_DIM, ATT_DIM, SSM_DIM, SSM_CONV_DIM, LANES)
    n_seq = mp // t_prompt
    out_shape = ((sds((m_pad, RWKV_PROJ), F32),
                  sds((ATT_PAIRS, m_pad, LANES), F32), sds((ATT_PAIRS, m_pad, LANES), F32),
                  sds((ATT_PAIRS, m_pad, LANES), F32),
                  sds((n_seq, ATT_DIM, t_prompt), F32), sds((n_seq, ATT_DIM, t_prompt), F32),
                  sds((m_pad, SSM_DIM), F32), sds((m_pad, SSM_CONV_DIM), F32), sds((m_pad, LANES), F32))
                 + tuple(sds((tm, w_), F32) for w_ in widths))
    p3 = pl.BlockSpec((ATT_PAIRS, tm, LANES), row3)
    out_specs = ((pl.BlockSpec((tm, RWKV_PROJ), row), p3, p3, p3,
                  pl.BlockSpec((1, ATT_DIM, tm), col), pl.BlockSpec((1, ATT_DIM, tm), col),
                  pl.BlockSpec((tm, SSM_DIM), row), pl.BlockSpec((tm, SSM_CONV_DIM), row),
                  pl.BlockSpec((tm, LANES), row))
                 + tuple(pl.BlockSpec((tm, w_), fixed) for w_ in widths))
    return pl.pallas_call(
        functools.partial(_proj_body, n_prompt_tiles),
        grid=(m_pad // tm,),
        in_specs=[pl.BlockSpec((tm, D_MODEL), row), pl.BlockSpec((1, D_MODEL), fixed),
                  pl.BlockSpec((D_MODEL, PROJ_PAD), fixed, pipeline_mode=pl.Buffered(1))],
        out_specs=out_specs, out_shape=out_shape,
        compiler_params=_params(("arbitrary",)), name="proj")(x, nw, w)


def _rwkv_body(chunk, group, by_batch, t_valid, t_total, p_ref, sh_ref, s0_ref, mu_ref, w0_ref, w2_ref, a0_ref, a2_ref,
               g2_ref, kk_ref, ka_ref, rk_ref, lnw_ref, lnb_ref, y_ref, so_ref, last_ref, sb_ref, prev_ref):
    C = chunk
    G = group
    H = RWKV_HEADS
    N = HEAD_DIM
    R = RWKV_DIM
    GC = G * C
    c = pl.program_id(1)
    nc = pl.num_programs(1)
    p = p_ref[...]
    row_p = lax.broadcasted_iota(jnp.int32, p.shape, 0)
    p_prev = pltpu.roll(p, 1, axis=0)

    if by_batch:
        for gi in range(G):
            p_prev = jnp.where(row_p == gi * C, sh_ref[gi], p_prev)
        zero = jnp.zeros((N, N), F32)
        states = [jnp.concatenate([jnp.concatenate([s0_ref[gi, h] if hh == h else zero for hh in range(H)], axis=1)
                                   for h in range(H)], axis=0) for gi in range(G)]
    else:
        @pl.when(c == 0)
        def _():
            sb_ref[...] = jnp.zeros((R, R), F32)
            for h in range(H):
                sb_ref[h * N:(h + 1) * N, h * N:(h + 1) * N] = s0_ref[0, h]
            prev_ref[...] = jnp.broadcast_to(sh_ref[0], (8, RWKV_PROJ))

        p_prev = jnp.where(row_p == 0, prev_ref[0:1, :], p_prev)
        prev_ref[...] = jnp.broadcast_to(p[GC - 1:GC, :], (8, RWKV_PROJ))
    ps = p + mu_ref[...] * (p_prev - p)
    r = ps[:, 0:R]
    k = ps[:, R:2 * R]
    v = ps[:, 2 * R:3 * R]
    w_lr = ps[:, 3 * R:3 * R + 64]
    a_lr = ps[:, 3 * R + 64:3 * R + 128]
    g_lr = ps[:, 3 * R + 128:]

    w_log = -_softplus(-(w0_ref[...] + _dot(jnp.tanh(w_lr), w2_ref[...]))) - 0.5
    ld = -jnp.exp(w_log)
    a = _sigmoid(a0_ref[...] + _dot(a_lr, a2_ref[...]))
    g = _dot(_sigmoid(g_lr), g2_ref[...])

    lane_r = lax.broadcasted_iota(jnp.int32, (R, R), 1)
    row_r = lax.broadcasted_iota(jnp.int32, (R, R), 0)
    own_head = _div(lane_r, N) == _div(row_r, N)
    head_ones = jnp.where(own_head, 1.0, 0.0).astype(BF16)

    def head_sum(x):
        hi = x.astype(BF16)
        lo = (x - hi.astype(F32)).astype(BF16)
        return (jnp.dot(hi, head_ones, preferred_element_type=F32)
                + jnp.dot(lo, head_ones, preferred_element_type=F32))

    kk = k * kk_ref[...]
    kk = kk / jnp.maximum(jnp.sqrt(head_sum(kk * kk)), 1e-12)
    kp = k * (1.0 + (a - 1.0) * ka_ref[...])
    if t_valid < t_total:
        row_r2 = lax.broadcasted_iota(jnp.int32, (GC, R), 0)
        live = (_mod(row_r2, C) if by_batch else c * GC + row_r2) < t_valid
        ld = jnp.where(live, ld, 0.0)
        kk = jnp.where(live, kk, 0.0)
        kp = jnp.where(live, kp, 0.0)
        v = jnp.where(live, v, 0.0)

    HC = H * C
    bd_row = _div(lax.broadcasted_iota(jnp.int32, (HC, R), 0), C)
    bd_lane = _div(lax.broadcasted_iota(jnp.int32, (HC, R), 1), N)

    def bd_rows(x):
        return jnp.where(bd_row == bd_lane, jnp.concatenate([x] * H, axis=0), 0.0)

    sq_row = _div(lax.broadcasted_iota(jnp.int32, (HC, HC), 0), C)
    sq_col = _div(lax.broadcasted_iota(jnp.int32, (HC, HC), 1), C)

    def mm(x, y):
        return _dot(x, jnp.where(sq_row == sq_col, jnp.concatenate([y] * H, axis=0), 0.0))

    t_idx = lax.broadcasted_iota(jnp.int32, (C, HC), 0)
    s_idx = _mod(lax.broadcasted_iota(jnp.int32, (C, HC), 1), C)
    strict = s_idx < t_idx
    incl = s_idx <= t_idx
    eye = jnp.where(s_idx == t_idx, 1.0, 0.0).astype(F32)
    same16 = _div(s_idx, 16) == _div(t_idx, 16)

    chunks = [slice(gi * C, (gi + 1) * C) for gi in range(G)]
    cum = [_cumsum_rows(ld[rows]) for rows in chunks]
    w_inc = [jnp.exp(x) for x in cum]
    w_inv = [jnp.exp(-x) for x in cum]
    w_end = [x[C - 1:C, :] for x in w_inc]
    b_t = [kk[rows] * a[rows] * wi for rows, wi in zip(chunks, w_inv)]
    k_t = [kp[rows] * wi for rows, wi in zip(chunks, w_inv)]
    lhs = [jnp.concatenate([kk[rows] * jnp.exp(cm - ld[rows]), r[rows] * wi], axis=0)
           for rows, cm, wi in zip(chunks, cum, w_inc)]
    sc = [_dot_nt(x, jnp.concatenate([bd_rows(bt), bd_rows(kt)], axis=0)) for x, bt, kt in zip(lhs, b_t, k_t)]
    a_ab = [jnp.where(strict, x[0:C, 0:HC], 0.0) for x in sc]
    a_ak = [jnp.where(strict, x[0:C, HC:], 0.0) for x in sc]
    p_rb = [jnp.where(incl, x[C:, 0:HC], 0.0) for x in sc]
    p_rk = [jnp.where(incl, x[C:, HC:], 0.0) for x in sc]
    e1 = [jnp.where(same16, -x, 0.0) for x in a_ab]
    n_low = [x + y for x, y in zip(a_ab, e1)]
    e2 = [mm(x, x) for x in e1]
    st = [mm(jnp.concatenate([eye + x, y], axis=0), y) for x, y in zip(e1, e2)]
    p4 = [eye + x + y[0:C] for x, y in zip(e1, st)]
    e4 = [y[C:] for y in st]
    st = [mm(jnp.concatenate([x, y], axis=0), y) for x, y in zip(p4, e4)]
    p8 = [x + y[0:C] for x, y in zip(p4, st)]
    e8 = [y[C:] for y in st]
    d_inv = [x + mm(x, y) for x, y in zip(p8, e8)]
    f1 = [-mm(x, y) for x, y in zip(d_inv, n_low)]
    f2 = [mm(x, x) for x in f1]
    q3 = [eye + x + mm(eye + x, y) for x, y in zip(f1, f2)]
    t_inv = [mm(x, y) for x, y in zip(q3, d_inv)]
    v_c = [v[rows] for rows in chunks]
    v_bd = [bd_rows(x) for x in v_c]
    avp = [_dot(jnp.concatenate([x, y], axis=0), z) for x, y, z in zip(a_ak, p_rk, v_bd)]
    tail = [jnp.concatenate([bt * we, kt * we], axis=0) for bt, kt, we in zip(b_t, k_t, w_end)]

    def chain(gi, sb):
        ks_rs = _dot_nt(lhs[gi], sb)
        u = _dot(t_inv[gi], bd_rows(-ks_rs[0:C] - avp[gi][0:C]))
        upd = _dot_tn(jnp.concatenate([u, v_c[gi]], axis=0), tail[gi])
        y_c = ks_rs[C:] + _dot(p_rb[gi], bd_rows(u)) + avp[gi][C:]
        return jnp.where(own_head, sb * w_end[gi] + upd, 0.0), y_c

    ys = []
    if by_batch:
        for gi in range(G):
            sb, y_c = chain(gi, states[gi])
            ys.append(y_c)
            for h in range(H):
                so_ref[gi, h] = sb[h * N:(h + 1) * N, h * N:(h + 1) * N]
    else:
        sb = sb_ref[...]
        for gi in range(G):
            sb, y_c = chain(gi, sb)
            ys.append(y_c)
        sb_ref[...] = sb
    y = jnp.concatenate(ys, axis=0) if G > 1 else ys[0]

    yc = y - head_sum(y) * (1.0 / N)
    var = head_sum(yc * yc) * (1.0 / N)
    yn = yc * lax.rsqrt(var + RWKV_GN_EPS) * lnw_ref[...] + lnb_ref[...]
    bonus = head_sum(r * kp * rk_ref[...]) * v
    y_ref[...] = (yn + bonus) * g
    if by_batch:
        for gi in range(G):
            last_ref[gi] = jnp.broadcast_to(p[gi * C + t_valid - 1:gi * C + t_valid, :], (8, RWKV_PROJ))
    else:
        @pl.when(c == nc - 1)
        def _():
            last_ref[0] = prev_ref[...]
            for h in range(H):
                so_ref[0, h] = sb_ref[h * N:(h + 1) * N, h * N:(h + 1) * N]


def _rwkv_call(p, shift_prev, s0, lp, n_batch, t_total, t_valid):
    C = RWKV_CHUNK
    by_batch = t_total == C
    G = RWKV_GROUP
    assert (n_batch if by_batch else t_total // C) % G == 0
    nc = 1 if by_batch else t_total // (G * C)
    n_outer = n_batch // G if by_batch else n_batch
    gb = G if by_batch else 1
    vec = lambda n: pl.BlockSpec((1, n), lambda b, c: (0, 0))
    mat = lambda a, n: pl.BlockSpec((a, n), lambda b, c: (0, 0))
    in_specs = [pl.BlockSpec((G * C, RWKV_PROJ), lambda b, c: (b * nc + c, 0)),
                pl.BlockSpec((gb, 1, RWKV_PROJ), lambda b, c: (b, 0, 0)),
                pl.BlockSpec((gb, RWKV_HEADS, HEAD_DIM, HEAD_DIM), lambda b, c: (b, 0, 0, 0)),
                vec(RWKV_PROJ), vec(RWKV_DIM), mat(64, RWKV_DIM), vec(RWKV_DIM), mat(64, RWKV_DIM),
                mat(128, RWKV_DIM), vec(RWKV_DIM), vec(RWKV_DIM), vec(RWKV_DIM), vec(RWKV_DIM), vec(RWKV_DIM)]
    out_shape = (jax.ShapeDtypeStruct((n_batch * t_total, RWKV_DIM), F32),
                 jax.ShapeDtypeStruct((n_batch, RWKV_HEADS, HEAD_DIM, HEAD_DIM), F32),
                 jax.ShapeDtypeStruct((n_batch, 8, RWKV_PROJ), F32))
    out_specs = (pl.BlockSpec((G * C, RWKV_DIM), lambda b, c: (b * nc + c, 0)),
                 pl.BlockSpec((gb, RWKV_HEADS, HEAD_DIM, HEAD_DIM), lambda b, c: (b, 0, 0, 0)),
                 pl.BlockSpec((gb, 8, RWKV_PROJ), lambda b, c: (b, 0, 0)))
    return pl.pallas_call(
        functools.partial(_rwkv_body, C, G, by_batch, t_valid, t_total),
        grid=(n_outer, nc), in_specs=in_specs, out_specs=out_specs, out_shape=out_shape,
        scratch_shapes=[pltpu.VMEM((RWKV_DIM, RWKV_DIM), F32), pltpu.VMEM((8, RWKV_PROJ), F32)],
        compiler_params=_params(("arbitrary", "arbitrary")), name="rwkv")(
            p, shift_prev.reshape(n_batch, 1, RWKV_PROJ), s0,
            lp["rwkv_mu"], lp["rwkv_w0"], lp["rwkv_w2"], lp["rwkv_a0"], lp["rwkv_a2"], lp["rwkv_g2"],
            lp["rwkv_k_k"], lp["rwkv_k_a"], lp["rwkv_r_k"], lp["rwkv_ln_w"], lp["rwkv_ln_b"])


def _alibi_slope(h):
    return 2.0 ** (-8.0 * (h + 1) / ATT_HEADS)


def _att_prompt_body(t_len, q_ref, k_ref, v_ref, nw_ref, o_ref, acc_ref, m_ref, l_ref, bias_ref):
    blk = ATT_STEPS
    pairs = range(ATT_PAIRS)
    lane_top = lax.broadcasted_iota(jnp.int32, (blk, LANES), 1) < HEAD_DIM

    @pl.when(pl.program_id(0) == 0)
    def _():
        qi = _mod(lax.broadcasted_iota(jnp.int32, (2 * blk, 2 * blk), 0), blk)
        kj = lax.broadcasted_iota(jnp.int32, (2 * blk, 2 * blk), 1)
        top = lax.broadcasted_iota(jnp.int32, (2 * blk, 2 * blk), 0) < blk
        for di, d in enumerate(DILATIONS):
            for off in range(2):
                delta = off * blk + qi - kj
                valid = (delta >= 0) & (delta <= blk)
                dist = (d * delta).astype(F32) * LOG2E
                for j in pairs:
                    slope = jnp.where(top, _alibi_slope(2 * j), _alibi_slope(2 * j + 1))
                    bias_ref[di, off, j] = jnp.where(valid, -slope * dist, NEG_INF)

    order = tuple(reversed(tuple(enumerate(DILATIONS))))
    for di, d in order:
        n = t_len // d
        kw = min(2 * blk, n)
        nb = n // blk

        def step(it, carry, di=di, d=d, kw=kw, nb=nb):
            res = it // nb
            b = it % nb
            kb = jnp.maximum(b - 1, 0) if kw > blk else b
            q_rows = pl.ds(res + d * blk * b, blk, stride=d)
            k_rows = pl.ds(res + d * blk * kb, kw, stride=d)
            q2 = [q_ref.at[j][q_rows, :] * (HEAD_DIM ** -0.5 * LOG2E) for j in pairs]
            k2 = [k_ref.at[j][k_rows, :] for j in pairs]
            v2 = [v_ref.at[j][k_rows, :] for j in pairs]
            q_st = [jnp.concatenate([jnp.where(lane_top, x, 0.0), jnp.where(lane_top, 0.0, x)], axis=0) for x in q2]
            s = [_dot_nt(x, y) + bias_ref[di, b - kb, j, :, 0:kw] for j, x, y in zip(pairs, q_st, k2)]
            mh = [jnp.max(x, axis=-1, keepdims=True) for x in s]
            pr = [jnp.exp2(x - m) for x, m in zip(s, mh)]
            lh = [jnp.sum(x, axis=-1, keepdims=True) for x in pr]
            ah = [_dot(x, y) for x, y in zip(pr, v2)]

            def unstack(x):
                x = jnp.broadcast_to(x, (2 * blk, LANES))
                return jnp.where(lane_top, x[0:blk], x[blk:])

            results = []
            for j in pairs:
                acc, m, l = unstack(ah[j]), unstack(mh[j]), unstack(lh[j])
                if d != order[0][1]:
                    m_old = m_ref.at[j][q_rows, :]
                    m_new = jnp.maximum(m_old, m)
                    e_old = jnp.exp2(m_old - m_new)
                    e_new = jnp.exp2(m - m_new)
                    acc = acc_ref.at[j][q_rows, :] * e_old + acc * e_new
                    l = l_ref.at[j][q_rows, :] * e_old + l * e_new
                    m = m_new
                results.append((acc, m, l))
            if d != order[-1][1]:
                for j, (acc, m, l) in enumerate(results):
                    acc_ref.at[j][q_rows, :] = acc
                    m_ref.at[j][q_rows, :] = m
                    l_ref.at[j][q_rows, :] = l
            else:
                assert d == 1
                ys = [acc / l for acc, _, l in results]
                ms = sum(jnp.sum(y * y, axis=-1, keepdims=True) for y in ys) * (1.0 / ATT_DIM)
                scale = lax.rsqrt(ms + NORM_EPS)
                rows = pl.ds(pl.multiple_of(blk * b, blk), blk)
                for j in pairs:
                    o_ref[rows, j * LANES:(j + 1) * LANES] = ys[j] * scale * nw_ref[:, j * LANES:(j + 1) * LANES]
            return carry

        lax.fori_loop(0, d * nb, step, 0)


def _att_prompt_call(q3, k3, v3, nw, n_batch, t_len):
    blk3 = pl.BlockSpec((ATT_PAIRS, t_len, LANES), lambda b: (0, b, 0))
    return pl.pallas_call(
        functools.partial(_att_prompt_body, t_len),
        grid=(n_batch,),
        in_specs=[blk3, blk3, blk3, pl.BlockSpec((1, ATT_DIM), lambda b: (0, 0))],
        out_specs=pl.BlockSpec((t_len, ATT_DIM), lambda b: (b, 0)),
        out_shape=jax.ShapeDtypeStruct((n_batch * t_len, ATT_DIM), F32),
        scratch_shapes=[pltpu.VMEM((ATT_PAIRS, t_len, LANES), F32)] * 3
        + [pltpu.VMEM((len(DILATIONS), 2, ATT_PAIRS, 2 * ATT_STEPS, 2 * ATT_STEPS), F32)],
        compiler_params=_params(("arbitrary",)), name="att_prompt")(q3, k3, v3, nw)


def _multiplicity(dist):
    mult = jnp.zeros(dist.shape, F32)
    for d in DILATIONS:
        hit = (_mod(dist, d) == 0) & (dist <= d * ATT_STEPS)
        mult = mult + jnp.where(hit, 1.0, 0.0)
    return mult


def _att_sample_body(t_new, w_buf, q_ref, kn_ref, vn_ref, kt_ref, vt_ref, nw_ref, o_ref):
    RQ = 8
    rows = ATT_HEADS * RQ
    q = q_ref[0] * (HEAD_DIM ** -0.5)
    head_of_row = _div(lax.broadcasted_iota(jnp.int32, (rows, ATT_DIM), 0), RQ)
    head_of_lane = _div(lax.broadcasted_iota(jnp.int32, (rows, ATT_DIM), 1), HEAD_DIM)
    own = head_of_row == head_of_lane
    q_bd = jnp.where(own, jnp.concatenate([q] * ATT_HEADS, axis=0), 0.0)

    s_buf = _dot(q_bd, kt_ref[0, 0])
    s_new = _dot_nt(q_bd, kn_ref[0])

    def weights(s, key_pos):
        t = _mod(lax.broadcasted_iota(jnp.int32, s.shape, 0), RQ)
        h = _div(lax.broadcasted_iota(jnp.int32, s.shape, 0), RQ)
        dist = w_buf + t - key_pos
        slope = sum(jnp.where(h == hh, _alibi_slope(hh), 0.0) for hh in range(ATT_HEADS))
        mult = jnp.where(dist >= 0, _multiplicity(jnp.maximum(dist, 0)), 0.0)
        return jnp.where(mult > 0, s - slope * dist.astype(F32), NEG_INF), mult

    sb, mult_b = weights(s_buf, lax.broadcasted_iota(jnp.int32, s_buf.shape, 1))
    sn, mult_n = weights(s_new, w_buf + lax.broadcasted_iota(jnp.int32, s_new.shape, 1))
    m = jnp.maximum(jnp.max(sb, axis=-1, keepdims=True), jnp.max(sn, axis=-1, keepdims=True))
    pb = mult_b * jnp.exp(sb - m)
    pn = mult_n * jnp.exp(sn - m)
    den = jnp.sum(pb, axis=-1, keepdims=True) + jnp.sum(pn, axis=-1, keepdims=True)
    num = _dot_nt(pb, vt_ref[0, 0]) + _dot(pn, vn_ref[0])
    out = jnp.where(own, num / den, 0.0).reshape(ATT_HEADS, RQ, ATT_DIM)
    o_ref[0] = _rms(jnp.sum(out, axis=0), nw_ref[...])


def _att_sample_call(q, k_new, v_new, cache_kt, cache_vt, layer, nw, t_new):
    depth, n_batch, _, w_buf = cache_kt.shape
    small = pl.BlockSpec((1, 8, ATT_DIM), lambda b: (b, 0, 0))
    big = pl.BlockSpec((1, 1, ATT_DIM, w_buf), lambda b: (layer, b, 0, 0))
    return pl.pallas_call(
        functools.partial(_att_sample_body, t_new, w_buf),
        grid=(n_batch,),
        in_specs=[small, small, small, big, big, pl.BlockSpec((1, ATT_DIM), lambda b: (0, 0))],
        out_specs=small, out_shape=jax.ShapeDtypeStruct((n_batch, 8, ATT_DIM), F32),
        compiler_params=_params(("arbitrary",)), name="att_sample")(q, k_new, v_new, cache_kt, cache_vt, nw)


def _ssd_body(chunk, group, t_valid, t_total, z_ref, xbc_ref, dt_ref, cp_ref, h0_ref, cw_ref, cb_ref, dtb_ref,
              alog_ref, dskip_ref, nw_ref, y_ref, ho_ref, cv_ref, hs_ref, xc_ref):
    L = chunk
    G = group
    GL = G * L
    P = HEAD_DIM
    NS = SSM_STATE
    c = pl.program_id(1)
    nc = pl.num_programs(1)

    @pl.when(c == 0)
    def _():
        hs_ref[...] = h0_ref[0]
        xc_ref[0:8, :] = cp_ref[0]

    xc_ref[8:8 + GL, :] = xbc_ref[...]
    conv = cb_ref[...]
    for i in range(SSM_CONV):
        conv = conv + xc_ref[5 + i:5 + i + GL, :] * cw_ref[i:i + 1, :]
    last_valid = t_valid - (t_total // GL - 1) * GL
    conv_tail = xc_ref[last_valid:last_valid + 8, :]
    xc_ref[0:8, :] = xc_ref[GL:GL + 8, :]
    act = conv * _sigmoid(conv)
    xs = act[:, 0:SSM_DIM]
    bm = act[:, SSM_DIM:SSM_DIM + SSM_GROUPS * NS]
    cm = act[:, SSM_DIM + SSM_GROUPS * NS:]

    dt_all = _softplus(dt_ref[...] + dtb_ref[...])
    if t_valid < t_total:
        live = (c * GL + lax.broadcasted_iota(jnp.int32, dt_all.shape, 0)) < t_valid
        dt_all = jnp.where(live, dt_all, 0.0)
    a_neg = -jnp.exp(alog_ref[...])
    li = lax.broadcasted_iota(jnp.int32, (L, L), 0)
    si = lax.broadcasted_iota(jnp.int32, (L, L), 1)
    causal = li >= si
    rep = SSM_HEADS // SSM_GROUPS
    GW = rep * P
    bd_row = _div(lax.broadcasted_iota(jnp.int32, (rep * L, GW), 0), L)
    bd_lane = _div(lax.broadcasted_iota(jnp.int32, (rep * L, GW), 1), P)

    def lanes_of(cols):
        return jnp.concatenate([jnp.broadcast_to(x, (x.shape[0], P)) for x in cols], axis=1)

    parts = []
    for ci in range(G):
        rows = slice(ci * L, (ci + 1) * L)
        dt = dt_all[rows]
        acum = _cumsum_rows(dt * a_neg)
        acum_t = acum.T
        dt_t = dt.T
        per_group = []
        for g in range(SSM_GROUPS):
            heads = range(g * rep, (g + 1) * rep)
            x_g = xs[rows, g * GW:(g + 1) * GW]
            b_g = bm[rows, g * NS:(g + 1) * NS]
            c_g = cm[rows, g * NS:(g + 1) * NS]
            cb = _dot_nt(c_g, b_g)
            scores = [cb * jnp.exp(jnp.where(causal, acum[:, h:h + 1] - acum_t[h:h + 1, :], NEG_INF)) * dt_t[h:h + 1, :]
                      for h in heads]
            ac = lanes_of([acum[:, h:h + 1] for h in heads])
            a_last = lanes_of([acum[L - 1:L, h:h + 1] for h in heads])
            x_bd = jnp.where(bd_row == bd_lane, jnp.concatenate([x_g] * rep, axis=0), 0.0)
            y_own = _dot(jnp.concatenate(scores, axis=1), x_bd) + lanes_of([dskip_ref[:, h:h + 1] for h in heads]) * x_g
            to_end = jnp.exp(a_last - ac) * lanes_of([dt[:, h:h + 1] for h in heads])
            keep = jnp.concatenate([jnp.broadcast_to(jnp.exp(acum[L - 1:L, h:h + 1]), (P, NS)) for h in heads], axis=0)
            per_group.append((y_own, c_g, jnp.exp(ac), keep, _dot_tn(x_g * to_end, b_g)))
        parts.append(per_group)

    ys = [[None] * SSM_GROUPS for _ in range(G)]
    for g in range(SSM_GROUPS):
        h_g = hs_ref[g * rep:(g + 1) * rep].reshape(GW, NS)
        for ci in range(G):
            y_own, c_g, grow, keep, add = parts[ci][g]
            ys[ci][g] = y_own + _dot_nt(c_g, h_g) * grow
            h_g = h_g * keep + add
        hs_ref[g * rep:(g + 1) * rep] = h_g.reshape(rep, P, NS)
    y = jnp.concatenate([jnp.concatenate(row, axis=1) for row in ys], axis=0)
    z = z_ref[...]
    y = y * (z * _sigmoid(z))
    y_ref[...] = _rms(y, nw_ref[...])

    @pl.when(c == nc - 1)
    def _():
        ho_ref[0] = hs_ref[...]
        cv_ref[0] = conv_tail


def _ssd_call(z, xbc, dt, conv_prev8, h0, lp, n_batch, t_total, t_valid, chunk):
    L = chunk
    G = SSD_GROUP if (t_total // L) % SSD_GROUP == 0 else 1
    nc = t_total // (G * L)
    rows = lambda w: pl.BlockSpec((G * L, w), lambda b, c: (b * nc + c, 0))
    vec = lambda a, n: pl.BlockSpec((a, n), lambda b, c: (0, 0))
    in_specs = [rows(SSM_DIM), rows(SSM_CONV_DIM), rows(LANES),
                pl.BlockSpec((1, 8, SSM_CONV_DIM), lambda b, c: (b, 0, 0)),
                pl.BlockSpec((1, SSM_HEADS, HEAD_DIM, SSM_STATE), lambda b, c: (b, 0, 0, 0)),
                vec(SSM_CONV, SSM_CONV_DIM), vec(1, SSM_CONV_DIM), vec(1, LANES), vec(1, LANES), vec(1, LANES),
                vec(1, SSM_DIM)]
    out_shape = (jax.ShapeDtypeStruct((n_batch * t_total, SSM_DIM), F32),
                 jax.ShapeDtypeStruct((n_batch, SSM_HEADS, HEAD_DIM, SSM_STATE), F32),
                 jax.ShapeDtypeStruct((n_batch, 8, SSM_CONV_DIM), F32))
    out_specs = (rows(SSM_DIM), pl.BlockSpec((1, SSM_HEADS, HEAD_DIM, SSM_STATE), lambda b, c: (b, 0, 0, 0)),
                 pl.BlockSpec((1, 8, SSM_CONV_DIM), lambda b, c: (b, 0, 0)))
    return pl.pallas_call(
        functools.partial(_ssd_body, L, G, t_valid, t_total),
        grid=(n_batch, nc), in_specs=in_specs, out_specs=out_specs, out_shape=out_shape,
        scratch_shapes=[pltpu.VMEM((SSM_HEADS, HEAD_DIM, SSM_STATE), F32),
                        pltpu.VMEM((G * L + 8, SSM_CONV_DIM), F32)],
        compiler_params=_params(("arbitrary", "arbitrary")), name="ssd")(
            z, xbc, dt, conv_prev8, h0, lp["ssm_conv_w"], lp["ssm_conv_b"], lp["ssm_dt_bias"], lp["ssm_A_log"],
            lp["ssm_D"], lp["ssm_norm_w"])


def _pad_lanes(v, n):
    return jnp.pad(v, ((0, 0), (0, n - v.shape[-1])))


def _pad_rows(x, n):
    return jnp.pad(x, ((0, n - x.shape[0]),) + ((0, 0),) * (x.ndim - 1))


def kernel(x_prompt, x_sample, cache_att_k, cache_att_v, state_rwkv_shift, state_rwkv, state_ssm_conv, state_ssm, ffn1_norm, ffn1_w_gate, ffn1_w_up, ffn1_w_down, mix_norm, w_in, rwkv_mu, rwkv_w0, rwkv_w2, rwkv_a0, rwkv_a2, rwkv_g2, rwkv_k_k, rwkv_k_a, rwkv_r_k, rwkv_ln_w, rwkv_ln_b, att_norm_w, ssm_conv_w, ssm_conv_b, ssm_dt_bias, ssm_A_log, ssm_D, ssm_norm_w, w_out, ffn2_norm, ffn2_w_gate, ffn2_w_up, ffn2_w_down, final_norm):
    depth = w_in.shape[0]
    bp, tp, _ = x_prompt.shape
    bs, ts, _ = x_sample.shape
    w_buf = cache_att_k.shape[2]
    mp = bp * tp
    ms = bs * ts
    assert mp % ROW_TILE == 0 and ms <= ROW_TILE and tp % (ATT_STEPS * DILATIONS[-1]) == 0 and ts <= 8
    n_ptiles = mp // ROW_TILE
    ts_rwkv = RWKV_CHUNK
    ts_ssd = 8

    x = jnp.concatenate([x_prompt.reshape(mp, D_MODEL), _pad_rows(x_sample.reshape(ms, D_MODEL), ROW_TILE)], axis=0)
    zeros_shift = jnp.zeros((bp, RWKV_PROJ), F32)
    zeros_s = jnp.zeros((bp, RWKV_HEADS, HEAD_DIM, HEAD_DIM), F32)
    zeros_conv = jnp.zeros((bp, 8, SSM_CONV_DIM), F32)
    zeros_h = jnp.zeros((bp, SSM_HEADS, HEAD_DIM, SSM_STATE), F32)
    to_cp = lambda c: jnp.transpose(c, (0, 1, 3, 4, 2)).reshape(depth, bs, ATT_DIM, w_buf)
    cache_kt, cache_vt = to_cp(cache_att_k), to_cp(cache_att_v)

    outs = {k: [] for k in ("p_k", "p_v", "p_shift", "p_s", "p_conv", "p_h",
                            "s_k", "s_v", "s_shift", "s_s", "s_conv", "s_h")}
    mix = None
    wo = None
    y_prompt = y_sample = None
    for l in range(depth):
        row1 = lambda v: v[l].reshape(1, -1)
        lp = dict(rwkv_mu=row1(rwkv_mu), rwkv_w0=row1(rwkv_w0), rwkv_w2=rwkv_w2[l], rwkv_a0=row1(rwkv_a0),
                  rwkv_a2=rwkv_a2[l], rwkv_g2=rwkv_g2[l], rwkv_k_k=row1(rwkv_k_k), rwkv_k_a=row1(rwkv_k_a),
                  rwkv_r_k=row1(rwkv_r_k), rwkv_ln_w=row1(rwkv_ln_w), rwkv_ln_b=row1(rwkv_ln_b),
                  ssm_conv_w=ssm_conv_w[l], ssm_conv_b=row1(ssm_conv_b),
                  ssm_dt_bias=_pad_lanes(row1(ssm_dt_bias), LANES), ssm_A_log=_pad_lanes(row1(ssm_A_log), LANES),
                  ssm_D=_pad_lanes(row1(ssm_D), LANES), ssm_norm_w=row1(ssm_norm_w))
        bf = lambda w: w[l].astype(BF16)
        if l == 0:
            x = _ffn_call(x, None, None, row1(ffn1_norm), bf(ffn1_w_gate), bf(ffn1_w_up), bf(ffn1_w_down), None,
                          n_ptiles)
        (p_rwkv, q3, k3, v3, k_cp, v_cp, z, xbc, dt, pr_t, q_s, k_s, v_s, z_t, xbc_t, dt_t) = _proj_call(
            x, row1(mix_norm), _pad_lanes(w_in[l], PROJ_PAD).astype(BF16), n_ptiles, tp)

        ya_p, s_p, last_p = _rwkv_call(p_rwkv, zeros_shift, zeros_s, lp, bp, tp, tp)
        yb_p = _att_prompt_call(q3, k3, v3, row1(att_norm_w), bp, tp)
        yc_p, h_p, conv_p = _ssd_call(z, xbc, dt, zeros_conv, zeros_h, lp, bp, tp, tp, SSD_CHUNK)

        def sample_rows(a, t_pad):
            return jnp.pad(a[:ms].reshape(bs, ts, -1), ((0, 0), (0, t_pad - ts), (0, 0)))

        pr_s = sample_rows(pr_t, ts_rwkv)
        ya_s, s_s, last_s = _rwkv_call(pr_s.reshape(bs * ts_rwkv, RWKV_PROJ), state_rwkv_shift[l], state_rwkv[l], lp,
                               bs, ts_rwkv, ts)
        yb_s = _att_sample_call(sample_rows(q_s, 8), sample_rows(k_s, 8), sample_rows(v_s, 8),
                                cache_kt, cache_vt, l, row1(att_norm_w), ts)
        conv8 = jnp.pad(state_ssm_conv[l], ((0, 0), (8 - (SSM_CONV - 1), 0), (0, 0)))
        flat8 = lambda a: sample_rows(a, ts_ssd).reshape(bs * ts_ssd, -1)
        yc_s, h_s, conv_s = _ssd_call(flat8(z_t), flat8(xbc_t), flat8(dt_t), conv8, state_ssm[l], lp, bs, ts_ssd, ts,
                                      ts_ssd)

        def tile_rows(a, t_pad):
            return _pad_rows(a.reshape(bs, t_pad, -1)[:, :ts].reshape(ms, -1), ROW_TILE)

        mix = ((ya_p, tile_rows(ya_s, ts_rwkv)), (yb_p, tile_rows(yb_s, 8)), (yc_p, tile_rows(yc_s, ts_ssd)))
        wo = w_out[l].astype(BF16)

        keep = min(2048, tp)
        from_cp = lambda a: jnp.transpose(a.reshape(bp, ATT_HEADS, HEAD_DIM, tp), (0, 3, 1, 2))[:, tp - keep:]
        outs["p_k"].append(from_cp(k_cp))
        outs["p_v"].append(from_cp(v_cp))
        outs["p_shift"].append(last_p[:, 0])
        outs["p_s"].append(s_p)
        outs["p_conv"].append(conv_p[:, 8 - (SSM_CONV - 1):])
        outs["p_h"].append(h_p)
        outs["s_k"].append(k_s[:ms].reshape(bs, ts, ATT_HEADS, HEAD_DIM))
        outs["s_v"].append(v_s[:ms].reshape(bs, ts, ATT_HEADS, HEAD_DIM))
        outs["s_shift"].append(last_s[:, 0])
        outs["s_s"].append(s_s)
        outs["s_conv"].append(conv_s[:, 8 - (SSM_CONV - 1):])
        outs["s_h"].append(h_s)

        x = _ffn_call(x, mix, wo, row1(ffn2_norm), bf(ffn2_w_gate), bf(ffn2_w_up), bf(ffn2_w_down),
                      final_norm.reshape(1, -1) if l == depth - 1 else None, n_ptiles)
        if l == depth - 1:
            y_prompt, y_sample = x
        else:
            lnext = l + 1
            x = _ffn_call(x, None, None, ffn1_norm[lnext].reshape(1, -1), ffn1_w_gate[lnext].astype(BF16),
                          ffn1_w_up[lnext].astype(BF16), ffn1_w_down[lnext].astype(BF16), None, n_ptiles)

    st = lambda k: jnp.stack(outs[k])
    return (y_prompt.reshape(bp, tp, D_MODEL), y_sample[:ms].reshape(bs, ts, D_MODEL),
            st("p_k"), st("p_v"), st("p_shift"), st("p_s"), st("p_conv"), st("p_h"),
            st("s_k"), st("s_v"), st("s_shift"), st("s_s"), st("s_conv"), st("s_h"))
```

```python
import functools

import jax
import jax.numpy as jnp
from jax import lax
from jax.experimental import pallas as pl
from jax.experimental.pallas import tpu as pltpu

F32 = jnp.float32
BF16 = jnp.bfloat16

D_MODEL = 1024
FFN_DIM = 2816
HEAD_DIM = 64
RWKV_HEADS = 4
RWKV_DIM = RWKV_HEADS * HEAD_DIM
RWKV_PROJ = 1024
RWKV_GN_EPS = 64e-5
ATT_HEADS = 6
ATT_DIM = ATT_HEADS * HEAD_DIM
ATT_PAIRS = ATT_HEADS // 2
DILATIONS = (1, 4, 16)
ATT_STEPS = 128
SSM_HEADS = 6
SSM_DIM = SSM_HEADS * HEAD_DIM
SSM_GROUPS = 2
SSM_STATE = 128
SSM_CONV = 4
SSM_CONV_DIM = SSM_DIM + 2 * SSM_GROUPS * SSM_STATE
PROJ_DIM = RWKV_PROJ + 3 * ATT_DIM + SSM_DIM + SSM_CONV_DIM + SSM_HEADS
PROJ_PAD = 3584
NORM_EPS = 1e-6
NEG_INF = -1e30
LOG2E = 1.4426950408889634

LANES = 128
ROW_TILE = 512
FFN_CHUNK = 256
RWKV_CHUNK = 64
RWKV_GROUP = 8
SSD_CHUNK = 128
SSD_GROUP = 4
VMEM_LIMIT = 56 * 1024 * 1024

OFF_ATT = RWKV_PROJ
OFF_Z = OFF_ATT + 3 * ATT_DIM
OFF_XBC = OFF_Z + SSM_DIM
OFF_DT = OFF_XBC + SSM_CONV_DIM


def _params(sem):
    return pltpu.CompilerParams(dimension_semantics=sem, vmem_limit_bytes=VMEM_LIMIT)


def _dot(a, b):
    return jnp.dot(a.astype(BF16), b.astype(BF16), preferred_element_type=F32)


def _dot_nt(a, b):
    return lax.dot_general(a.astype(BF16), b.astype(BF16), (((1,), (1,)), ((), ())),
                           preferred_element_type=F32)


def _dot_tn(a, b):
    return lax.dot_general(a.astype(BF16), b.astype(BF16), (((0,), (0,)), ((), ())),
                           preferred_element_type=F32)


def _dot_hi(a, b):
    return jnp.dot(a, b, preferred_element_type=F32, precision=lax.Precision.HIGHEST)


def _rms(x, w):
    return x * lax.rsqrt(jnp.mean(x * x, axis=-1, keepdims=True) + NORM_EPS) * w


def _sigmoid(x):
    return 1.0 / (1.0 + jnp.exp(-x))


def _softplus(x):
    return jnp.maximum(x, 0.0) + jnp.log(1.0 + jnp.exp(-jnp.abs(x)))


def _div(x, n):
    assert n & (n - 1) == 0
    return lax.shift_right_logical(x, n.bit_length() - 1)


def _mod(x, n):
    assert n & (n - 1) == 0
    return x & (n - 1)


def _cumsum_rows(x):
    n = x.shape[0]
    row = lax.broadcasted_iota(jnp.int32, x.shape, 0)
    s = 1
    while s < n:
        x = x + jnp.where(row >= s, pltpu.roll(x, s, axis=0), 0.0)
        s *= 2
    return x


def _ffn_body(has_mix, final, n_prompt_tiles, *refs):
    it = iter(refs)
    x_ref = next(it)
    if has_mix:
        yap, yas, ybp, ybs, ycp, ycs, wo_ref = (next(it) for _ in range(7))
    nw_ref, wg_ref, wu_ref, wd_ref = (next(it) for _ in range(4))
    if final:
        fn_ref = next(it)
        op_ref, os_ref = next(it), next(it)
    else:
        o_ref = next(it)
    h_ref, acc_ref = next(it), next(it)
    i = pl.program_id(0)
    is_sample = i >= n_prompt_tiles

    x = x_ref[...]
    if has_mix:
        ya = jnp.where(is_sample, yas[...], yap[...])
        yb = jnp.where(is_sample, ybs[...], ybp[...])
        yc = jnp.where(is_sample, ycs[...], ycp[...])
        x = (x + _dot(ya, wo_ref[0:RWKV_DIM, :])
             + _dot(yb, wo_ref[RWKV_DIM:RWKV_DIM + ATT_DIM, :])
             + _dot(yc, wo_ref[RWKV_DIM + ATT_DIM:, :]))
    h_ref[...] = _rms(x, nw_ref[...]).astype(BF16)
    acc_ref[...] = x

    def chunk(c, carry):
        off = pl.multiple_of(c * FFN_CHUNK, FFN_CHUNK)
        h = h_ref[...]
        g = jnp.dot(h, wg_ref[:, pl.ds(off, FFN_CHUNK)], preferred_element_type=F32)
        u = jnp.dot(h, wu_ref[:, pl.ds(off, FFN_CHUNK)], preferred_element_type=F32)
        a = (g * _sigmoid(g) * u).astype(BF16)
        acc_ref[...] += 0.5 * jnp.dot(a, wd_ref[pl.ds(off, FFN_CHUNK), :], preferred_element_type=F32)
        return carry

    lax.fori_loop(0, FFN_DIM // FFN_CHUNK, chunk, 0)
    if final:
        y = _rms(acc_ref[...], fn_ref[...])

        @pl.when(jnp.logical_not(is_sample))
        def _():
            op_ref[...] = y

        @pl.when(is_sample)
        def _():
            os_ref[...] = y
    else:
        o_ref[...] = acc_ref[...]


def _ffn_call(x, mix, wo, nw, wg, wu, wd, final_norm, n_prompt_tiles, layer):
    m_pad = x.shape[0]
    tm = ROW_TILE
    has_mix = mix is not None
    final = final_norm is not None
    last_p = n_prompt_tiles - 1
    row = lambda i: (i, 0)
    fixed = lambda i: (0, 0)
    prow = lambda i: (jnp.minimum(i, last_p), 0)
    resident = lambda a, b: pl.BlockSpec((None, a, b), lambda i: (layer, 0, 0), pipeline_mode=pl.Buffered(1))
    in_specs = [pl.BlockSpec((tm, D_MODEL), row)]
    args = [x]
    if has_mix:
        for yp, ys in mix:
            w = yp.shape[1]
            in_specs += [pl.BlockSpec((tm, w), prow), pl.BlockSpec((tm, w), fixed)]
            args += [yp, ys]
        in_specs.append(resident(D_MODEL, D_MODEL))
        args.append(wo)
    in_specs += [pl.BlockSpec((1, D_MODEL), fixed), resident(D_MODEL, FFN_DIM),
                 resident(D_MODEL, FFN_DIM), resident(FFN_DIM, D_MODEL)]
    args += [nw, wg, wu, wd]
    if final:
        in_specs.append(pl.BlockSpec((1, D_MODEL), fixed))
        args.append(final_norm)
        out_shape = (jax.ShapeDtypeStruct((n_prompt_tiles * tm, D_MODEL), F32),
                     jax.ShapeDtypeStruct((tm, D_MODEL), F32))
        out_specs = (pl.BlockSpec((tm, D_MODEL), prow), pl.BlockSpec((tm, D_MODEL), fixed))
    else:
        out_shape = jax.ShapeDtypeStruct((m_pad, D_MODEL), F32)
        out_specs = pl.BlockSpec((tm, D_MODEL), row)
    return pl.pallas_call(
        functools.partial(_ffn_body, has_mix, final, n_prompt_tiles),
        grid=(m_pad // tm,), in_specs=in_specs, out_specs=out_specs, out_shape=out_shape,
        scratch_shapes=[pltpu.VMEM((tm, D_MODEL), BF16), pltpu.VMEM((tm, D_MODEL), F32)],
        compiler_params=_params(("arbitrary",)), name="ffn")(*args)


def _proj_body(n_prompt_tiles, x_ref, nw_ref, w_ref, pr_ref, q3_ref, k3_ref, v3_ref, kn_ref, vn_ref, z_ref, xbc_ref,
               dt_ref, prs_ref, qs_ref, ks_ref, vs_ref, zs_ref, xbcs_ref, dts_ref):
    h = _rms(x_ref[...], nw_ref[...]).astype(BF16)

    def seg(a, b):
        return jnp.dot(h, w_ref[:, a:b], preferred_element_type=F32)

    pr = seg(0, OFF_ATT)
    q = seg(OFF_ATT, OFF_ATT + ATT_DIM)
    k = seg(OFF_ATT + ATT_DIM, OFF_ATT + 2 * ATT_DIM)
    v = seg(OFF_ATT + 2 * ATT_DIM, OFF_Z)
    z = seg(OFF_Z, OFF_XBC)
    xbc = seg(OFF_XBC, OFF_DT)
    dt = seg(OFF_DT, PROJ_PAD)
    pr_ref[...] = pr
    z_ref[...] = z
    xbc_ref[...] = xbc
    dt_ref[...] = dt
    for j in range(ATT_PAIRS):
        q3_ref[j] = q[:, j * LANES:(j + 1) * LANES]
        k3_ref[j] = k[:, j * LANES:(j + 1) * LANES]
        v3_ref[j] = v[:, j * LANES:(j + 1) * LANES]
    prs_ref[...] = pr
    qs_ref[...] = q
    ks_ref[...] = k
    vs_ref[...] = v
    zs_ref[...] = z
    xbcs_ref[...] = xbc
    dts_ref[...] = dt

    @pl.when(pl.program_id(0) < n_prompt_tiles)
    def _():
        kn_ref[0] = k.T
        vn_ref[0] = v.T


def _proj_call(x, nw, w, n_prompt_tiles, t_prompt, layer):
    m_pad = x.shape[0]
    tm = ROW_TILE
    mp = n_prompt_tiles * tm
    tiles_per_seq = t_prompt // tm
    assert m_pad == mp + tm
    row = lambda i: (i, 0)
    fixed = lambda i: (0, 0)
    row3 = lambda i: (0, i, 0)
    last_p = n_prompt_tiles - 1
    col = lambda i: (jnp.minimum(i, last_p) // tiles_per_seq, 0, jnp.minimum(i, last_p) % tiles_per_seq)
    sds = jax.ShapeDtypeStruct
    widths = (RWKV_PROJ, ATT_DIM, ATT_DIM, ATT_DIM, SSM_DIM, SSM_CONV_DIM, LANES)
    n_seq = mp // t_prompt
    out_shape = ((sds((m_pad, RWKV_PROJ), F32),
                  sds((ATT_PAIRS, m_pad, LANES), F32), sds((ATT_PAIRS, m_pad, LANES), F32),
                  sds((ATT_PAIRS, m_pad, LANES), F32),
                  sds((n_seq, ATT_DIM, t_prompt), F32), sds((n_seq, ATT_DIM, t_prompt), F32),
                  sds((m_pad, SSM_DIM), F32), sds((m_pad, SSM_CONV_DIM), F32), sds((m_pad, LANES), F32))
                 + tuple(sds((tm, w_), F32) for w_ in widths))
    p3 = pl.BlockSpec((ATT_PAIRS, tm, LANES), row3)
    out_specs = ((pl.BlockSpec((tm, RWKV_PROJ), row), p3, p3, p3,
                  pl.BlockSpec((1, ATT_DIM, tm), col), pl.BlockSpec((1, ATT_DIM, tm), col),
                  pl.BlockSpec((tm, SSM_DIM), row), pl.BlockSpec((tm, SSM_CONV_DIM), row),
                  pl.BlockSpec((tm, LANES), row))
                 + tuple(pl.BlockSpec((tm, w_), fixed) for w_ in widths))
    return pl.pallas_call(
        functools.partial(_proj_body, n_prompt_tiles),
        grid=(m_pad // tm,),
        in_specs=[pl.BlockSpec((tm, D_MODEL), row), pl.BlockSpec((1, D_MODEL), fixed),
                  pl.BlockSpec((None, D_MODEL, PROJ_PAD), lambda i: (layer, 0, 0), pipeline_mode=pl.Buffered(1))],
        out_specs=out_specs, out_shape=out_shape,
        compiler_params=_params(("arbitrary",)), name="proj")(x, nw, w)


def _rwkv_body(chunk, group, by_batch, t_valid, t_total, p_ref, sh_ref, s0_ref, mu_ref, w0_ref, w2_ref, a0_ref, a2_ref,
               g2_ref, kk_ref, ka_ref, rk_ref, lnw_ref, lnb_ref, y_ref, so_ref, last_ref, sb_ref, prev_ref):
    C = chunk
    G = group
    H = RWKV_HEADS
    N = HEAD_DIM
    R = RWKV_DIM
    GC = G * C
    c = pl.program_id(1)
    nc = pl.num_programs(1)
    p = p_ref[...]
    row_p = lax.broadcasted_iota(jnp.int32, p.shape, 0)
    p_prev = pltpu.roll(p, 1, axis=0)

    if by_batch:
        for gi in range(G):
            p_prev = jnp.where(row_p == gi * C, sh_ref[gi], p_prev)
        zero = jnp.zeros((N, N), F32)
        states = [jnp.concatenate([jnp.concatenate([s0_ref[gi, h] if hh == h else zero for hh in range(H)], axis=1)
                                   for h in range(H)], axis=0) for gi in range(G)]
    else:
        @pl.when(c == 0)
        def _():
            sb_ref[...] = jnp.zeros((R, R), F32)
            for h in range(H):
                sb_ref[h * N:(h + 1) * N, h * N:(h + 1) * N] = s0_ref[0, h]
            prev_ref[...] = jnp.broadcast_to(sh_ref[0], (8, RWKV_PROJ))

        p_prev = jnp.where(row_p == 0, prev_ref[0:1, :], p_prev)
        prev_ref[...] = jnp.broadcast_to(p[GC - 1:GC, :], (8, RWKV_PROJ))
    ps = p + mu_ref[...] * (p_prev - p)
    r = ps[:, 0:R]
    k = ps[:, R:2 * R]
    v = ps[:, 2 * R:3 * R]
    w_lr = ps[:, 3 * R:3 * R + 64]
    a_lr = ps[:, 3 * R + 64:3 * R + 128]
    g_lr = ps[:, 3 * R + 128:]

    w_log = -_softplus(-(w0_ref[...] + _dot(jnp.tanh(w_lr), w2_ref[...]))) - 0.5
    ld = -jnp.exp(w_log)
    a = _sigmoid(a0_ref[...] + _dot(a_lr, a2_ref[...]))
    g = _dot(_sigmoid(g_lr), g2_ref[...])

    lane_r = lax.broadcasted_iota(jnp.int32, (R, R), 1)
    row_r = lax.broadcasted_iota(jnp.int32, (R, R), 0)
    own_head = _div(lane_r, N) == _div(row_r, N)
    head_ones = jnp.where(own_head, 1.0, 0.0).astype(BF16)

    def head_sum(x):
        hi = x.astype(BF16)
        lo = (x - hi.astype(F32)).astype(BF16)
        return (jnp.dot(hi, head_ones, preferred_element_type=F32)
                + jnp.dot(lo, head_ones, preferred_element_type=F32))

    kk = k * kk_ref[...]
    kk = kk / jnp.maximum(jnp.sqrt(head_sum(kk * kk)), 1e-12)
    kp = k * (1.0 + (a - 1.0) * ka_ref[...])
    if t_valid < t_total:
        row_r2 = lax.broadcasted_iota(jnp.int32, (GC, R), 0)
        live = (_mod(row_r2, C) if by_batch else c * GC + row_r2) < t_valid
        ld = jnp.where(live, ld, 0.0)
        kk = jnp.where(live, kk, 0.0)
        kp = jnp.where(live, kp, 0.0)
        v = jnp.where(live, v, 0.0)

    HC = H * C
    bd_row = _div(lax.broadcasted_iota(jnp.int32, (HC, R), 0), C)
    bd_lane = _div(lax.broadcasted_iota(jnp.int32, (HC, R), 1), N)

    def bd_rows(x):
        return jnp.where(bd_row == bd_lane, jnp.concatenate([x] * H, axis=0), 0.0)

    sq_row = _div(lax.broadcasted_iota(jnp.int32, (HC, HC), 0), C)
    sq_col = _div(lax.broadcasted_iota(jnp.int32, (HC, HC), 1), C)

    def mm(x, y):
        return _dot(x, jnp.where(sq_row == sq_col, jnp.concatenate([y] * H, axis=0), 0.0))

    t_idx = lax.broadcasted_iota(jnp.int32, (C, HC), 0)
    s_idx = _mod(lax.broadcasted_iota(jnp.int32, (C, HC), 1), C)
    strict = s_idx < t_idx
    incl = s_idx <= t_idx
    eye = jnp.where(s_idx == t_idx, 1.0, 0.0).astype(F32)
    same16 = _div(s_idx, 16) == _div(t_idx, 16)

    chunks = [slice(gi * C, (gi + 1) * C) for gi in range(G)]
    cum = [_cumsum_rows(ld[rows]) for rows in chunks]
    w_inc = [jnp.exp(x) for x in cum]
    w_inv = [jnp.exp(-x) for x in cum]
    w_end = [x[C - 1:C, :] for x in w_inc]
    b_t = [kk[rows] * a[rows] * wi for rows, wi in zip(chunks, w_inv)]
    k_t = [kp[rows] * wi for rows, wi in zip(chunks, w_inv)]
    lhs = [jnp.concatenate([kk[rows] * jnp.exp(cm - ld[rows]), r[rows] * wi], axis=0)
           for rows, cm, wi in zip(chunks, cum, w_inc)]
    sc = [_dot_nt(x, jnp.concatenate([bd_rows(bt), bd_rows(kt)], axis=0)) for x, bt, kt in zip(lhs, b_t, k_t)]
    a_ab = [jnp.where(strict, x[0:C, 0:HC], 0.0) for x in sc]
    a_ak = [jnp.where(strict, x[0:C, HC:], 0.0) for x in sc]
    p_rb = [jnp.where(incl, x[C:, 0:HC], 0.0) for x in sc]
    p_rk = [jnp.where(incl, x[C:, HC:], 0.0) for x in sc]
    e1 = [jnp.where(same16, -x, 0.0) for x in a_ab]
    n_low = [x + y for x, y in zip(a_ab, e1)]
    e2 = [mm(x, x) for x in e1]
    st = [mm(jnp.concatenate([eye + x, y], axis=0), y) for x, y in zip(e1, e2)]
    p4 = [eye + x + y[0:C] for x, y in zip(e1, st)]
    e4 = [y[C:] for y in st]
    st = [mm(jnp.concatenate([x, y], axis=0), y) for x, y in zip(p4, e4)]
    p8 = [x + y[0:C] for x, y in zip(p4, st)]
    e8 = [y[C:] for y in st]
    d_inv = [x + mm(x, y) for x, y in zip(p8, e8)]
    f1 = [-mm(x, y) for x, y in zip(d_inv, n_low)]
    f2 = [mm(x, x) for x in f1]
    q3 = [eye + x + mm(eye + x, y) for x, y in zip(f1, f2)]
    t_inv = [mm(x, y) for x, y in zip(q3, d_inv)]
    v_c = [v[rows] for rows in chunks]
    v_bd = [bd_rows(x) for x in v_c]
    avp = [_dot(jnp.concatenate([x, y], axis=0), z) for x, y, z in zip(a_ak, p_rk, v_bd)]
    tail = [jnp.concatenate([bt * we, kt * we], axis=0) for bt, kt, we in zip(b_t, k_t, w_end)]

    def chain(gi, sb):
        ks_rs = _dot_nt(lhs[gi], sb)
        u = _dot(t_inv[gi], bd_rows(-ks_rs[0:C] - avp[gi][0:C]))
        upd = _dot_tn(jnp.concatenate([u, v_c[gi]], axis=0), tail[gi])
        y_c = ks_rs[C:] + _dot(p_rb[gi], bd_rows(u)) + avp[gi][C:]
        return jnp.where(own_head, sb * w_end[gi] + upd, 0.0), y_c

    ys = []
    if by_batch:
        for gi in range(G):
            sb, y_c = chain(gi, states[gi])
            ys.append(y_c)
            for h in range(H):
                so_ref[gi, h] = sb[h * N:(h + 1) * N, h * N:(h + 1) * N]
    else:
        sb = sb_ref[...]
        for gi in range(G):
            sb, y_c = chain(gi, sb)
            ys.append(y_c)
        sb_ref[...] = sb
    y = jnp.concatenate(ys, axis=0) if G > 1 else ys[0]

    yc = y - head_sum(y) * (1.0 / N)
    var = head_sum(yc * yc) * (1.0 / N)
    yn = yc * lax.rsqrt(var + RWKV_GN_EPS) * lnw_ref[...] + lnb_ref[...]
    bonus = head_sum(r * kp * rk_ref[...]) * v
    y_ref[...] = (yn + bonus) * g
    if by_batch:
        for gi in range(G):
            last_ref[gi] = jnp.broadcast_to(p[gi * C + t_valid - 1:gi * C + t_valid, :], (8, RWKV_PROJ))
    else:
        @pl.when(c == nc - 1)
        def _():
            last_ref[0] = prev_ref[...]
            for h in range(H):
                so_ref[0, h] = sb_ref[h * N:(h + 1) * N, h * N:(h + 1) * N]


def _rwkv_call(p, shift_prev, s0, lp, n_batch, t_total, t_valid):
    C = RWKV_CHUNK
    by_batch = t_total == C
    G = RWKV_GROUP
    assert (n_batch if by_batch else t_total // C) % G == 0
    nc = 1 if by_batch else t_total // (G * C)
    n_outer = n_batch // G if by_batch else n_batch
    gb = G if by_batch else 1
    vec = lambda n: pl.BlockSpec((1, n), lambda b, c: (0, 0))
    mat = lambda a, n: pl.BlockSpec((a, n), lambda b, c: (0, 0))
    in_specs = [pl.BlockSpec((G * C, RWKV_PROJ), lambda b, c: (b * nc + c, 0)),
                pl.BlockSpec((gb, 1, RWKV_PROJ), lambda b, c: (b, 0, 0)),
                pl.BlockSpec((gb, RWKV_HEADS, HEAD_DIM, HEAD_DIM), lambda b, c: (b, 0, 0, 0)),
                vec(RWKV_PROJ), vec(RWKV_DIM), mat(64, RWKV_DIM), vec(RWKV_DIM), mat(64, RWKV_DIM),
                mat(128, RWKV_DIM), vec(RWKV_DIM), vec(RWKV_DIM), vec(RWKV_DIM), vec(RWKV_DIM), vec(RWKV_DIM)]
    out_shape = (jax.ShapeDtypeStruct((n_batch * t_total, RWKV_DIM), F32),
                 jax.ShapeDtypeStruct((n_batch, RWKV_HEADS, HEAD_DIM, HEAD_DIM), F32),
                 jax.ShapeDtypeStruct((n_batch, 8, RWKV_PROJ), F32))
    out_specs = (pl.BlockSpec((G * C, RWKV_DIM), lambda b, c: (b * nc + c, 0)),
                 pl.BlockSpec((gb, RWKV_HEADS, HEAD_DIM, HEAD_DIM), lambda b, c: (b, 0, 0, 0)),
                 pl.BlockSpec((gb, 8, RWKV_PROJ), lambda b, c: (b, 0, 0)))
    return pl.pallas_call(
        functools.partial(_rwkv_body, C, G, by_batch, t_valid, t_total),
        grid=(n_outer, nc), in_specs=in_specs, out_specs=out_specs, out_shape=out_shape,
        scratch_shapes=[pltpu.VMEM((RWKV_DIM, RWKV_DIM), F32), pltpu.VMEM((8, RWKV_PROJ), F32)],
        compiler_params=_params(("arbitrary", "arbitrary")), name="rwkv")(
            p, shift_prev.reshape(n_batch, 1, RWKV_PROJ), s0,
            lp["rwkv_mu"], lp["rwkv_w0"], lp["rwkv_w2"], lp["rwkv_a0"], lp["rwkv_a2"], lp["rwkv_g2"],
            lp["rwkv_k_k"], lp["rwkv_k_a"], lp["rwkv_r_k"], lp["rwkv_ln_w"], lp["rwkv_ln_b"])


def _alibi_slope(h):
    return 2.0 ** (-8.0 * (h + 1) / ATT_HEADS)


def _att_prompt_body(t_len, q_ref, k_ref, v_ref, nw_ref, o_ref, acc_ref, m_ref, l_ref, bias_ref):
    blk = ATT_STEPS
    pairs = range(ATT_PAIRS)
    lane_top = lax.broadcasted_iota(jnp.int32, (blk, LANES), 1) < HEAD_DIM

    @pl.when(pl.program_id(0) == 0)
    def _():
        qi = _mod(lax.broadcasted_iota(jnp.int32, (2 * blk, 2 * blk), 0), blk)
        kj = lax.broadcasted_iota(jnp.int32, (2 * blk, 2 * blk), 1)
        top = lax.broadcasted_iota(jnp.int32, (2 * blk, 2 * blk), 0) < blk
        for di, d in enumerate(DILATIONS):
            for off in range(2):
                delta = off * blk + qi - kj
                valid = (delta >= 0) & (delta <= blk)
                dist = (d * delta).astype(F32) * LOG2E
                for j in pairs:
                    slope = jnp.where(top, _alibi_slope(2 * j), _alibi_slope(2 * j + 1))
                    bias_ref[di, off, j] = jnp.where(valid, -slope * dist, NEG_INF)

    order = tuple(reversed(tuple(enumerate(DILATIONS))))
    for di, d in order:
        n = t_len // d
        kw = min(2 * blk, n)
        nb = n // blk

        def step(it, carry, di=di, d=d, kw=kw, nb=nb):
            res = it // nb
            b = it % nb
            kb = jnp.maximum(b - 1, 0) if kw > blk else b
            q_rows = pl.ds(res + d * blk * b, blk, stride=d)
            k_rows = pl.ds(res + d * blk * kb, kw, stride=d)
            q2 = [q_ref.at[j][q_rows, :] * (HEAD_DIM ** -0.5 * LOG2E) for j in pairs]
            k2 = [k_ref.at[j][k_rows, :] for j in pairs]
            v2 = [v_ref.at[j][k_rows, :] for j in pairs]
            q_st = [jnp.concatenate([jnp.where(lane_top, x, 0.0), jnp.where(lane_top, 0.0, x)], axis=0) for x in q2]
            s = [_dot_nt(x, y) + bias_ref[di, b - kb, j, :, 0:kw] for j, x, y in zip(pairs, q_st, k2)]
            mh = [jnp.max(x, axis=-1, keepdims=True) for x in s]
            pr = [jnp.exp2(x - m) for x, m in zip(s, mh)]
            lh = [jnp.sum(x, axis=-1, keepdims=True) for x in pr]
            ah = [_dot(x, y) for x, y in zip(pr, v2)]

            def unstack(x):
                x = jnp.broadcast_to(x, (2 * blk, LANES))
                return jnp.where(lane_top, x[0:blk], x[blk:])

            results = []
            for j in pairs:
                acc, m, l = unstack(ah[j]), unstack(mh[j]), unstack(lh[j])
                if d != order[0][1]:
                    m_old = m_ref.at[j][q_rows, :]
                    m_new = jnp.maximum(m_old, m)
                    e_old = jnp.exp2(m_old - m_new)
                    e_new = jnp.exp2(m - m_new)
                    acc = acc_ref.at[j][q_rows, :] * e_old + acc * e_new
                    l = l_ref.at[j][q_rows, :] * e_old + l * e_new
                    m = m_new
                results.append((acc, m, l))
            if d != order[-1][1]:
                for j, (acc, m, l) in enumerate(results):
                    acc_ref.at[j][q_rows, :] = acc
                    m_ref.at[j][q_rows, :] = m
                    l_ref.at[j][q_rows, :] = l
            else:
                assert d == 1
                ys = [acc / l for acc, _, l in results]
                ms = sum(jnp.sum(y * y, axis=-1, keepdims=True) for y in ys) * (1.0 / ATT_DIM)
                scale = lax.rsqrt(ms + NORM_EPS)
                rows = pl.ds(pl.multiple_of(blk * b, blk), blk)
                for j in pairs:
                    o_ref[rows, j * LANES:(j + 1) * LANES] = ys[j] * scale * nw_ref[:, j * LANES:(j + 1) * LANES]
            return carry

        lax.fori_loop(0, d * nb, step, 0)


def _att_prompt_call(q3, k3, v3, nw, n_batch, t_len):
    blk3 = pl.BlockSpec((ATT_PAIRS, t_len, LANES), lambda b: (0, b, 0))
    return pl.pallas_call(
        functools.partial(_att_prompt_body, t_len),
        grid=(n_batch,),
        in_specs=[blk3, blk3, blk3, pl.BlockSpec((1, ATT_DIM), lambda b: (0, 0))],
        out_specs=pl.BlockSpec((t_len, ATT_DIM), lambda b: (b, 0)),
        out_shape=jax.ShapeDtypeStruct((n_batch * t_len, ATT_DIM), F32),
        scratch_shapes=[pltpu.VMEM((ATT_PAIRS, t_len, LANES), F32)] * 3
        + [pltpu.VMEM((len(DILATIONS), 2, ATT_PAIRS, 2 * ATT_STEPS, 2 * ATT_STEPS), F32)],
        compiler_params=_params(("arbitrary",)), name="att_prompt")(q3, k3, v3, nw)


def _multiplicity(dist):
    mult = jnp.zeros(dist.shape, F32)
    for d in DILATIONS:
        hit = (_mod(dist, d) == 0) & (dist <= d * ATT_STEPS)
        mult = mult + jnp.where(hit, 1.0, 0.0)
    return mult


def _att_sample_body(t_new, w_buf, q_ref, kn_ref, vn_ref, kt_ref, vt_ref, nw_ref, o_ref):
    RQ = 8
    rows = ATT_HEADS * RQ
    q = q_ref[0] * (HEAD_DIM ** -0.5)
    head_of_row = _div(lax.broadcasted_iota(jnp.int32, (rows, ATT_DIM), 0), RQ)
    head_of_lane = _div(lax.broadcasted_iota(jnp.int32, (rows, ATT_DIM), 1), HEAD_DIM)
    own = head_of_row == head_of_lane
    q_bd = jnp.where(own, jnp.concatenate([q] * ATT_HEADS, axis=0), 0.0)

    s_buf = _dot(q_bd, kt_ref[0, 0])
    s_new = _dot_nt(q_bd, kn_ref[0])

    def weights(s, key_pos):
        t = _mod(lax.broadcasted_iota(jnp.int32, s.shape, 0), RQ)
        h = _div(lax.broadcasted_iota(jnp.int32, s.shape, 0), RQ)
        dist = w_buf + t - key_pos
        slope = sum(jnp.where(h == hh, _alibi_slope(hh), 0.0) for hh in range(ATT_HEADS))
        mult = jnp.where(dist >= 0, _multiplicity(jnp.maximum(dist, 0)), 0.0)
        return jnp.where(mult > 0, s - slope * dist.astype(F32), NEG_INF), mult

    sb, mult_b = weights(s_buf, lax.broadcasted_iota(jnp.int32, s_buf.shape, 1))
    sn, mult_n = weights(s_new, w_buf + lax.broadcasted_iota(jnp.int32, s_new.shape, 1))
    m = jnp.maximum(jnp.max(sb, axis=-1, keepdims=True), jnp.max(sn, axis=-1, keepdims=True))
    pb = mult_b * jnp.exp(sb - m)
    pn = mult_n * jnp.exp(sn - m)
    den = jnp.sum(pb, axis=-1, keepdims=True) + jnp.sum(pn, axis=-1, keepdims=True)
    num = _dot_nt(pb, vt_ref[0, 0]) + _dot(pn, vn_ref[0])
    out = jnp.where(own, num / den, 0.0).reshape(ATT_HEADS, RQ, ATT_DIM)
    o_ref[0] = _rms(jnp.sum(out, axis=0), nw_ref[...])


def _att_sample_call(q, k_new, v_new, cache_kt, cache_vt, layer, nw, t_new):
    depth, n_batch, _, w_buf = cache_kt.shape
    small = pl.BlockSpec((1, 8, ATT_DIM), lambda b: (b, 0, 0))
    big = pl.BlockSpec((1, 1, ATT_DIM, w_buf), lambda b: (layer, b, 0, 0))
    return pl.pallas_call(
        functools.partial(_att_sample_body, t_new, w_buf),
        grid=(n_batch,),
        in_specs=[small, small, small, big, big, pl.BlockSpec((1, ATT_DIM), lambda b: (0, 0))],
        out_specs=small, out_shape=jax.ShapeDtypeStruct((n_batch, 8, ATT_DIM), F32),
        compiler_params=_params(("arbitrary",)), name="att_sample")(q, k_new, v_new, cache_kt, cache_vt, nw)


def _ssd_body(chunk, group, t_valid, t_total, z_ref, xbc_ref, dt_ref, cp_ref, h0_ref, cw_ref, cb_ref, dtb_ref,
              alog_ref, dskip_ref, nw_ref, y_ref, ho_ref, cv_ref, hs_ref, xc_ref):
    L = chunk
    G = group
    GL = G * L
    P = HEAD_DIM
    NS = SSM_STATE
    c = pl.program_id(1)
    nc = pl.num_programs(1)

    @pl.when(c == 0)
    def _():
        hs_ref[...] = h0_ref[0]
        xc_ref[0:8, :] = cp_ref[0]

    xc_ref[8:8 + GL, :] = xbc_ref[...]
    conv = cb_ref[...]
    for i in range(SSM_CONV):
        conv = conv + xc_ref[5 + i:5 + i + GL, :] * cw_ref[i:i + 1, :]
    last_valid = t_valid - (t_total // GL - 1) * GL
    conv_tail = xc_ref[last_valid:last_valid + 8, :]
    xc_ref[0:8, :] = xc_ref[GL:GL + 8, :]
    act = conv * _sigmoid(conv)
    xs = act[:, 0:SSM_DIM]
    bm = act[:, SSM_DIM:SSM_DIM + SSM_GROUPS * NS]
    cm = act[:, SSM_DIM + SSM_GROUPS * NS:]

    dt_all = _softplus(dt_ref[...] + dtb_ref[...])
    if t_valid < t_total:
        live = (c * GL + lax.broadcasted_iota(jnp.int32, dt_all.shape, 0)) < t_valid
        dt_all = jnp.where(live, dt_all, 0.0)
    a_neg = -jnp.exp(alog_ref[...])
    li = lax.broadcasted_iota(jnp.int32, (L, L), 0)
    si = lax.broadcasted_iota(jnp.int32, (L, L), 1)
    causal = li >= si
    rep = SSM_HEADS // SSM_GROUPS
    GW = rep * P
    bd_row = _div(lax.broadcasted_iota(jnp.int32, (rep * L, GW), 0), L)
    bd_lane = _div(lax.broadcasted_iota(jnp.int32, (rep * L, GW), 1), P)

    def lanes_of(cols):
        return jnp.concatenate([jnp.broadcast_to(x, (x.shape[0], P)) for x in cols], axis=1)

    parts = []
    for ci in range(G):
        rows = slice(ci * L, (ci + 1) * L)
        dt = dt_all[rows]
        acum = _cumsum_rows(dt * a_neg)
        acum_t = acum.T
        dt_t = dt.T
        per_group = []
        for g in range(SSM_GROUPS):
            heads = range(g * rep, (g + 1) * rep)
            x_g = xs[rows, g * GW:(g + 1) * GW]
            b_g = bm[rows, g * NS:(g + 1) * NS]
            c_g = cm[rows, g * NS:(g + 1) * NS]
            cb = _dot_nt(c_g, b_g)
            scores = [cb * jnp.exp(jnp.where(causal, acum[:, h:h + 1] - acum_t[h:h + 1, :], NEG_INF)) * dt_t[h:h + 1, :]
                      for h in heads]
            ac = lanes_of([acum[:, h:h + 1] for h in heads])
            a_last = lanes_of([acum[L - 1:L, h:h + 1] for h in heads])
            x_bd = jnp.where(bd_row == bd_lane, jnp.concatenate([x_g] * rep, axis=0), 0.0)
            y_own = _dot(jnp.concatenate(scores, axis=1), x_bd) + lanes_of([dskip_ref[:, h:h + 1] for h in heads]) * x_g
            to_end = jnp.exp(a_last - ac) * lanes_of([dt[:, h:h + 1] for h in heads])
            keep = jnp.concatenate([jnp.broadcast_to(jnp.exp(acum[L - 1:L, h:h + 1]), (P, NS)) for h in heads], axis=0)
            per_group.append((y_own, c_g, jnp.exp(ac), keep, _dot_tn(x_g * to_end, b_g)))
        parts.append(per_group)

    ys = [[None] * SSM_GROUPS for _ in range(G)]
    for g in range(SSM_GROUPS):
        h_g = hs_ref[g * rep:(g + 1) * rep].reshape(GW, NS)
        for ci in range(G):
            y_own, c_g, grow, keep, add = parts[ci][g]
            ys[ci][g] = y_own + _dot_nt(c_g, h_g) * grow
            h_g = h_g * keep + add
        hs_ref[g * rep:(g + 1) * rep] = h_g.reshape(rep, P, NS)
    y = jnp.concatenate([jnp.concatenate(row, axis=1) for row in ys], axis=0)
    z = z_ref[...]
    y = y * (z * _sigmoid(z))
    y_ref[...] = _rms(y, nw_ref[...])

    @pl.when(c == nc - 1)
    def _():
        ho_ref[0] = hs_ref[...]
        cv_ref[0] = conv_tail


def _ssd_call(z, xbc, dt, conv_prev8, h0, lp, n_batch, t_total, t_valid, chunk):
    L = chunk
    G = SSD_GROUP if (t_total // L) % SSD_GROUP == 0 else 1
    nc = t_total // (G * L)
    rows = lambda w: pl.BlockSpec((G * L, w), lambda b, c: (b * nc + c, 0))
    vec = lambda a, n: pl.BlockSpec((a, n), lambda b, c: (0, 0))
    in_specs = [rows(SSM_DIM), rows(SSM_CONV_DIM), rows(LANES),
                pl.BlockSpec((1, 8, SSM_CONV_DIM), lambda b, c: (b, 0, 0)),
                pl.BlockSpec((1, SSM_HEADS, HEAD_DIM, SSM_STATE), lambda b, c: (b, 0, 0, 0)),
                vec(SSM_CONV, SSM_CONV_DIM), vec(1, SSM_CONV_DIM), vec(1, LANES), vec(1, LANES), vec(1, LANES),
                vec(1, SSM_DIM)]
    out_shape = (jax.ShapeDtypeStruct((n_batch * t_total, SSM_DIM), F32),
                 jax.ShapeDtypeStruct((n_batch, SSM_HEADS, HEAD_DIM, SSM_STATE), F32),
                 jax.ShapeDtypeStruct((n_batch, 8, SSM_CONV_DIM), F32))
    out_specs = (rows(SSM_DIM), pl.BlockSpec((1, SSM_HEADS, HEAD_DIM, SSM_STATE), lambda b, c: (b, 0, 0, 0)),
                 pl.BlockSpec((1, 8, SSM_CONV_DIM), lambda b, c: (b, 0, 0)))
    return pl.pallas_call(
        functools.partial(_ssd_body, L, G, t_valid, t_total),
        grid=(n_batch, nc), in_specs=in_specs, out_specs=out_specs, out_shape=out_shape,
        scratch_shapes=[pltpu.VMEM((SSM_HEADS, HEAD_DIM, SSM_STATE), F32),
                        pltpu.VMEM((G * L + 8, SSM_CONV_DIM), F32)],
        compiler_params=_params(("arbitrary", "arbitrary")), name="ssd")(
            z, xbc, dt, conv_prev8, h0, lp["ssm_conv_w"], lp["ssm_conv_b"], lp["ssm_dt_bias"], lp["ssm_A_log"],
            lp["ssm_D"], lp["ssm_norm_w"])


def _pad_lanes(v, n):
    return jnp.pad(v, ((0, 0), (0, n - v.shape[-1])))


def _pad_rows(x, n):
    return jnp.pad(x, ((0, n - x.shape[0]),) + ((0, 0),) * (x.ndim - 1))


def kernel(x_prompt, x_sample, cache_att_k, cache_att_v, state_rwkv_shift, state_rwkv, state_ssm_conv, state_ssm, ffn1_norm, ffn1_w_gate, ffn1_w_up, ffn1_w_down, mix_norm, w_in, rwkv_mu, rwkv_w0, rwkv_w2, rwkv_a0, rwkv_a2, rwkv_g2, rwkv_k_k, rwkv_k_a, rwkv_r_k, rwkv_ln_w, rwkv_ln_b, att_norm_w, ssm_conv_w, ssm_conv_b, ssm_dt_bias, ssm_A_log, ssm_D, ssm_norm_w, w_out, ffn2_norm, ffn2_w_gate, ffn2_w_up, ffn2_w_down, final_norm):
    depth = w_in.shape[0]
    bp, tp, _ = x_prompt.shape
    bs, ts, _ = x_sample.shape
    w_buf = cache_att_k.shape[2]
    mp = bp * tp
    ms = bs * ts
    assert mp % ROW_TILE == 0 and ms <= ROW_TILE and tp % (ATT_STEPS * DILATIONS[-1]) == 0 and ts <= 8
    n_ptiles = mp // ROW_TILE
    ts_rwkv = RWKV_CHUNK
    ts_ssd = 8

    x = jnp.concatenate([x_prompt.reshape(mp, D_MODEL), _pad_rows(x_sample.reshape(ms, D_MODEL), ROW_TILE)], axis=0)
    zeros_shift = jnp.zeros((bp, RWKV_PROJ), F32)
    zeros_s = jnp.zeros((bp, RWKV_HEADS, HEAD_DIM, HEAD_DIM), F32)
    zeros_conv = jnp.zeros((bp, 8, SSM_CONV_DIM), F32)
    zeros_h = jnp.zeros((bp, SSM_HEADS, HEAD_DIM, SSM_STATE), F32)
    ffn1_w = tuple(w.astype(BF16) for w in (ffn1_w_gate, ffn1_w_up, ffn1_w_down))
    ffn2_w = tuple(w.astype(BF16) for w in (ffn2_w_gate, ffn2_w_up, ffn2_w_down))
    w_out_b = w_out.astype(BF16)
    w_in_b = jnp.pad(w_in, ((0, 0), (0, 0), (0, PROJ_PAD - w_in.shape[-1]))).astype(BF16)
    to_cp = lambda c: jnp.transpose(c, (0, 1, 3, 4, 2)).reshape(depth, bs, ATT_DIM, w_buf)
    cache_kt, cache_vt = to_cp(cache_att_k), to_cp(cache_att_v)

    outs = {k: [] for k in ("p_k", "p_v", "p_shift", "p_s", "p_conv", "p_h",
                            "s_k", "s_v", "s_shift", "s_s", "s_conv", "s_h")}
    mix = None
    wo = None
    y_prompt = y_sample = None
    for l in range(depth):
        row1 = lambda v: v[l].reshape(1, -1)
        lp = dict(rwkv_mu=row1(rwkv_mu), rwkv_w0=row1(rwkv_w0), rwkv_w2=rwkv_w2[l], rwkv_a0=row1(rwkv_a0),
                  rwkv_a2=rwkv_a2[l], rwkv_g2=rwkv_g2[l], rwkv_k_k=row1(rwkv_k_k), rwkv_k_a=row1(rwkv_k_a),
                  rwkv_r_k=row1(rwkv_r_k), rwkv_ln_w=row1(rwkv_ln_w), rwkv_ln_b=row1(rwkv_ln_b),
                  ssm_conv_w=ssm_conv_w[l], ssm_conv_b=row1(ssm_conv_b),
                  ssm_dt_bias=_pad_lanes(row1(ssm_dt_bias), LANES), ssm_A_log=_pad_lanes(row1(ssm_A_log), LANES),
                  ssm_D=_pad_lanes(row1(ssm_D), LANES), ssm_norm_w=row1(ssm_norm_w))
        if l == 0:
            x = _ffn_call(x, None, None, row1(ffn1_norm), *ffn1_w, None, n_ptiles, l)
        (p_rwkv, q3, k3, v3, k_cp, v_cp, z, xbc, dt, pr_t, q_s, k_s, v_s, z_t, xbc_t, dt_t) = _proj_call(
            x, row1(mix_norm), w_in_b, n_ptiles, tp, l)

        ya_p, s_p, last_p = _rwkv_call(p_rwkv, zeros_shift, zeros_s, lp, bp, tp, tp)
        yb_p = _att_prompt_call(q3, k3, v3, row1(att_norm_w), bp, tp)
        yc_p, h_p, conv_p = _ssd_call(z, xbc, dt, zeros_conv, zeros_h, lp, bp, tp, tp, SSD_CHUNK)

        def sample_rows(a, t_pad):
            return jnp.pad(a[:ms].reshape(bs, ts, -1), ((0, 0), (0, t_pad - ts), (0, 0)))

        pr_s = sample_rows(pr_t, ts_rwkv)
        ya_s, s_s, last_s = _rwkv_call(pr_s.reshape(bs * ts_rwkv, RWKV_PROJ), state_rwkv_shift[l], state_rwkv[l], lp,
                               bs, ts_rwkv, ts)
        yb_s = _att_sample_call(sample_rows(q_s, 8), sample_rows(k_s, 8), sample_rows(v_s, 8),
                                cache_kt, cache_vt, l, row1(att_norm_w), ts)
        conv8 = jnp.pad(state_ssm_conv[l], ((0, 0), (8 - (SSM_CONV - 1), 0), (0, 0)))
        flat8 = lambda a: sample_rows(a, ts_ssd).reshape(bs * ts_ssd, -1)
        yc_s, h_s, conv_s = _ssd_call(flat8(z_t), flat8(xbc_t), flat8(dt_t), conv8, state_ssm[l], lp, bs, ts_ssd, ts,
                                      ts_ssd)

        def tile_rows(a, t_pad):
            return _pad_rows(a.reshape(bs, t_pad, -1)[:, :ts].reshape(ms, -1), ROW_TILE)

        mix = ((ya_p, tile_rows(ya_s, ts_rwkv)), (yb_p, tile_rows(yb_s, 8)), (yc_p, tile_rows(yc_s, ts_ssd)))

        keep = min(2048, tp)
        from_cp = lambda a: jnp.transpose(a.reshape(bp, ATT_HEADS, HEAD_DIM, tp), (0, 3, 1, 2))[:, tp - keep:]
        outs["p_k"].append(from_cp(k_cp))
        outs["p_v"].append(from_cp(v_cp))
        outs["p_shift"].append(last_p[:, 0])
        outs["p_s"].append(s_p)
        outs["p_conv"].append(conv_p[:, 8 - (SSM_CONV - 1):])
        outs["p_h"].append(h_p)
        outs["s_k"].append(k_s[:ms].reshape(bs, ts, ATT_HEADS, HEAD_DIM))
        outs["s_v"].append(v_s[:ms].reshape(bs, ts, ATT_HEADS, HEAD_DIM))
        outs["s_shift"].append(last_s[:, 0])
        outs["s_s"].append(s_s)
        outs["s_conv"].append(conv_s[:, 8 - (SSM_CONV - 1):])
        outs["s_h"].append(h_s)

        x = _ffn_call(x, mix, w_out_b, row1(ffn2_norm), *ffn2_w,
                      final_norm.reshape(1, -1) if l == depth - 1 else None, n_ptiles, l)
        if l == depth - 1:
            y_prompt, y_sample = x
        else:
            x = _ffn_call(x, None, None, ffn1_norm[l + 1].reshape(1, -1), *ffn1_w, None, n_ptiles, l + 1)

    st = lambda k: jnp.stack(outs[k])
    return (y_prompt.reshape(bp, tp, D_MODEL), y_sample[:ms].reshape(bs, ts, D_MODEL),
            st("p_k"), st("p_v"), st("p_shift"), st("p_s"), st("p_conv"), st("p_h"),
            st("s_k"), st("s_v"), st("s_shift"), st("s_s"), st("s_conv"), st("s_h"))
```
